```python
import math
import jax, jax.numpy as jnp
from jax import lax
import numpy as np

D_MODEL = 2048
BATCH = 4
SEQ = 2048
DEPTH = 4
DEC_BATCH = 128
DEC_SEQ = 1
PAST_LEN = 16384
PAGE_SIZE = 128

BRANCH_WIDTH = 1024
N_BRANCH = 3
R_HEADS = 4
R_DK = BRANCH_WIDTH // R_HEADS
R_DV = BRANCH_WIDTH // R_HEADS
R_CHUNK = 128
ROPE_BASE = 10000.0
S5_GROUP = 16
S5_GROUPS = BRANCH_WIDTH // S5_GROUP
S5_STATE = 64
G_HEADS = 8
G_DK = BRANCH_WIDTH // G_HEADS
G_DV = BRANCH_WIDTH // G_HEADS
G_CONV = 4
G_CHUNK = 64
D_FF = ((8 * D_MODEL // 3 + 255) // 256) * 256
NORM_EPS = 1e-6
N_IN = 4 * BRANCH_WIDTH + BRANCH_WIDTH + 3 * BRANCH_WIDTH + 2 * G_HEADS + BRANCH_WIDTH + N_BRANCH * D_MODEL

kernel_name = 'hybrid_retention_s5_gdn_decoder_step'


def rmsnorm(x, w):
    xf = x.astype(jnp.float32)
    y = xf * lax.rsqrt(jnp.mean(xf * xf, -1, keepdims=True) + NORM_EPS)
    return (y * w.astype(jnp.float32)).astype(x.dtype)


def _head_rms(o):
    return o * lax.rsqrt(jnp.mean(o * o, -1, keepdims=True) + NORM_EPS)


def _l2norm(x):
    return x * lax.rsqrt(jnp.sum(x * x, -1, keepdims=True) + NORM_EPS)


def _split_proj(proj):
    sizes = [BRANCH_WIDTH] * 4 + [BRANCH_WIDTH, 3 * BRANCH_WIDTH, G_HEADS, G_HEADS, BRANCH_WIDTH, N_BRANCH * D_MODEL]
    outs, off = [], 0
    for s in sizes:
        outs.append(proj[..., off:off + s])
        off += s
    return outs


def _chunk_len(l, c):
    return c if l % c == 0 else l


def _to_chunks(a, c):
    b, h, l = a.shape[:3]
    return jnp.moveaxis(a.reshape(b, h, l // c, c, *a.shape[3:]), 2, 0)


def _from_chunks(o):
    n, b, h, c = o.shape[:4]
    return jnp.moveaxis(o, 0, 2).reshape(b, h, n * c, *o.shape[4:])


def rotary(x, pos):
    half = x.shape[-1] // 2
    inv = ROPE_BASE ** (-jnp.arange(half, dtype=jnp.float32) / half)
    ang = pos.astype(jnp.float32)[:, None] * inv[None, :]
    cos = jnp.cos(ang)[None, :, None, :]
    sin = jnp.sin(ang)[None, :, None, :]
    x1, x2 = x[..., :half], x[..., half:]
    return jnp.concatenate([x1 * cos - x2 * sin, x1 * sin + x2 * cos], -1)


def _retention_chunk(S, q, k, v, log_gamma):
    c = q.shape[2]
    idx = jnp.arange(c, dtype=jnp.float32)
    diff = idx[:, None] - idx[None, :]
    decay = jnp.where(diff >= 0, jnp.exp(log_gamma[..., None] * jnp.maximum(diff, 0.0)), 0.0)
    scores = jnp.einsum('bhid,bhjd->bhij', q, k) * decay[None]
    o = jnp.einsum('bhij,bhjv->bhiv', scores, v)
    o = o + jnp.einsum('bhid,bhdv->bhiv', q * jnp.exp(log_gamma * (idx + 1.0))[None, :, :, None], S)
    k_dec = k * jnp.exp(log_gamma * (c - 1.0 - idx))[None, :, :, None]
    S_new = jnp.exp(log_gamma * c)[None, :, :, None] * S + jnp.einsum('bhjd,bhjv->bhdv', k_dec, v)
    return S_new, o


def retention(q, k, v, g, S0, pos):
    b, l, _ = q.shape
    log_gamma = jnp.log1p(-jnp.exp2(-5.0 - jnp.arange(R_HEADS, dtype=jnp.float32)))[:, None]
    q = rotary(q.reshape(b, l, R_HEADS, R_DK), pos).transpose(0, 2, 1, 3)
    k = (rotary(k.reshape(b, l, R_HEADS, R_DK), pos) * R_DK ** -0.5).transpose(0, 2, 1, 3)
    v = v.reshape(b, l, R_HEADS, R_DV).transpose(0, 2, 1, 3)
    c = _chunk_len(l, R_CHUNK)
    S, o = lax.scan(lambda s, xs: _retention_chunk(s, xs[0], xs[1], xs[2], log_gamma),
                    S0.astype(jnp.float32), (_to_chunks(q, c), _to_chunks(k, c), _to_chunks(v, c)))
    o = _head_rms(_from_chunks(o)).transpose(0, 2, 1, 3).reshape(b, l, BRANCH_WIDTH)
    return jax.nn.silu(g) * o, S


def _linear_combine(e1, e2):
    a1, b1 = e1
    a2, b2 = e2
    return a1 * a2, a2 * b1 + b2


def s5_mixer(u, x0_re, x0_im, lam_re, lam_im, log_dt, b_re, b_im, c_re, c_im, d_skip, w_glu, b_glu):
    f32 = jnp.float32
    b, l, _ = u.shape
    ug = u.reshape(b, l, S5_GROUPS, S5_GROUP)
    lam = lax.complex(lam_re.astype(f32), lam_im.astype(f32))
    dt = jnp.exp(log_dt.astype(f32))[:, None]
    lam_bar = jnp.exp(lam * dt)
    b_bar = ((lam_bar - 1.0) / lam)[..., None] * lax.complex(b_re.astype(f32), b_im.astype(f32))
    bu = jnp.einsum('gpc,blgc->blgp', b_bar, ug.astype(jnp.complex64))
    a = jnp.broadcast_to(lam_bar, bu.shape)
    a_cum, xs = lax.associative_scan(_linear_combine, (a, bu), axis=1)
    xs = xs + a_cum * lax.complex(x0_re.astype(f32), x0_im.astype(f32))[:, None]
    cm = lax.complex(c_re.astype(f32), c_im.astype(f32))
    y = jnp.einsum('gcp,blgp->blgc', cm, xs).real + d_skip.astype(f32).reshape(S5_GROUPS, S5_GROUP) * ug
    h = jax.nn.gelu(y.reshape(b, l, BRANCH_WIDTH))
    out = h * jax.nn.sigmoid(h @ w_glu.astype(f32) + b_glu.astype(f32))
    return out, xs[:, -1].real, xs[:, -1].imag


def _short_conv(x, buf, w):
    l = x.shape[1]
    xp = jnp.concatenate([buf.astype(jnp.float32), x], axis=1)
    y = xp[:, 0:l] * w[0]
    for i in range(1, G_CONV):
        y = y + xp[:, i:i + l] * w[i]
    return jax.nn.silu(y), xp[:, -(G_CONV - 1):]


def _gdn_chunk(S, q, k, v, g, beta):
    c = q.shape[2]
    G = jnp.cumsum(g, axis=-1)
    causal = jnp.tril(jnp.ones((c, c), bool))
    strict = jnp.tril(jnp.ones((c, c), bool), -1)
    diff = G[..., :, None] - G[..., None, :]
    decay = jnp.where(causal, jnp.exp(jnp.where(causal, diff, 0.0)), 0.0)
    a = jnp.where(strict, beta[..., :, None] * jnp.einsum('bhid,bhjd->bhij', k, k) * decay, 0.0)
    t_mat = a + jnp.eye(c, dtype=jnp.float32)
    rhs = jnp.concatenate([beta[..., None] * v, (beta * jnp.exp(G))[..., None] * k], -1)
    sol = lax.linalg.triangular_solve(t_mat, rhs, left_side=True, lower=True, unit_diagonal=True)
    w = sol[..., :G_DV] - jnp.einsum('bhid,bhdv->bhiv', sol[..., G_DV:], S)
    qk = jnp.einsum('bhid,bhjd->bhij', q, k) * decay
    o = jnp.einsum('bhid,bhdv->bhiv', q * jnp.exp(G)[..., None], S) + jnp.einsum('bhij,bhjv->bhiv', qk, w)
    gl = G[..., -1:]
    S_new = jnp.exp(gl)[..., None] * S + jnp.einsum('bhjd,bhjv->bhdv', k * jnp.exp(gl - G)[..., None], w)
    return S_new, o


def gated_deltanet(qkv, a_in, b_in, z, conv_buf, S0, conv_w, a_log, dt_bias, norm_w):
    b, l, _ = qkv.shape
    qkv, new_buf = _short_conv(qkv, conv_buf, conv_w)
    q = qkv[..., :BRANCH_WIDTH].reshape(b, l, G_HEADS, G_DK)
    k = qkv[..., BRANCH_WIDTH:2 * BRANCH_WIDTH].reshape(b, l, G_HEADS, G_DK)
    v = qkv[..., 2 * BRANCH_WIDTH:].reshape(b, l, G_HEADS, G_DV)
    q = (_l2norm(q) * G_DK ** -0.5).transpose(0, 2, 1, 3)
    k = _l2norm(k).transpose(0, 2, 1, 3)
    v = v.transpose(0, 2, 1, 3)
    g = (-jnp.exp(a_log.astype(jnp.float32)) * jax.nn.softplus(a_in + dt_bias.astype(jnp.float32))).transpose(0, 2, 1)
    beta = jax.nn.sigmoid(b_in).transpose(0, 2, 1)
    c = _chunk_len(l, G_CHUNK)
    S, o = lax.scan(lambda s, xs: _gdn_chunk(s, *xs), S0.astype(jnp.float32),
                    (_to_chunks(q, c), _to_chunks(k, c), _to_chunks(v, c), _to_chunks(g, c), _to_chunks(beta, c)))
    o = _from_chunks(o).transpose(0, 2, 1, 3)
    o = _head_rms(o) * norm_w.astype(jnp.float32) * jax.nn.silu(z.reshape(b, l, G_HEADS, G_DV))
    return o.reshape(b, l, BRANCH_WIDTH), S, new_buf


def _block(x, pos, st_ret, st_re, st_im, st_gdn, st_conv,
           norm_mix_w, w_in, lam_re, lam_im, log_dt, b_re, b_im, c_re, c_im, d_skip, w_glu, b_glu,
           conv_w, a_log, dt_bias, gnorm_w, w_branch, w_out, norm_ffn_w, w_gate, w_up, w_down):
    b, l, _ = x.shape
    xn = rmsnorm(x, norm_mix_w)
    proj = (xn @ w_in).astype(jnp.float32)
    rq, rk, rv, rg, su, gqkv, ga, gb, gz, gates = _split_proj(proj)
    r_out, r_S = retention(rq, rk, rv, rg, st_ret, pos)
    s_out, s_re, s_im = s5_mixer(su, st_re, st_im, lam_re, lam_im, log_dt, b_re, b_im, c_re, c_im, d_skip, w_glu, b_glu)
    g_out, g_S, g_buf = gated_deltanet(gqkv, ga, gb, gz, st_conv, st_gdn, conv_w, a_log, dt_bias, gnorm_w)
    branches = jnp.stack([r_out, s_out, g_out], axis=2).astype(x.dtype)
    up = jnp.einsum('blnc,ncd->blnd', branches, w_branch)
    gate = jax.nn.sigmoid(gates.reshape(b, l, N_BRANCH, D_MODEL)).astype(x.dtype)
    merged = jnp.sum(gate * up, axis=2)
    h = x + merged @ w_out
    hn = rmsnorm(h, norm_ffn_w)
    y = h + (jax.nn.silu(hn @ w_gate) * (hn @ w_up)) @ w_down
    return y, r_S, s_re, s_im, g_S, g_buf


def setup_inputs(seed: int = 0) -> dict:
    key = jax.random.key(seed)
    ks = iter(jax.random.split(key, 48))
    f32 = jnp.float32

    def nrm(shape, scale):
        return jax.random.normal(next(ks), shape, f32) * scale

    x_prompt = nrm((BATCH, SEQ, D_MODEL), 1.0)
    x_sample = nrm((DEC_BATCH, DEC_SEQ, D_MODEL), 1.0)
    state_ret = nrm((DEPTH, DEC_BATCH, R_HEADS, R_DK, R_DV), 1.0)
    state_s5_re = nrm((DEPTH, DEC_BATCH, S5_GROUPS, S5_STATE), 0.1)
    state_s5_im = nrm((DEPTH, DEC_BATCH, S5_GROUPS, S5_STATE), 0.1)
    state_gdn = nrm((DEPTH, DEC_BATCH, G_HEADS, G_DK, G_DV), 0.3)
    state_gdn_conv = nrm((DEPTH, DEC_BATCH, G_CONV - 1, 3 * BRANCH_WIDTH), 1.0)
    norm_mix_w = 1.0 + nrm((DEPTH, D_MODEL), 0.02)
    w_in = nrm((DEPTH, D_MODEL, N_IN), D_MODEL ** -0.5)
    s5_lambda_re = -0.5 + nrm((DEPTH, S5_GROUPS, S5_STATE), 0.01)
    s5_lambda_im = math.pi * jnp.arange(S5_STATE, dtype=f32) + nrm((DEPTH, S5_GROUPS, S5_STATE), 0.01)
    s5_log_dt = jax.random.uniform(next(ks), (DEPTH, S5_GROUPS), f32, math.log(1e-3), math.log(1e-1))
    s5_b_re = nrm((DEPTH, S5_GROUPS, S5_STATE, S5_GROUP), (2 * S5_GROUP) ** -0.5)
    s5_b_im = nrm((DEPTH, S5_GROUPS, S5_STATE, S5_GROUP), (2 * S5_GROUP) ** -0.5)
    s5_c_re = nrm((DEPTH, S5_GROUPS, S5_GROUP, S5_STATE), 0.5)
    s5_c_im = nrm((DEPTH, S5_GROUPS, S5_GROUP, S5_STATE), 0.5)
    s5_d = nrm((DEPTH, BRANCH_WIDTH), 1.0)
    s5_w_glu = nrm((DEPTH, BRANCH_WIDTH, BRANCH_WIDTH), BRANCH_WIDTH ** -0.5)
    s5_b_glu = nrm((DEPTH, BRANCH_WIDTH), 0.01)
    gdn_conv_w = nrm((DEPTH, G_CONV, 3 * BRANCH_WIDTH), G_CONV ** -0.5)
    gdn_a_log = jnp.log(jax.random.uniform(next(ks), (DEPTH, G_HEADS), f32, 1.0, 16.0))
    dt = jnp.exp(jax.random.uniform(next(ks), (DEPTH, G_HEADS), f32, math.log(1e-3), math.log(1e-1)))
    gdn_dt_bias = dt + jnp.log(-jnp.expm1(-dt))
    gdn_norm_w = 1.0 + nrm((DEPTH, G_DV), 0.02)
    w_branch = nrm((DEPTH, N_BRANCH, BRANCH_WIDTH, D_MODEL), BRANCH_WIDTH ** -0.5)
    w_out = nrm((DEPTH, D_MODEL, D_MODEL), (D_MODEL ** -0.5) / math.sqrt(2 * DEPTH))
    norm_ffn_w = 1.0 + nrm((DEPTH, D_MODEL), 0.02)
    w_gate_ffn = nrm((DEPTH, D_MODEL, D_FF), D_MODEL ** -0.5)
    w_up_ffn = nrm((DEPTH, D_MODEL, D_FF), D_MODEL ** -0.5)
    w_down_ffn = nrm((DEPTH, D_FF, D_MODEL), (D_FF ** -0.5) / math.sqrt(2 * DEPTH))
    norm_final_w = 1.0 + nrm((D_MODEL,), 0.02)
    return {'x_prompt': x_prompt, 'x_sample': x_sample,
            'state_ret': state_ret, 'state_s5_re': state_s5_re, 'state_s5_im': state_s5_im,
            'state_gdn': state_gdn, 'state_gdn_conv': state_gdn_conv,
            'norm_mix_w': norm_mix_w, 'w_in': w_in,
            's5_lambda_re': s5_lambda_re, 's5_lambda_im': s5_lambda_im, 's5_log_dt': s5_log_dt,
            's5_b_re': s5_b_re, 's5_b_im': s5_b_im, 's5_c_re': s5_c_re, 's5_c_im': s5_c_im,
            's5_d': s5_d, 's5_w_glu': s5_w_glu, 's5_b_glu': s5_b_glu,
            'gdn_conv_w': gdn_conv_w, 'gdn_a_log': gdn_a_log, 'gdn_dt_bias': gdn_dt_bias, 'gdn_norm_w': gdn_norm_w,
            'w_branch': w_branch, 'w_out': w_out, 'norm_ffn_w': norm_ffn_w,
            'w_gate_ffn': w_gate_ffn, 'w_up_ffn': w_up_ffn, 'w_down_ffn': w_down_ffn,
            'norm_final_w': norm_final_w}


def reference(x_prompt, x_sample, state_ret, state_s5_re, state_s5_im, state_gdn, state_gdn_conv,
              norm_mix_w, w_in, s5_lambda_re, s5_lambda_im, s5_log_dt, s5_b_re, s5_b_im, s5_c_re, s5_c_im,
              s5_d, s5_w_glu, s5_b_glu, gdn_conv_w, gdn_a_log, gdn_dt_bias, gdn_norm_w,
              w_branch, w_out, norm_ffn_w, w_gate_ffn, w_up_ffn, w_down_ffn, norm_final_w):
    f32 = jnp.float32
    bp, lp, _ = x_prompt.shape
    pos_p = jnp.arange(lp, dtype=jnp.int32)
    pos_s = PAST_LEN + jnp.arange(x_sample.shape[1], dtype=jnp.int32)
    z_ret = jnp.zeros((bp, R_HEADS, R_DK, R_DV), f32)
    z_ssm = jnp.zeros((bp, S5_GROUPS, S5_STATE), f32)
    z_gdn = jnp.zeros((bp, G_HEADS, G_DK, G_DV), f32)
    z_conv = jnp.zeros((bp, G_CONV - 1, 3 * BRANCH_WIDTH), f32)
    xp, xs = x_prompt, x_sample
    new_p = [[], [], [], [], []]
    new_s = [[], [], [], [], []]
    for l in range(DEPTH):
        lw = (norm_mix_w[l], w_in[l], s5_lambda_re[l], s5_lambda_im[l], s5_log_dt[l], s5_b_re[l], s5_b_im[l],
              s5_c_re[l], s5_c_im[l], s5_d[l], s5_w_glu[l], s5_b_glu[l], gdn_conv_w[l], gdn_a_log[l],
              gdn_dt_bias[l], gdn_norm_w[l], w_branch[l], w_out[l], norm_ffn_w[l], w_gate_ffn[l],
              w_up_ffn[l], w_down_ffn[l])
        xp, *sp = _block(xp, pos_p, z_ret, z_ssm, z_ssm, z_gdn, z_conv, *lw)
        xs, *ss = _block(xs, pos_s, state_ret[l], state_s5_re[l], state_s5_im[l], state_gdn[l], state_gdn_conv[l], *lw)
        for i in range(5):
            new_p[i].append(sp[i])
            new_s[i].append(ss[i])
    y_prompt = rmsnorm(xp, norm_final_w)
    y_sample = rmsnorm(xs, norm_final_w)
    ret_p, s5re_p, s5im_p, gdn_p, conv_p = [jnp.stack(a, 0) for a in new_p]
    ret_s, s5re_s, s5im_s, gdn_s, conv_s = [jnp.stack(a, 0) for a in new_s]
    return (y_prompt, y_sample, ret_p, s5re_p, s5im_p, gdn_p, conv_p, ret_s, s5re_s, s5im_s, gdn_s, conv_s)
```

```python
import functools
import math

import jax
import jax.numpy as jnp
from jax import lax
from jax.experimental import pallas as pl
from jax.experimental.pallas import tpu as pltpu

F32 = jnp.float32
BF16 = jnp.bfloat16

D_MODEL = 2048
BRANCH_WIDTH = 1024
N_BRANCH = 3
R_HEADS = 4
R_DK = 256
R_CHUNK = 128
ROPE_BASE = 10000.0
S5_GROUP = 16
S5_GROUPS = 64
S5_STATE = 64
S5_CHUNK = 16
G_HEADS = 8
G_DK = 128
G_CONV = 4
G_CHUNK = 64
NORM_EPS = 1e-6
PAST_LEN = 16384

VMEM_LIMIT_BYTES = 56 * 1024 * 1024
TM = 640


def _cparams(sem):
    return pltpu.CompilerParams(dimension_semantics=sem, vmem_limit_bytes=VMEM_LIMIT_BYTES)


_NN = (((1,), (0,)), ((), ()))
_NT = (((1,), (1,)), ((), ()))
_TN = (((0,), (0,)), ((), ()))


def _dg(a, b, dims):
    return lax.dot_general(a, b, dims, preferred_element_type=F32)


def _split2(a):
    hi = a.astype(BF16)
    lo = (a - hi.astype(F32)).astype(BF16)
    return hi, lo


def _dot3(a, b, dims=_NN):
    ah, al = _split2(a)
    bh, bl = _split2(b)
    return _dg(ah, bh, dims) + (_dg(ah, bl, dims) + _dg(al, bh, dims))


def _dot_sel(sel_bf16, x, dims=_NN):
    h1 = x.astype(BF16)
    r1 = x - h1.astype(F32)
    h2 = r1.astype(BF16)
    h3 = (r1 - h2.astype(F32)).astype(BF16)
    return _dg(sel_bf16, h1, dims) + (_dg(sel_bf16, h2, dims) + _dg(sel_bf16, h3, dims))


def _sel_dot(x, sel_bf16, dims=_NN):
    h1 = x.astype(BF16)
    r1 = x - h1.astype(F32)
    h2 = r1.astype(BF16)
    h3 = (r1 - h2.astype(F32)).astype(BF16)
    return _dg(h1, sel_bf16, dims) + (_dg(h2, sel_bf16, dims) + _dg(h3, sel_bf16, dims))


def _eye(n, dtype=BF16):
    r = lax.broadcasted_iota(jnp.int32, (n, n), 0)
    c = lax.broadcasted_iota(jnp.int32, (n, n), 1)
    return jnp.where(r == c, 1.0, 0.0).astype(dtype)


def _sigmoid(x):
    return 1.0 / (1.0 + jnp.exp(-x))


def _silu(x):
    return x * _sigmoid(x)


def _gelu_tanh(x):
    return 0.5 * x * (1.0 + jnp.tanh(math.sqrt(2.0 / math.pi) * (x + 0.044715 * (x * x * x))))


def _softplus(x):
    return jnp.maximum(x, 0.0) + jnp.log1p(jnp.exp(-jnp.abs(x)))


def _rms_rows(x, eps=NORM_EPS):
    return x * lax.rsqrt(jnp.mean(x * x, axis=-1, keepdims=True) + eps)


def _norm_kernel(x_ref, w_ref, o_ref):
    o_ref[...] = (_rms_rows(x_ref[...]) * w_ref[...]).astype(o_ref.dtype)


def rmsnorm_rows(x, w, out_dtype, tm=TM):
    t, d = x.shape
    return pl.pallas_call(
        _norm_kernel,
        grid=(t // tm,),
        in_specs=[pl.BlockSpec((tm, d), lambda i: (i, 0)), pl.BlockSpec((1, d), lambda i: (0, 0))],
        out_specs=pl.BlockSpec((tm, d), lambda i: (i, 0)),
        out_shape=jax.ShapeDtypeStruct((t, d), out_dtype),
        compiler_params=_cparams(("parallel",)),
        name="rmsnorm",
    )(x, w.reshape(1, d))


def _mm_kernel(a_ref, w_ref, o_ref):
    o_ref[...] = jnp.dot(a_ref[...], w_ref[...], preferred_element_type=F32)


def matmul(a, w, tn, tm=TM, name="proj"):
    t, k = a.shape
    n = w.shape[1]
    return pl.pallas_call(
        _mm_kernel,
        grid=(t // tm, n // tn),
        in_specs=[pl.BlockSpec((tm, k), lambda i, j: (i, 0)), pl.BlockSpec((k, tn), lambda i, j: (0, j))],
        out_specs=pl.BlockSpec((tm, tn), lambda i, j: (i, j)),
        out_shape=jax.ShapeDtypeStruct((t, n), F32),
        compiler_params=_cparams(("parallel", "arbitrary")),
        name=name,
    )(a, w)


def _merge_kernel(xn_ref, br_r, br_s, br_g, wg0, wg1, wg2, wb_ref, o_ref):
    xn = xn_ref[...]
    acc = None
    for i, (br, wg) in enumerate(((br_r, wg0), (br_s, wg1), (br_g, wg2))):
        gate = _sigmoid(jnp.dot(xn, wg[...], preferred_element_type=F32))
        up = jnp.dot(br[...], wb_ref[i], preferred_element_type=F32)
        acc = gate * up if acc is None else acc + gate * up
    o_ref[...] = acc.astype(o_ref.dtype)


def merge_branches(xn, br_r, br_s, br_g, w_gate, w_branch, tn=512, tm=TM):
    t, d = xn.shape
    bw = br_r.shape[1]
    nj = d // tn
    a_spec = pl.BlockSpec((tm, bw), lambda i, j: (i, 0))
    wg_specs = [pl.BlockSpec((d, tn), functools.partial(lambda i, j, n: (0, n * nj + j), n=n))
                for n in range(N_BRANCH)]
    return pl.pallas_call(
        _merge_kernel,
        grid=(t // tm, nj),
        in_specs=[pl.BlockSpec((tm, d), lambda i, j: (i, 0)), a_spec, a_spec, a_spec, *wg_specs,
                  pl.BlockSpec((N_BRANCH, bw, tn), lambda i, j: (0, 0, j))],
        out_specs=pl.BlockSpec((tm, tn), lambda i, j: (i, j)),
        out_shape=jax.ShapeDtypeStruct((t, d), BF16),
        compiler_params=_cparams(("parallel", "arbitrary")),
        name="merge",
    )(xn, br_r, br_s, br_g, w_gate, w_gate, w_gate, w_branch)


def _outproj_kernel(m_ref, w_ref, x_ref, nw_ref, h_ref, hn_ref):
    h = x_ref[...] + jnp.dot(m_ref[...], w_ref[...], preferred_element_type=F32)
    h_ref[...] = h
    hn_ref[...] = (_rms_rows(h) * nw_ref[...]).astype(hn_ref.dtype)


def out_proj(merged, w_out, x, norm_w, tm=TM):
    t, d = x.shape
    row = pl.BlockSpec((tm, d), lambda i: (i, 0))
    return pl.pallas_call(
        _outproj_kernel,
        grid=(t // tm,),
        in_specs=[row, pl.BlockSpec((d, d), lambda i: (0, 0)), row, pl.BlockSpec((1, d), lambda i: (0, 0))],
        out_specs=[row, row],
        out_shape=[jax.ShapeDtypeStruct((t, d), F32), jax.ShapeDtypeStruct((t, d), BF16)],
        compiler_params=_cparams(("parallel",)),
        name="outproj",
    )(merged, w_out, x, norm_w.reshape(1, d))


def _ffn_kernel(hn_ref, h_ref, wg_ref, wu_ref, wd_ref, nw_ref, y_ref, xn_ref, *, nf):
    f = pl.program_id(1)

    @pl.when(f == 0)
    def _():
        y_ref[...] = h_ref[...]

    hn = hn_ref[...]
    g = jnp.dot(hn, wg_ref[...], preferred_element_type=F32)
    u = jnp.dot(hn, wu_ref[...], preferred_element_type=F32)
    act = (_silu(g) * u).astype(BF16)
    y_ref[...] += jnp.dot(act, wd_ref[...], preferred_element_type=F32)

    @pl.when(f == nf - 1)
    def _():
        xn_ref[...] = (_rms_rows(y_ref[...]) * nw_ref[...]).astype(xn_ref.dtype)


def ffn(hn, h, w_gate, w_up, w_down, next_norm_w, xn_dtype, tf=512, tm=TM):
    t, d = h.shape
    dff = w_gate.shape[1]
    nf = dff // tf
    row = pl.BlockSpec((tm, d), lambda i, f: (i, 0))
    return pl.pallas_call(
        functools.partial(_ffn_kernel, nf=nf),
        grid=(t // tm, nf),
        in_specs=[row, row, pl.BlockSpec((d, tf), lambda i, f: (0, f)), pl.BlockSpec((d, tf), lambda i, f: (0, f)),
                  pl.BlockSpec((tf, d), lambda i, f: (f, 0)), pl.BlockSpec((1, d), lambda i, f: (0, 0))],
        out_specs=[row, row],
        out_shape=[jax.ShapeDtypeStruct((t, d), F32), jax.ShapeDtypeStruct((t, d), xn_dtype)],
        compiler_params=_cparams(("parallel", "arbitrary")),
        name="ffn",
    )(hn, h, w_gate, w_up, w_down, next_norm_w.reshape(1, d))


def _glu_kernel(h_ref, w_ref, b_ref, o_ref):
    h = h_ref[...]
    z = jnp.dot(h.astype(BF16), w_ref[...], preferred_element_type=F32) + b_ref[...]
    o_ref[...] = (h * _sigmoid(z)).astype(o_ref.dtype)


def s5_glu(h, w_glu, b_glu, tm=TM):
    t, d = h.shape
    row = pl.BlockSpec((tm, d), lambda i: (i, 0))
    return pl.pallas_call(
        _glu_kernel,
        grid=(t // tm,),
        in_specs=[row, pl.BlockSpec((d, d), lambda i: (0, 0)), pl.BlockSpec((1, d), lambda i: (0, 0))],
        out_specs=row,
        out_shape=jax.ShapeDtypeStruct((t, d), BF16),
        compiler_params=_cparams(("parallel",)),
        name="s5_glu",
    )(h, w_glu, b_glu.reshape(1, d))


def _rotary(x, cos, sin):
    half = x.shape[-1] // 2
    x1, x2 = x[:, :half], x[:, half:]
    return jnp.concatenate([x1 * cos - x2 * sin, x1 * sin + x2 * cos], axis=-1)


def _ret_prompt_kernel(lg_ref, q_ref, k_ref, v_ref, g_ref, cos_ref, sin_ref, o_ref, s_out_ref, s_ref, *, nc, chunk):
    h = pl.program_id(1)
    c = pl.program_id(2)
    lg = lg_ref[h]

    @pl.when(c == 0)
    def _():
        s_ref[...] = jnp.zeros_like(s_ref)

    cos, sin = cos_ref[...], sin_ref[...]
    q = _rotary(q_ref[...], cos, sin)
    k = _rotary(k_ref[...], cos, sin) * (R_DK ** -0.5)
    v = v_ref[...]
    ii = lax.broadcasted_iota(jnp.int32, (chunk, chunk), 0)
    jj = lax.broadcasted_iota(jnp.int32, (chunk, chunk), 1)
    diff = (ii - jj).astype(F32)
    decay = jnp.where(diff >= 0, jnp.exp(lg * jnp.maximum(diff, 0.0)), 0.0)
    idx = lax.broadcasted_iota(jnp.int32, (chunk, 1), 0).astype(F32)
    s = s_ref[...]
    scores = _dot3(q, k, _NT) * decay
    o = _dot3(scores, v) + _dot3(q * jnp.exp(lg * (idx + 1.0)), s)
    k_dec = k * jnp.exp(lg * (chunk - 1.0 - idx))
    s_new = jnp.exp(lg * jnp.full((1, 1), float(chunk), F32)) * s + _dot3(k_dec, v, _TN)
    s_ref[...] = s_new
    o_ref[...] = (_silu(g_ref[...]) * _rms_rows(o)).astype(o_ref.dtype)

    @pl.when(c == nc - 1)
    def _():
        s_out_ref[0, 0] = s_new


def retention_prompt(p_ret, cos, sin, log_gamma, batch, seq, chunk=R_CHUNK):
    nc = seq // chunk
    rows = lambda off: pl.BlockSpec((chunk, R_DK), functools.partial(
        lambda b, h, c, off: (b * nc + c, off + h), off=off))
    tab = pl.BlockSpec((chunk, R_DK // 2), lambda b, h, c: (c, 0))
    return pl.pallas_call(
        functools.partial(_ret_prompt_kernel, nc=nc, chunk=chunk),
        grid=(batch, R_HEADS, nc),
        in_specs=[pl.BlockSpec(memory_space=pltpu.SMEM), rows(0), rows(R_HEADS), rows(2 * R_HEADS),
                  rows(3 * R_HEADS), tab, tab],
        out_specs=[pl.BlockSpec((chunk, R_DK), lambda b, h, c: (b * nc + c, h)),
                   pl.BlockSpec((1, 1, R_DK, R_DK), lambda b, h, c: (b, h, 0, 0))],
        out_shape=[jax.ShapeDtypeStruct((batch * seq, BRANCH_WIDTH), BF16),
                   jax.ShapeDtypeStruct((batch, R_HEADS, R_DK, R_DK), F32)],
        scratch_shapes=[pltpu.VMEM((R_DK, R_DK), F32)],
        compiler_params=_cparams(("parallel", "parallel", "arbitrary")),
        name="ret_prompt",
    )(log_gamma, p_ret, p_ret, p_ret, p_ret, cos, sin)


def _ret_step_kernel(lg_ref, q_ref, k_ref, v_ref, g_ref, cos_ref, sin_ref, s_ref, o_ref, s_out_ref, *, bb):
    h = pl.program_id(0)
    gamma = jnp.exp(lg_ref[h] * jnp.ones((1, 1), F32))
    cos, sin = cos_ref[...], sin_ref[...]
    q = _rotary(q_ref[...], cos, sin)
    k = _rotary(k_ref[...], cos, sin) * (R_DK ** -0.5)
    v = v_ref[...]
    eye = _eye(R_DK)
    q_cols = _dot_sel(eye, q, _NT)
    k_cols = _dot_sel(eye, k, _NT)
    for i in range(bb):
        s_new = gamma * s_ref[i, 0] + k_cols[:, i:i + 1] * v[i:i + 1, :]
        s_out_ref[i, 0] = s_new
        o = jnp.sum(q_cols[:, i:i + 1] * s_new, axis=0, keepdims=True)
        o_ref[i:i + 1, :] = (_silu(g_ref[i:i + 1, :]) * _rms_rows(o)).astype(o_ref.dtype)


def retention_step(p_ret, row0, cos, sin, log_gamma, state, bb=8):
    nb = state.shape[0]
    r0 = row0 // bb
    rows = lambda off: pl.BlockSpec((bb, R_DK), functools.partial(lambda h, j, off: (r0 + j, off + h), off=off))
    tab = pl.BlockSpec((1, R_DK // 2), lambda h, j: (0, 0))
    st = pl.BlockSpec((bb, 1, R_DK, R_DK), lambda h, j: (j, h, 0, 0))
    return pl.pallas_call(
        functools.partial(_ret_step_kernel, bb=bb),
        grid=(R_HEADS, nb // bb),
        in_specs=[pl.BlockSpec(memory_space=pltpu.SMEM), rows(0), rows(R_HEADS), rows(2 * R_HEADS),
                  rows(3 * R_HEADS), tab, tab, st],
        out_specs=[pl.BlockSpec((bb, R_DK), lambda h, j: (j, h)), st],
        out_shape=[jax.ShapeDtypeStruct((nb, BRANCH_WIDTH), BF16), jax.ShapeDtypeStruct(state.shape, F32)],
        compiler_params=_cparams(("parallel", "parallel")),
        name="ret_step",
    )(log_gamma, p_ret, p_ret, p_ret, p_ret, cos, sin, state)


def _cmul(ar, ai, br, bi):
    return ar * br - ai * bi, ar * bi + ai * br


def s5_discretize(lam_re, lam_im, log_dt, b_re, b_im):
    dt = jnp.exp(log_dt)[:, None]
    mag = jnp.exp(lam_re * dt)
    lbr, lbi = mag * jnp.cos(lam_im * dt), mag * jnp.sin(lam_im * dt)
    den = lam_re * lam_re + lam_im * lam_im
    nr, ni = lbr - 1.0, lbi
    cr, ci = (nr * lam_re + ni * lam_im) / den, (ni * lam_re - nr * lam_im) / den
    bbr, bbi = _cmul(cr[..., None], ci[..., None], b_re, b_im)
    return dt, lbr, lbi, bbr, bbi


def _lam_power(lam_re, lam_im, dt, n):
    n = jnp.asarray(n, F32).reshape((-1,) + (1,) * lam_re.ndim)
    mag = jnp.exp(lam_re * dt * n)
    return mag * jnp.cos(lam_im * dt * n), mag * jnp.sin(lam_im * dt * n)


def s5_prompt_params(lam_re, lam_im, log_dt, b_re, b_im, c_re, c_im, d_skip, nchb):
    g, p, q = S5_GROUPS, S5_STATE, S5_CHUNK
    dt, lbr, lbi, bbr, bbi = s5_discretize(lam_re, lam_im, log_dt, b_re, b_im)
    pr, pi = _lam_power(lam_re, lam_im, dt, jnp.arange(q + 1))
    lb_r, lb_i = _cmul(pr[:q, :, :, None], pi[:q, :, :, None], bbr[None], bbi[None])
    hp = lax.Precision.HIGHEST
    kern = (jnp.einsum('gkp,tgpc->tgkc', c_re, lb_r, precision=hp)
            - jnp.einsum('gkp,tgpc->tgkc', c_im, lb_i, precision=hp))
    s_i = jnp.arange(q)[:, None]
    t_i = jnp.arange(q)[None, :]
    tau = jnp.clip(t_i - s_i, 0, q - 1)
    m = jnp.where((t_i >= s_i)[:, :, None, None, None], kern[tau], 0.0)
    m = m.transpose(2, 0, 4, 1, 3).reshape(g, q * S5_GROUP, q * S5_GROUP)
    er, ei = _cmul(pr[q - 1 - jnp.arange(q)][:, :, :, None], pi[q - 1 - jnp.arange(q)][:, :, :, None],
                   bbr[None], bbi[None])
    w_re = er.transpose(1, 0, 3, 2).reshape(g, q * S5_GROUP, p)
    w_im = ei.transpose(1, 0, 3, 2).reshape(g, q * S5_GROUP, p)
    vr, vi = _cmul(c_re[None], c_im[None], pr[1:, :, None, :], pi[1:, :, None, :])
    v_re = vr.transpose(1, 3, 0, 2).reshape(g, p, q * S5_GROUP)
    v_im = (-vi).transpose(1, 3, 0, 2).reshape(g, p, q * S5_GROUP)

    def pair_cols(a):
        z = jnp.zeros_like(a)
        even = (jnp.arange(g) % 2 == 0)[:, None, None]
        return jnp.concatenate([jnp.where(even, a, z), jnp.where(even, z, a)], axis=-1)

    def pair_rows(a):
        z = jnp.zeros_like(a)
        even = (jnp.arange(g) % 2 == 0)[:, None, None]
        return jnp.concatenate([jnp.where(even, a, z), jnp.where(even, z, a)], axis=1)

    nlog = max(int(math.log2(nchb)), 1)
    sr, si = _lam_power(lam_re, lam_im, dt, q * (2.0 ** jnp.arange(8)))
    lam_sc_re = sr.reshape(8, g // 2, 2 * p).transpose(1, 0, 2)
    lam_sc_im = si.reshape(8, g // 2, 2 * p).transpose(1, 0, 2)
    d_vec = jnp.tile(d_skip.reshape(g, 1, S5_GROUP), (1, q, 1)).reshape(g, 1, q * S5_GROUP)
    del nlog
    return (m, pair_cols(w_re), pair_cols(w_im), pair_rows(v_re), pair_rows(v_im), lam_sc_re, lam_sc_im, d_vec)


def _s5_prompt_kernel(u_ref, m_ref, wre_ref, wim_ref, vre_ref, vim_ref, lre_ref, lim_ref, d_ref,
                      h_ref, fre_ref, fim_ref, *, nchb, nb):
    rows = nb * nchb
    u0, u1 = u_ref[0], u_ref[1]
    e_re = _dot3(u0, wre_ref[0]) + _dot3(u1, wre_ref[1])
    e_im = _dot3(u0, wim_ref[0]) + _dot3(u1, wim_ref[1])
    pos = jnp.bitwise_and(lax.broadcasted_iota(jnp.int32, (rows, 2 * S5_STATE), 0), nchb - 1)
    d, k = 1, 0
    while d < nchb:
        lr, li = lre_ref[0, k:k + 1, :], lim_ref[0, k:k + 1, :]
        sr, si = pltpu.roll(e_re, d, 0), pltpu.roll(e_im, d, 0)
        keep = pos >= d
        e_re, e_im = (e_re + jnp.where(keep, lr * sr - li * si, 0.0),
                      e_im + jnp.where(keep, lr * si + li * sr, 0.0))
        d, k = d * 2, k + 1
    for b in range(nb):
        r = (b + 1) * nchb - 1
        fre_ref[0, b:b + 1, :] = e_re[r:r + 1, :]
        fim_ref[0, b:b + 1, :] = e_im[r:r + 1, :]
    first = pos == 0
    xc_re = jnp.where(first, 0.0, pltpu.roll(e_re, 1, 0))
    xc_im = jnp.where(first, 0.0, pltpu.roll(e_im, 1, 0))
    for gi, u in enumerate((u0, u1)):
        y = (_dot3(u, m_ref[gi]) + _dot3(xc_re, vre_ref[gi]) + _dot3(xc_im, vim_ref[gi])) + u * d_ref[gi]
        h_ref[gi] = _gelu_tanh(y)


def s5_prompt(u_t, params, batch, nchb):
    m, w_re, w_im, v_re, v_im, l_re, l_im, d_vec = params
    g, rows, qc = u_t.shape
    p2 = 2 * S5_STATE
    gp = lambda *tail: pl.BlockSpec((2,) + tail, lambda i: (i,) + (0,) * len(tail))
    one = lambda *tail: pl.BlockSpec((1,) + tail, lambda i: (i,) + (0,) * len(tail))
    return pl.pallas_call(
        functools.partial(_s5_prompt_kernel, nchb=nchb, nb=batch),
        grid=(g // 2,),
        in_specs=[gp(rows, qc), gp(qc, qc), gp(qc, p2), gp(qc, p2), gp(p2, qc), gp(p2, qc),
                  one(8, p2), one(8, p2), gp(1, qc)],
        out_specs=[gp(rows, qc), one(8, p2), one(8, p2)],
        out_shape=[jax.ShapeDtypeStruct((g, rows, qc), F32), jax.ShapeDtypeStruct((g // 2, 8, p2), F32),
                   jax.ShapeDtypeStruct((g // 2, 8, p2), F32)],
        compiler_params=_cparams(("parallel",)),
        name="s5_prompt",
    )(u_t, m, w_re, w_im, v_re, v_im, l_re, l_im, d_vec)


def s5_step_params(lam_re, lam_im, log_dt, b_re, b_im, c_re, c_im, d_skip):
    g, p = S5_GROUPS, S5_STATE
    _, lbr, lbi, bbr, bbi = s5_discretize(lam_re, lam_im, log_dt, b_re, b_im)
    eye8 = jnp.eye(8, dtype=F32)

    def bd_in(bb):
        x = bb.reshape(8, 8, p, S5_GROUP)
        return jnp.einsum('jgpc,gh->jgchp', x, eye8).reshape(8, 8 * S5_GROUP, 8 * p)

    def bd_out(cc):
        x = cc.reshape(8, 8, S5_GROUP, p)
        return jnp.einsum('jgkp,gh->jgphk', x, eye8).reshape(8, 8 * p, 8 * S5_GROUP)

    return (bd_in(bbr), bd_in(bbi), bd_out(c_re), bd_out(-c_im), lbr.reshape(1, g * p), lbi.reshape(1, g * p),
            d_skip.reshape(1, g * S5_GROUP))


def _s5_step_kernel(u_ref, xr_ref, xi_ref, bre_ref, bim_ref, cre_ref, cim_ref, lr_ref, li_ref, d_ref,
                    h_ref, nr_ref, ni_ref):
    u = u_ref[...]
    lr, li = lr_ref[...], li_ref[...]
    xr, xi = xr_ref[...], xi_ref[...]
    nr = lr * xr - li * xi + _dot3(u, bre_ref[0])
    ni = lr * xi + li * xr + _dot3(u, bim_ref[0])
    nr_ref[...] = nr
    ni_ref[...] = ni
    y = _dot3(nr, cre_ref[0]) + _dot3(ni, cim_ref[0]) + u * d_ref[...]
    h_ref[...] = _gelu_tanh(y)


def s5_step(p_s5, row0, x_re, x_im, params):
    b_re, b_im, c_re, c_im, l_re, l_im, d_vec = params
    nb, gp = x_re.shape
    lt = 8 * S5_GROUP
    st = 8 * S5_STATE
    r0 = row0 // nb
    u_spec = pl.BlockSpec((nb, lt), lambda j: (r0, j))
    x_spec = pl.BlockSpec((nb, st), lambda j: (0, j))
    return pl.pallas_call(
        _s5_step_kernel,
        grid=(gp // st,),
        in_specs=[u_spec, x_spec, x_spec,
                  pl.BlockSpec((1, lt, st), lambda j: (j, 0, 0)), pl.BlockSpec((1, lt, st), lambda j: (j, 0, 0)),
                  pl.BlockSpec((1, st, lt), lambda j: (j, 0, 0)), pl.BlockSpec((1, st, lt), lambda j: (j, 0, 0)),
                  pl.BlockSpec((1, st), lambda j: (0, j)), pl.BlockSpec((1, st), lambda j: (0, j)),
                  pl.BlockSpec((1, lt), lambda j: (0, j))],
        out_specs=[pl.BlockSpec((nb, lt), lambda j: (0, j)), x_spec, x_spec],
        out_shape=[jax.ShapeDtypeStruct((nb, BRANCH_WIDTH), F32), jax.ShapeDtypeStruct((nb, gp), F32),
                   jax.ShapeDtypeStruct((nb, gp), F32)],
        compiler_params=_cparams(("parallel",)),
        name="s5_step",
    )(p_s5, x_re, x_im, b_re, b_im, c_re, c_im, l_re, l_im, d_vec)


def _l2norm_rows(x):
    return x * lax.rsqrt(jnp.sum(x * x, axis=-1, keepdims=True) + NORM_EPS)


def _gdn_prompt_kernel(qkv_ref, z_ref, ab_ref, cw_ref, alog_ref, dtb_ref, nw_ref, o_ref, s_out_ref,
                       xbuf_ref, s_ref, *, nc, chunk):
    c = pl.program_id(1)
    bw = BRANCH_WIDTH
    halo = 8

    @pl.when(c == 0)
    def _():
        xbuf_ref[0:halo, :] = jnp.zeros((halo, 3 * bw), F32)
        s_ref[...] = jnp.zeros_like(s_ref)

    x = qkv_ref[...]
    xbuf_ref[halo:halo + chunk, :] = x
    conv = x * cw_ref[G_CONV - 1:G_CONV, :]
    for i in range(1, G_CONV):
        conv = conv + xbuf_ref[halo - i:halo - i + chunk, :] * cw_ref[G_CONV - 1 - i:G_CONV - i, :]
    xbuf_ref[0:halo, :] = x[chunk - halo:chunk, :]
    qkv = _silu(conv)

    ab = ab_ref[...]
    g_all = -jnp.exp(alog_ref[...]) * _softplus(ab + dtb_ref[...])
    beta_all = _sigmoid(ab)
    ri = lax.broadcasted_iota(jnp.int32, (chunk, chunk), 0)
    ci = lax.broadcasted_iota(jnp.int32, (chunk, chunk), 1)
    causal = ri >= ci
    strict = ri > ci
    ltri = jnp.where(causal, 1.0, 0.0).astype(BF16)
    eye_c = jnp.where(ri == ci, 1.0, 0.0).astype(F32)
    gcum = _dot_sel(ltri, g_all)
    gcum_t = _dot_sel(_eye(128), gcum, _NT)
    z = z_ref[...]
    nw = nw_ref[...]

    for h in range(G_HEADS):
        q = _l2norm_rows(qkv[:, h * G_DK:(h + 1) * G_DK]) * (G_DK ** -0.5)
        k = _l2norm_rows(qkv[:, bw + h * G_DK:bw + (h + 1) * G_DK])
        v = qkv[:, 2 * bw + h * G_DK:2 * bw + (h + 1) * G_DK]
        gc = gcum[:, h:h + 1]
        gr = gcum_t[h:h + 1, :]
        beta = beta_all[:, G_HEADS + h:G_HEADS + h + 1]
        decay = jnp.where(causal, jnp.exp(jnp.where(causal, gc - gr, 0.0)), 0.0)
        qk_kk = _dot3(jnp.concatenate([k, q], axis=0), k, _NT)
        kk, qk = qk_kk[:chunk], qk_kk[chunk:]
        n = jnp.where(strict, -(beta * kk * decay), 0.0)
        tinv = eye_c + n
        span = 2
        while span < chunk:
            n = _dot3(n, n)
            tinv = tinv + _dot3(tinv, n)
            span *= 2
        eg = jnp.exp(gc)
        rhs = jnp.concatenate([beta * v, (beta * eg) * k], axis=-1)
        sol = _dot3(tinv, rhs)
        s = s_ref[h]
        ks_qs = _dot3(jnp.concatenate([sol[:, G_DK:], q * eg], axis=0), s)
        w = sol[:, :G_DK] - ks_qs[:chunk]
        o = ks_qs[chunk:] + _dot3(qk * decay, w)
        g_last = gc[chunk - 1:chunk, :]
        s_new = jnp.exp(g_last) * s + _dot3(k * jnp.exp(g_last - gc), w, _TN)
        s_ref[h] = s_new
        zh = z[:, h * G_DK:(h + 1) * G_DK]
        o_ref[:, h * G_DK:(h + 1) * G_DK] = (_rms_rows(o) * nw * _silu(zh)).astype(o_ref.dtype)

        @pl.when(c == nc - 1)
        def _():
            s_out_ref[0, h] = s_new


def gdn_prompt(p_gdn, conv_w, a_log, dt_bias, norm_w, batch, seq, chunk=G_CHUNK):
    nc = seq // chunk
    bw = BRANCH_WIDTH
    pad = lambda x: jnp.zeros((1, 128), F32).at[0, :G_HEADS].set(x)
    full = lambda shape: pl.BlockSpec(shape, lambda b, c: (0,) * len(shape))
    return pl.pallas_call(
        functools.partial(_gdn_prompt_kernel, nc=nc, chunk=chunk),
        grid=(batch, nc),
        in_specs=[pl.BlockSpec((chunk, 3 * bw), lambda b, c: (b * nc + c, 0)),
                  pl.BlockSpec((chunk, bw), lambda b, c: (b * nc + c, 3)),
                  pl.BlockSpec((chunk, 128), lambda b, c: (b * nc + c, 4 * bw // 128)),
                  full((G_CONV, 3 * bw)), full((1, 128)), full((1, 128)), full((1, G_DK))],
        out_specs=[pl.BlockSpec((chunk, bw), lambda b, c: (b * nc + c, 0)),
                   pl.BlockSpec((1, G_HEADS, G_DK, G_DK), lambda b, c: (b, 0, 0, 0))],
        out_shape=[jax.ShapeDtypeStruct((batch * seq, bw), BF16),
                   jax.ShapeDtypeStruct((batch, G_HEADS, G_DK, G_DK), F32)],
        scratch_shapes=[pltpu.VMEM((8 + chunk, 3 * bw), F32), pltpu.VMEM((G_HEADS, G_DK, G_DK), F32)],
        compiler_params=_cparams(("parallel", "arbitrary")),
        name="gdn_prompt",
    )(p_gdn, p_gdn, p_gdn, conv_w, pad(a_log), pad(dt_bias), norm_w.reshape(1, G_DK))


def _gdn_step_kernel(qkv_ref, z_ref, ab_ref, cbuf_ref, cw_ref, alog_ref, dtb_ref, nw_ref, s_ref,
                     o_ref, cbuf_out_ref, s_out_ref, *, bb):
    bw = BRANCH_WIDTH
    x = qkv_ref[...]
    conv = x * cw_ref[G_CONV - 1:G_CONV, :]
    for i in range(G_CONV - 1):
        conv = conv + cbuf_ref[i] * cw_ref[i:i + 1, :]
    for i in range(G_CONV - 2):
        cbuf_out_ref[i] = cbuf_ref[i + 1]
    cbuf_out_ref[G_CONV - 2] = x
    qkv = _silu(conv)
    ab = ab_ref[...]
    g_all = -jnp.exp(alog_ref[...]) * _softplus(ab + dtb_ref[...])
    beta_all = _sigmoid(ab)
    eye = _eye(G_DK)
    z = z_ref[...]
    nw = nw_ref[...]
    for h in range(G_HEADS):
        q = _l2norm_rows(qkv[:, h * G_DK:(h + 1) * G_DK]) * (G_DK ** -0.5)
        k = _l2norm_rows(qkv[:, bw + h * G_DK:bw + (h + 1) * G_DK])
        v = qkv[:, 2 * bw + h * G_DK:2 * bw + (h + 1) * G_DK]
        eg = jnp.exp(g_all[:, h:h + 1])
        beta = beta_all[:, G_HEADS + h:G_HEADS + h + 1]
        q_cols = _dot_sel(eye, q, _NT)
        k_cols = _dot_sel(eye, k, _NT)
        rows = []
        for i in range(bb):
            s = s_ref[i, h]
            kc = k_cols[:, i:i + 1]
            ks = jnp.sum(kc * s, axis=0, keepdims=True)
            w = beta[i:i + 1, :] * (v[i:i + 1, :] - eg[i:i + 1, :] * ks)
            s_new = eg[i:i + 1, :] * s + kc * w
            s_out_ref[i, h] = s_new
            rows.append(jnp.sum(q_cols[:, i:i + 1] * s_new, axis=0, keepdims=True))
        o = jnp.concatenate(rows, axis=0)
        zh = z[:, h * G_DK:(h + 1) * G_DK]
        o_ref[:, h * G_DK:(h + 1) * G_DK] = (_rms_rows(o) * nw * _silu(zh)).astype(o_ref.dtype)


def gdn_step(p_gdn, row0, conv_buf_t, state, conv_w, a_log, dt_bias, norm_w, bb=8):
    nb = state.shape[0]
    bw = BRANCH_WIDTH
    r0 = row0 // bb
    pad = lambda x: jnp.zeros((1, 128), F32).at[0, :G_HEADS].set(x)
    full = lambda shape: pl.BlockSpec(shape, lambda j: (0,) * len(shape))
    cb = pl.BlockSpec((G_CONV - 1, bb, 3 * bw), lambda j: (0, j, 0))
    st = pl.BlockSpec((bb, G_HEADS, G_DK, G_DK), lambda j: (j, 0, 0, 0))
    return pl.pallas_call(
        functools.partial(_gdn_step_kernel, bb=bb),
        grid=(nb // bb,),
        in_specs=[pl.BlockSpec((bb, 3 * bw), lambda j: (r0 + j, 0)),
                  pl.BlockSpec((bb, bw), lambda j: (r0 + j, 3)),
                  pl.BlockSpec((bb, 128), lambda j: (r0 + j, 4 * bw // 128)),
                  cb, full((G_CONV, 3 * bw)), full((1, 128)), full((1, 128)), full((1, G_DK)), st],
        out_specs=[pl.BlockSpec((bb, bw), lambda j: (j, 0)), cb, st],
        out_shape=[jax.ShapeDtypeStruct((nb, bw), BF16), jax.ShapeDtypeStruct(conv_buf_t.shape, F32),
                   jax.ShapeDtypeStruct(state.shape, F32)],
        compiler_params=_cparams(("parallel",)),
        name="gdn_step",
    )(p_gdn, p_gdn, p_gdn, conv_buf_t, conv_w, pad(a_log), pad(dt_bias), norm_w.reshape(1, G_DK), state)


def _rope_tables(pos):
    half = R_DK // 2
    inv = ROPE_BASE ** (-jnp.arange(half, dtype=F32) / half)
    ang = pos.astype(F32)[:, None] * inv[None, :]
    return jnp.cos(ang), jnp.sin(ang)


def kernel(x_prompt, x_sample, state_ret, state_s5_re, state_s5_im, state_gdn, state_gdn_conv, norm_mix_w, w_in, s5_lambda_re, s5_lambda_im, s5_log_dt, s5_b_re, s5_b_im, s5_c_re, s5_c_im, s5_d, s5_w_glu, s5_b_glu, gdn_conv_w, gdn_a_log, gdn_dt_bias, gdn_norm_w, w_branch, w_out, norm_ffn_w, w_gate_ffn, w_up_ffn, w_down_ffn, norm_final_w):
    bp, lp, d = x_prompt.shape
    nb = x_sample.shape[0]
    depth = w_in.shape[0]
    tp = bp * lp
    bw = BRANCH_WIDTH
    x = jnp.concatenate([x_prompt.reshape(tp, d), x_sample.reshape(nb, d)], axis=0)

    cos_p, sin_p = _rope_tables(jnp.arange(lp, dtype=jnp.int32))
    cos_s, sin_s = _rope_tables(PAST_LEN + jnp.arange(1, dtype=jnp.int32))
    log_gamma = jnp.log1p(-jnp.exp2(-5.0 - jnp.arange(R_HEADS, dtype=F32)))
    nchb = lp // S5_CHUNK

    new_p = [[], [], [], [], []]
    new_s = [[], [], [], [], []]
    xn = rmsnorm_rows(x, norm_mix_w[0], BF16)
    y_final = None
    for l in range(depth):
        wl = w_in[l]
        o_su, o_gqkv, o_ga, o_gz, o_gates = 4 * bw, 5 * bw, 8 * bw, 8 * bw + 2 * G_HEADS, 9 * bw + 2 * G_HEADS
        w_ret = wl[:, :o_su].astype(BF16)
        w_s5 = wl[:, o_su:o_gqkv].astype(BF16)
        w_gdn = jnp.concatenate([wl[:, o_gqkv:o_ga], wl[:, o_gz:o_gates], wl[:, o_ga:o_gz],
                                 jnp.zeros((d, 128 - 2 * G_HEADS), F32)], axis=1).astype(BF16)
        w_gate = wl[:, o_gates:].astype(BF16)

        p_ret = matmul(xn, w_ret, tn=1024, name="proj_ret")
        p_s5 = matmul(xn, w_s5, tn=1024, name="proj_s5")
        p_gdn = matmul(xn, w_gdn, tn=1408, name="proj_gdn")

        r_p, ret_p = retention_prompt(p_ret, cos_p, sin_p, log_gamma, bp, lp)
        r_s, ret_s = retention_step(p_ret, tp, cos_s, sin_s, log_gamma, state_ret[l])

        s5_raw = (s5_lambda_re[l], s5_lambda_im[l], s5_log_dt[l], s5_b_re[l], s5_b_im[l], s5_c_re[l], s5_c_im[l],
                  s5_d[l])
        u_t = p_s5[:tp].reshape(tp // S5_CHUNK, S5_CHUNK, S5_GROUPS, S5_GROUP).transpose(2, 0, 1, 3)
        u_t = u_t.reshape(S5_GROUPS, tp // S5_CHUNK, S5_CHUNK * S5_GROUP)
        h_t, f_re, f_im = s5_prompt(u_t, s5_prompt_params(*s5_raw, nchb), bp, nchb)
        h_p = h_t.reshape(S5_GROUPS, tp // S5_CHUNK, S5_CHUNK, S5_GROUP).transpose(1, 2, 0, 3).reshape(tp, bw)
        fin = lambda f: f[:, :bp, :].reshape(S5_GROUPS // 2, bp, 2, S5_STATE).transpose(1, 0, 2, 3).reshape(
            bp, S5_GROUPS, S5_STATE)
        h_s, s5re_s, s5im_s = s5_step(p_s5, tp, state_s5_re[l].reshape(nb, -1), state_s5_im[l].reshape(nb, -1),
                                      s5_step_params(*s5_raw))
        s_all = s5_glu(jnp.concatenate([h_p, h_s], axis=0), s5_w_glu[l].astype(BF16), s5_b_glu[l])

        g_p, gdn_p = gdn_prompt(p_gdn, gdn_conv_w[l], gdn_a_log[l], gdn_dt_bias[l], gdn_norm_w[l], bp, lp)
        g_s, conv_s_t, gdn_s = gdn_step(p_gdn, tp, state_gdn_conv[l].transpose(1, 0, 2), state_gdn[l],
                                        gdn_conv_w[l], gdn_a_log[l], gdn_dt_bias[l], gdn_norm_w[l])
        conv_p = p_gdn[:tp, :3 * bw].reshape(bp, lp, 3 * bw)[:, lp - (G_CONV - 1):, :]

        merged = merge_branches(xn, jnp.concatenate([r_p, r_s], axis=0), s_all,
                                jnp.concatenate([g_p, g_s], axis=0), w_gate, w_branch[l].astype(BF16))
        h, hn = out_proj(merged, w_out[l].astype(BF16), x, norm_ffn_w[l])
        last = l == depth - 1
        next_w = norm_final_w if last else norm_mix_w[l + 1]
        x, xn = ffn(hn, h, w_gate_ffn[l].astype(BF16), w_up_ffn[l].astype(BF16), w_down_ffn[l].astype(BF16),
                    next_w, F32 if last else BF16)
        if last:
            y_final = xn

        for lst, val in zip(new_p, (ret_p, fin(f_re), fin(f_im), gdn_p, conv_p)):
            lst.append(val)
        for lst, val in zip(new_s, (ret_s, s5re_s.reshape(nb, S5_GROUPS, S5_STATE),
                                    s5im_s.reshape(nb, S5_GROUPS, S5_STATE), gdn_s, conv_s_t.transpose(1, 0, 2))):
            lst.append(val)

    y_prompt = y_final[:tp].reshape(bp, lp, d)
    y_sample = y_final[tp:].reshape(nb, 1, d)
    outs_p = [jnp.stack(a, 0) for a in new_p]
    outs_s = [jnp.stack(a, 0) for a in new_s]
    return (y_prompt, y_sample, *outs_p, *outs_s)
```

```python
import functools
import math

import jax
import jax.numpy as jnp
from jax import lax
from jax.experimental import pallas as pl
from jax.experimental.pallas import tpu as pltpu

F32 = jnp.float32
BF16 = jnp.bfloat16

D_MODEL = 2048
BRANCH_WIDTH = 1024
N_BRANCH = 3
R_HEADS = 4
R_DK = 256
R_CHUNK = 128
ROPE_BASE = 10000.0
S5_GROUP = 16
S5_GROUPS = 64
S5_STATE = 64
S5_CHUNK = 16
G_HEADS = 8
G_DK = 128
G_CONV = 4
G_CHUNK = 64
NORM_EPS = 1e-6
PAST_LEN = 16384

VMEM_LIMIT_BYTES = 56 * 1024 * 1024
TM = 640


def _cparams(sem):
    return pltpu.CompilerParams(dimension_semantics=sem, vmem_limit_bytes=VMEM_LIMIT_BYTES)


_NN = (((1,), (0,)), ((), ()))
_NT = (((1,), (1,)), ((), ()))
_TN = (((0,), (0,)), ((), ()))


def _dg(a, b, dims):
    return lax.dot_general(a, b, dims, preferred_element_type=F32)


def _split2(a):
    hi = a.astype(BF16)
    lo = (a - hi.astype(F32)).astype(BF16)
    return hi, lo


def _dot3(a, b, dims=_NN):
    ah, al = _split2(a)
    bh, bl = _split2(b)
    return _dg(ah, bh, dims) + (_dg(ah, bl, dims) + _dg(al, bh, dims))


def _dot_sel(sel_bf16, x, dims=_NN):
    h1 = x.astype(BF16)
    r1 = x - h1.astype(F32)
    h2 = r1.astype(BF16)
    h3 = (r1 - h2.astype(F32)).astype(BF16)
    return _dg(sel_bf16, h1, dims) + (_dg(sel_bf16, h2, dims) + _dg(sel_bf16, h3, dims))


def _sel_dot(x, sel_bf16, dims=_NN):
    h1 = x.astype(BF16)
    r1 = x - h1.astype(F32)
    h2 = r1.astype(BF16)
    h3 = (r1 - h2.astype(F32)).astype(BF16)
    return _dg(h1, sel_bf16, dims) + (_dg(h2, sel_bf16, dims) + _dg(h3, sel_bf16, dims))


def _eye(n, dtype=BF16):
    r = lax.broadcasted_iota(jnp.int32, (n, n), 0)
    c = lax.broadcasted_iota(jnp.int32, (n, n), 1)
    return jnp.where(r == c, 1.0, 0.0).astype(dtype)


def _sigmoid(x):
    return 1.0 / (1.0 + jnp.exp(-x))


def _silu(x):
    return x * _sigmoid(x)


def _gelu_tanh(x):
    return 0.5 * x * (1.0 + jnp.tanh(math.sqrt(2.0 / math.pi) * (x + 0.044715 * (x * x * x))))


def _softplus(x):
    return jnp.maximum(x, 0.0) + jnp.log1p(jnp.exp(-jnp.abs(x)))


def _rms_rows(x, eps=NORM_EPS):
    return x * lax.rsqrt(jnp.mean(x * x, axis=-1, keepdims=True) + eps)


def _norm_kernel(x_ref, w_ref, o_ref):
    o_ref[...] = (_rms_rows(x_ref[...]) * w_ref[...]).astype(o_ref.dtype)


def rmsnorm_rows(x, w, out_dtype, tm=TM):
    t, d = x.shape
    return pl.pallas_call(
        _norm_kernel,
        grid=(t // tm,),
        in_specs=[pl.BlockSpec((tm, d), lambda i: (i, 0)), pl.BlockSpec((1, d), lambda i: (0, 0))],
        out_specs=pl.BlockSpec((tm, d), lambda i: (i, 0)),
        out_shape=jax.ShapeDtypeStruct((t, d), out_dtype),
        compiler_params=_cparams(("parallel",)),
        name="rmsnorm",
    )(x, w.reshape(1, d))


def _mm_kernel(a_ref, w_ref, o_ref):
    o_ref[...] = jnp.dot(a_ref[...], w_ref[...], preferred_element_type=F32)


def matmul(a, w, tn, tm=TM, name="proj"):
    t, k = a.shape
    n = w.shape[1]
    return pl.pallas_call(
        _mm_kernel,
        grid=(t // tm, n // tn),
        in_specs=[pl.BlockSpec((tm, k), lambda i, j: (i, 0)), pl.BlockSpec((k, tn), lambda i, j: (0, j))],
        out_specs=pl.BlockSpec((tm, tn), lambda i, j: (i, j)),
        out_shape=jax.ShapeDtypeStruct((t, n), F32),
        compiler_params=_cparams(("parallel", "arbitrary")),
        name=name,
    )(a, w)


def _merge_kernel(xn_ref, br_r, br_s, br_g, wg0, wg1, wg2, wb_ref, o_ref):
    xn = xn_ref[...]
    acc = None
    for i, (br, wg) in enumerate(((br_r, wg0), (br_s, wg1), (br_g, wg2))):
        gate = _sigmoid(jnp.dot(xn, wg[...], preferred_element_type=F32))
        up = jnp.dot(br[...], wb_ref[i], preferred_element_type=F32)
        acc = gate * up if acc is None else acc + gate * up
    o_ref[...] = acc.astype(o_ref.dtype)


def merge_branches(xn, br_r, br_s, br_g, w_gate, w_branch, tn=512, tm=TM):
    t, d = xn.shape
    bw = br_r.shape[1]
    nj = d // tn
    a_spec = pl.BlockSpec((tm, bw), lambda i, j: (i, 0))
    wg_specs = [pl.BlockSpec((d, tn), functools.partial(lambda i, j, n: (0, n * nj + j), n=n))
                for n in range(N_BRANCH)]
    return pl.pallas_call(
        _merge_kernel,
        grid=(t // tm, nj),
        in_specs=[pl.BlockSpec((tm, d), lambda i, j: (i, 0)), a_spec, a_spec, a_spec, *wg_specs,
                  pl.BlockSpec((N_BRANCH, bw, tn), lambda i, j: (0, 0, j))],
        out_specs=pl.BlockSpec((tm, tn), lambda i, j: (i, j)),
        out_shape=jax.ShapeDtypeStruct((t, d), BF16),
        compiler_params=_cparams(("parallel", "arbitrary")),
        name="merge",
    )(xn, br_r, br_s, br_g, w_gate, w_gate, w_gate, w_branch)


def _outproj_kernel(m_ref, w_ref, x_ref, nw_ref, h_ref, hn_ref):
    h = x_ref[...] + jnp.dot(m_ref[...], w_ref[...], preferred_element_type=F32)
    h_ref[...] = h
    hn_ref[...] = (_rms_rows(h) * nw_ref[...]).astype(hn_ref.dtype)


def out_proj(merged, w_out, x, norm_w, tm=TM):
    t, d = x.shape
    row = pl.BlockSpec((tm, d), lambda i: (i, 0))
    return pl.pallas_call(
        _outproj_kernel,
        grid=(t // tm,),
        in_specs=[row, pl.BlockSpec((d, d), lambda i: (0, 0)), row, pl.BlockSpec((1, d), lambda i: (0, 0))],
        out_specs=[row, row],
        out_shape=[jax.ShapeDtypeStruct((t, d), F32), jax.ShapeDtypeStruct((t, d), BF16)],
        compiler_params=_cparams(("parallel",)),
        name="outproj",
    )(merged, w_out, x, norm_w.reshape(1, d))


def _ffn_kernel(hn_ref, h_ref, wg_ref, wu_ref, wd_ref, nw_ref, y_ref, xn_ref, *, nf):
    f = pl.program_id(1)

    @pl.when(f == 0)
    def _():
        y_ref[...] = h_ref[...]

    hn = hn_ref[...]
    g = jnp.dot(hn, wg_ref[...], preferred_element_type=F32)
    u = jnp.dot(hn, wu_ref[...], preferred_element_type=F32)
    act = (_silu(g) * u).astype(BF16)
    y_ref[...] += jnp.dot(act, wd_ref[...], preferred_element_type=F32)

    @pl.when(f == nf - 1)
    def _():
        xn_ref[...] = (_rms_rows(y_ref[...]) * nw_ref[...]).astype(xn_ref.dtype)


def ffn(hn, h, w_gate, w_up, w_down, next_norm_w, xn_dtype, tf=512, tm=TM):
    t, d = h.shape
    dff = w_gate.shape[1]
    nf = dff // tf
    row = pl.BlockSpec((tm, d), lambda i, f: (i, 0))
    return pl.pallas_call(
        functools.partial(_ffn_kernel, nf=nf),
        grid=(t // tm, nf),
        in_specs=[row, row, pl.BlockSpec((d, tf), lambda i, f: (0, f)), pl.BlockSpec((d, tf), lambda i, f: (0, f)),
                  pl.BlockSpec((tf, d), lambda i, f: (f, 0)), pl.BlockSpec((1, d), lambda i, f: (0, 0))],
        out_specs=[row, row],
        out_shape=[jax.ShapeDtypeStruct((t, d), F32), jax.ShapeDtypeStruct((t, d), xn_dtype)],
        compiler_params=_cparams(("parallel", "arbitrary")),
        name="ffn",
    )(hn, h, w_gate, w_up, w_down, next_norm_w.reshape(1, d))


def _glu_kernel(h_ref, w_ref, b_ref, o_ref):
    h = h_ref[...]
    z = jnp.dot(h.astype(BF16), w_ref[...], preferred_element_type=F32) + b_ref[...]
    o_ref[...] = (h * _sigmoid(z)).astype(o_ref.dtype)


def s5_glu(h, w_glu, b_glu, tm=TM):
    t, d = h.shape
    row = pl.BlockSpec((tm, d), lambda i: (i, 0))
    return pl.pallas_call(
        _glu_kernel,
        grid=(t // tm,),
        in_specs=[row, pl.BlockSpec((d, d), lambda i: (0, 0)), pl.BlockSpec((1, d), lambda i: (0, 0))],
        out_specs=row,
        out_shape=jax.ShapeDtypeStruct((t, d), BF16),
        compiler_params=_cparams(("parallel",)),
        name="s5_glu",
    )(h, w_glu, b_glu.reshape(1, d))


def _rotary(x, cos, sin):
    half = x.shape[-1] // 2
    x1, x2 = x[:, :half], x[:, half:]
    return jnp.concatenate([x1 * cos - x2 * sin, x1 * sin + x2 * cos], axis=-1)


def _ret_prompt_kernel(lg_ref, q_ref, k_ref, v_ref, g_ref, cos_ref, sin_ref, o_ref, s_out_ref, s_ref, *, nc, chunk):
    c = pl.program_id(1)

    @pl.when(c == 0)
    def _():
        s_ref[...] = jnp.zeros_like(s_ref)

    cos, sin = cos_ref[...], sin_ref[...]
    ii = lax.broadcasted_iota(jnp.int32, (chunk, chunk), 0)
    jj = lax.broadcasted_iota(jnp.int32, (chunk, chunk), 1)
    diff = (ii - jj).astype(F32)
    idx = lax.broadcasted_iota(jnp.int32, (chunk, 1), 0).astype(F32)
    heads = range(R_HEADS)
    hs = lambda h: slice(h * R_DK, (h + 1) * R_DK)
    lg = [lg_ref[h] for h in heads]
    q_l = [_rotary(q_ref[:, hs(h)], cos, sin) for h in heads]
    k_l = [_rotary(k_ref[:, hs(h)], cos, sin) * (R_DK ** -0.5) for h in heads]
    v_l = [v_ref[:, hs(h)].astype(BF16) for h in heads]
    s_l = [s_ref[h] for h in heads]
    scores = [_dg(q_l[h].astype(BF16), k_l[h].astype(BF16), _NT)
              * jnp.where(diff >= 0, jnp.exp(lg[h] * jnp.maximum(diff, 0.0)), 0.0) for h in heads]
    cross = [_dg((q_l[h] * jnp.exp(lg[h] * (idx + 1.0))).astype(BF16), s_l[h].astype(BF16), _NN) for h in heads]
    for h in heads:
        o = _dg(scores[h].astype(BF16), v_l[h], _NN) + cross[h]
        o_ref[:, hs(h)] = (_silu(g_ref[:, hs(h)]) * _rms_rows(o)).astype(o_ref.dtype)
    for h in heads:
        k_dec = (k_l[h] * jnp.exp(lg[h] * (chunk - 1.0 - idx))).astype(BF16)
        s_ref[h] = jnp.exp(lg[h] * jnp.full((1, 1), float(chunk), F32)) * s_l[h] + _dg(k_dec, v_l[h], _TN)

    @pl.when(c == nc - 1)
    def _():
        s_out_ref[0] = s_ref[...]


def retention_prompt(p_ret, cos, sin, log_gamma, batch, seq, chunk=R_CHUNK):
    nc = seq // chunk
    bw = BRANCH_WIDTH
    rows = lambda off: pl.BlockSpec((chunk, bw), functools.partial(lambda b, c, off: (b * nc + c, off), off=off))
    tab = pl.BlockSpec((chunk, R_DK // 2), lambda b, c: (c, 0))
    return pl.pallas_call(
        functools.partial(_ret_prompt_kernel, nc=nc, chunk=chunk),
        grid=(batch, nc),
        in_specs=[pl.BlockSpec(memory_space=pltpu.SMEM), rows(0), rows(1), rows(2), rows(3), tab, tab],
        out_specs=[pl.BlockSpec((chunk, bw), lambda b, c: (b * nc + c, 0)),
                   pl.BlockSpec((1, R_HEADS, R_DK, R_DK), lambda b, c: (b, 0, 0, 0))],
        out_shape=[jax.ShapeDtypeStruct((p_ret.shape[0], bw), BF16),
                   jax.ShapeDtypeStruct((batch, R_HEADS, R_DK, R_DK), F32)],
        scratch_shapes=[pltpu.VMEM((R_HEADS, R_DK, R_DK), F32)],
        compiler_params=_cparams(("parallel", "arbitrary")),
        name="ret_prompt",
    )(log_gamma, p_ret, p_ret, p_ret, p_ret, cos, sin)


def _ret_step_kernel(lg_ref, q_ref, k_ref, v_ref, g_ref, cos_ref, sin_ref, s_ref, *rest, bb):
    o_ref, s_out_ref = rest[-2:]
    h = pl.program_id(0)
    gamma = jnp.exp(lg_ref[h] * jnp.ones((1, 1), F32))
    cos, sin = cos_ref[...], sin_ref[...]
    q = _rotary(q_ref[...], cos, sin)
    k = _rotary(k_ref[...], cos, sin) * (R_DK ** -0.5)
    v = v_ref[...]
    eye = _eye(R_DK)
    q_cols = _dot_sel(eye, q, _NT)
    k_cols = _dot_sel(eye, k, _NT)
    for i in range(bb):
        s_new = gamma * s_ref[i, 0] + k_cols[:, i:i + 1] * v[i:i + 1, :]
        s_out_ref[i, 0] = s_new
        o = jnp.sum(q_cols[:, i:i + 1] * s_new, axis=0, keepdims=True)
        o_ref[i:i + 1, :] = (_silu(g_ref[i:i + 1, :]) * _rms_rows(o)).astype(o_ref.dtype)


def _alias_args(first_index, outputs):
    args, specs, aliases = [], [], {}
    for out_idx, a in outputs.items():
        if a is not None:
            aliases[first_index + len(args)] = out_idx
            args.append(a)
            specs.append(pl.BlockSpec(memory_space=pl.ANY))
    return args, specs, aliases


def retention_step(p_ret, row0, cos, sin, log_gamma, states, layer, o_all, s_all, bb=8):
    nb = states.shape[1]
    r0 = row0 // bb
    rows = lambda off: pl.BlockSpec((bb, R_DK), functools.partial(lambda h, j, off: (r0 + j, off + h), off=off))
    tab = pl.BlockSpec((1, R_DK // 2), lambda h, j: (0, 0))
    st = pl.BlockSpec((None, bb, 1, R_DK, R_DK), lambda h, j: (layer, j, h, 0, 0))
    extra, extra_specs, aliases = _alias_args(8, {0: o_all, 1: s_all})
    return pl.pallas_call(
        functools.partial(_ret_step_kernel, bb=bb),
        grid=(R_HEADS, nb // bb),
        in_specs=[pl.BlockSpec(memory_space=pltpu.SMEM), rows(0), rows(R_HEADS), rows(2 * R_HEADS),
                  rows(3 * R_HEADS), tab, tab, st, *extra_specs],
        out_specs=[pl.BlockSpec((bb, R_DK), lambda h, j: (r0 + j, h)), st],
        out_shape=[jax.ShapeDtypeStruct(o_all.shape, BF16), jax.ShapeDtypeStruct(states.shape, F32)],
        input_output_aliases=aliases,
        compiler_params=_cparams(("parallel", "parallel")),
        name="ret_step",
    )(log_gamma, p_ret, p_ret, p_ret, p_ret, cos, sin, states, *extra)


def _cmul(ar, ai, br, bi):
    return ar * br - ai * bi, ar * bi + ai * br


def s5_discretize(lam_re, lam_im, log_dt, b_re, b_im):
    dt = jnp.exp(log_dt)[:, None]
    mag = jnp.exp(lam_re * dt)
    lbr, lbi = mag * jnp.cos(lam_im * dt), mag * jnp.sin(lam_im * dt)
    den = lam_re * lam_re + lam_im * lam_im
    nr, ni = lbr - 1.0, lbi
    cr, ci = (nr * lam_re + ni * lam_im) / den, (ni * lam_re - nr * lam_im) / den
    bbr, bbi = _cmul(cr[..., None], ci[..., None], b_re, b_im)
    return dt, lbr, lbi, bbr, bbi


def _lam_power(lam_re, lam_im, dt, n):
    n = jnp.asarray(n, F32).reshape((-1,) + (1,) * lam_re.ndim)
    mag = jnp.exp(lam_re * dt * n)
    return mag * jnp.cos(lam_im * dt * n), mag * jnp.sin(lam_im * dt * n)


def s5_prompt_params(lam_re, lam_im, log_dt, b_re, b_im, c_re, c_im, d_skip, nchb):
    g, p, q = S5_GROUPS, S5_STATE, S5_CHUNK
    dt, lbr, lbi, bbr, bbi = s5_discretize(lam_re, lam_im, log_dt, b_re, b_im)
    pr, pi = _lam_power(lam_re, lam_im, dt, jnp.arange(q + 1))
    lb_r, lb_i = _cmul(pr[:q, :, :, None], pi[:q, :, :, None], bbr[None], bbi[None])
    hp = lax.Precision.HIGHEST
    kern = (jnp.einsum('gkp,tgpc->tgkc', c_re, lb_r, precision=hp)
            - jnp.einsum('gkp,tgpc->tgkc', c_im, lb_i, precision=hp))
    s_i = jnp.arange(q)[:, None]
    t_i = jnp.arange(q)[None, :]
    tau = jnp.clip(t_i - s_i, 0, q - 1)
    m = jnp.where((t_i >= s_i)[:, :, None, None, None], kern[tau], 0.0)
    m = m.transpose(2, 0, 4, 1, 3).reshape(g, q * S5_GROUP, q * S5_GROUP)
    er, ei = _cmul(pr[q - 1 - jnp.arange(q)][:, :, :, None], pi[q - 1 - jnp.arange(q)][:, :, :, None],
                   bbr[None], bbi[None])
    w_re = er.transpose(1, 0, 3, 2).reshape(g, q * S5_GROUP, p)
    w_im = ei.transpose(1, 0, 3, 2).reshape(g, q * S5_GROUP, p)
    vr, vi = _cmul(c_re[None], c_im[None], pr[1:, :, None, :], pi[1:, :, None, :])
    v_re = vr.transpose(1, 3, 0, 2).reshape(g, p, q * S5_GROUP)
    v_im = (-vi).transpose(1, 3, 0, 2).reshape(g, p, q * S5_GROUP)

    def pair_cols(a):
        z = jnp.zeros_like(a)
        even = (jnp.arange(g) % 2 == 0)[:, None, None]
        return jnp.concatenate([jnp.where(even, a, z), jnp.where(even, z, a)], axis=-1)

    def pair_rows(a):
        z = jnp.zeros_like(a)
        even = (jnp.arange(g) % 2 == 0)[:, None, None]
        return jnp.concatenate([jnp.where(even, a, z), jnp.where(even, z, a)], axis=1)

    nlog = max(int(math.log2(nchb)), 1)
    sr, si = _lam_power(lam_re, lam_im, dt, q * (2.0 ** jnp.arange(8)))
    lam_sc_re = sr.reshape(8, g // 2, 2 * p).transpose(1, 0, 2)
    lam_sc_im = si.reshape(8, g // 2, 2 * p).transpose(1, 0, 2)
    d_vec = jnp.tile(d_skip.reshape(g, 1, S5_GROUP), (1, q, 1)).reshape(g, 1, q * S5_GROUP)
    del nlog
    return (m, pair_cols(w_re), pair_cols(w_im), pair_rows(v_re), pair_rows(v_im), lam_sc_re, lam_sc_im, d_vec)


S5_TILE_GROUPS = 128 // S5_GROUP


def _s5_prompt_kernel(u_ref, m_ref, wre_ref, wim_ref, vre_ref, vim_ref, lre_ref, lim_ref, d_ref,
                      h_ref, fre_ref, fim_ref, ut_ref, ht_ref, tile_ref, *, nchb, nb):
    rows = nb * nchb
    q, gw, ng = S5_CHUNK, S5_GROUP, S5_TILE_GROUPS
    for s in range(q):
        xs = u_ref[pl.ds(s, rows, stride=q), :]
        for gg in range(ng):
            ut_ref[gg, :, s * gw:(s + 1) * gw] = xs[:, gg * gw:(gg + 1) * gw]

    pos = jnp.bitwise_and(lax.broadcasted_iota(jnp.int32, (rows, 2 * S5_STATE), 0), nchb - 1)
    first = pos == 0
    pairs = range(ng // 2)
    u_l = [(ut_ref[2 * pr], ut_ref[2 * pr + 1]) for pr in pairs]
    e_re = [_dot3(u_l[pr][0], wre_ref[2 * pr]) + _dot3(u_l[pr][1], wre_ref[2 * pr + 1]) for pr in pairs]
    e_im = [_dot3(u_l[pr][0], wim_ref[2 * pr]) + _dot3(u_l[pr][1], wim_ref[2 * pr + 1]) for pr in pairs]
    d, k = 1, 0
    while d < nchb:
        keep = pos >= d
        for pr in pairs:
            lr, li = lre_ref[pr, k:k + 1, :], lim_ref[pr, k:k + 1, :]
            sr, si = pltpu.roll(e_re[pr], d, 0), pltpu.roll(e_im[pr], d, 0)
            e_re[pr], e_im[pr] = (e_re[pr] + jnp.where(keep, lr * sr - li * si, 0.0),
                                  e_im[pr] + jnp.where(keep, lr * si + li * sr, 0.0))
        d, k = d * 2, k + 1
    for pr in pairs:
        for b in range(nb):
            r = (b + 1) * nchb - 1
            fre_ref[pr, b:b + 1, :] = e_re[pr][r:r + 1, :]
            fim_ref[pr, b:b + 1, :] = e_im[pr][r:r + 1, :]
    for pr in pairs:
        xc_re = jnp.where(first, 0.0, pltpu.roll(e_re[pr], 1, 0))
        xc_im = jnp.where(first, 0.0, pltpu.roll(e_im[pr], 1, 0))
        for gi in range(2):
            gg = 2 * pr + gi
            u = u_l[pr][gi]
            y = (_dot3(u, m_ref[gg]) + _dot3(xc_re, vre_ref[gg]) + _dot3(xc_im, vim_ref[gg])) + u * d_ref[gg]
            ht_ref[gg] = _gelu_tanh(y)
    for s in range(q):
        for gg in range(ng):
            tile_ref[:, gg * gw:(gg + 1) * gw] = ht_ref[gg, :, s * gw:(s + 1) * gw]
        h_ref[pl.ds(s, rows, stride=q), :] = tile_ref[...]


def s5_prompt(p_s5, params, batch, seq):
    m, w_re, w_im, v_re, v_im, l_re, l_im, d_vec = params
    ng, qc, p2 = S5_TILE_GROUPS, S5_CHUNK * S5_GROUP, 2 * S5_STATE
    nchb = seq // S5_CHUNK
    rows = batch * nchb
    tp = batch * seq
    nt = BRANCH_WIDTH // 128
    grp = lambda n, *tail: pl.BlockSpec((n,) + tail, lambda j: (j,) + (0,) * len(tail))
    return pl.pallas_call(
        functools.partial(_s5_prompt_kernel, nchb=nchb, nb=batch),
        grid=(nt,),
        in_specs=[pl.BlockSpec((tp, 128), lambda j: (0, j)), grp(ng, qc, qc), grp(ng, qc, p2), grp(ng, qc, p2),
                  grp(ng, p2, qc), grp(ng, p2, qc), grp(ng // 2, 8, p2), grp(ng // 2, 8, p2), grp(ng, 1, qc)],
        out_specs=[pl.BlockSpec((tp, 128), lambda j: (0, j)), grp(ng // 2, 8, p2), grp(ng // 2, 8, p2)],
        out_shape=[jax.ShapeDtypeStruct(p_s5.shape, F32),
                   jax.ShapeDtypeStruct((S5_GROUPS // 2, 8, p2), F32),
                   jax.ShapeDtypeStruct((S5_GROUPS // 2, 8, p2), F32)],
        scratch_shapes=[pltpu.VMEM((ng, rows, qc), F32), pltpu.VMEM((ng, rows, qc), F32),
                        pltpu.VMEM((rows, 128), F32)],
        compiler_params=_cparams(("parallel",)),
        name="s5_prompt",
    )(p_s5, m, w_re, w_im, v_re, v_im, l_re, l_im, d_vec)


def s5_step_params(lam_re, lam_im, log_dt, b_re, b_im, c_re, c_im, d_skip):
    g, p = S5_GROUPS, S5_STATE
    _, lbr, lbi, bbr, bbi = s5_discretize(lam_re, lam_im, log_dt, b_re, b_im)
    eye8 = jnp.eye(8, dtype=F32)

    def bd_in(bb):
        x = bb.reshape(8, 8, p, S5_GROUP)
        return jnp.einsum('jgpc,gh->jgchp', x, eye8).reshape(8, 8 * S5_GROUP, 8 * p)

    def bd_out(cc):
        x = cc.reshape(8, 8, S5_GROUP, p)
        return jnp.einsum('jgkp,gh->jgphk', x, eye8).reshape(8, 8 * p, 8 * S5_GROUP)

    return (bd_in(bbr), bd_in(bbi), bd_out(c_re), bd_out(-c_im), lbr.reshape(1, g * p), lbi.reshape(1, g * p),
            d_skip.reshape(1, g * S5_GROUP))


def _s5_step_kernel(u_ref, xr_ref, xi_ref, bre_ref, bim_ref, cre_ref, cim_ref, lr_ref, li_ref, d_ref, *rest):
    h_ref, nr_ref, ni_ref = rest[-3:]
    u = u_ref[...]
    lr, li = lr_ref[...], li_ref[...]
    xr, xi = xr_ref[...], xi_ref[...]
    nr = lr * xr - li * xi + _dot3(u, bre_ref[0])
    ni = lr * xi + li * xr + _dot3(u, bim_ref[0])
    nr_ref[...] = nr
    ni_ref[...] = ni
    y = _dot3(nr, cre_ref[0]) + _dot3(ni, cim_ref[0]) + u * d_ref[...]
    h_ref[...] = _gelu_tanh(y)


def s5_step(p_s5, row0, x_re, x_im, params, h_all):
    b_re, b_im, c_re, c_im, l_re, l_im, d_vec = params
    nb, gp = x_re.shape
    lt = 8 * S5_GROUP
    st = 8 * S5_STATE
    r0 = row0 // nb
    u_spec = pl.BlockSpec((nb, lt), lambda j: (r0, j))
    x_spec = pl.BlockSpec((nb, st), lambda j: (0, j))
    extra, extra_specs, aliases = _alias_args(10, {0: h_all})
    return pl.pallas_call(
        _s5_step_kernel,
        grid=(gp // st,),
        in_specs=[u_spec, x_spec, x_spec,
                  pl.BlockSpec((1, lt, st), lambda j: (j, 0, 0)), pl.BlockSpec((1, lt, st), lambda j: (j, 0, 0)),
                  pl.BlockSpec((1, st, lt), lambda j: (j, 0, 0)), pl.BlockSpec((1, st, lt), lambda j: (j, 0, 0)),
                  pl.BlockSpec((1, st), lambda j: (0, j)), pl.BlockSpec((1, st), lambda j: (0, j)),
                  pl.BlockSpec((1, lt), lambda j: (0, j)), *extra_specs],
        out_specs=[u_spec, x_spec, x_spec],
        out_shape=[jax.ShapeDtypeStruct(h_all.shape, F32), jax.ShapeDtypeStruct((nb, gp), F32),
                   jax.ShapeDtypeStruct((nb, gp), F32)],
        input_output_aliases=aliases,
        compiler_params=_cparams(("parallel",)),
        name="s5_step",
    )(p_s5, x_re, x_im, b_re, b_im, c_re, c_im, l_re, l_im, d_vec, *extra)


def _l2norm_rows(x):
    return x * lax.rsqrt(jnp.sum(x * x, axis=-1, keepdims=True) + NORM_EPS)


def _gdn_prompt_kernel(qkv_ref, z_ref, ab_ref, cw_ref, alog_ref, dtb_ref, nw_ref, o_ref, s_out_ref,
                       xbuf_ref, s_ref, *, nc, chunk):
    c = pl.program_id(1)
    bw = BRANCH_WIDTH
    halo = 8

    @pl.when(c == 0)
    def _():
        xbuf_ref[0:halo, :] = jnp.zeros((halo, 3 * bw), F32)
        s_ref[...] = jnp.zeros_like(s_ref)

    x = qkv_ref[...]
    xbuf_ref[halo:halo + chunk, :] = x
    conv = x * cw_ref[G_CONV - 1:G_CONV, :]
    for i in range(1, G_CONV):
        conv = conv + xbuf_ref[halo - i:halo - i + chunk, :] * cw_ref[G_CONV - 1 - i:G_CONV - i, :]
    xbuf_ref[0:halo, :] = x[chunk - halo:chunk, :]
    qkv = _silu(conv)

    ab = ab_ref[...]
    g_all = -jnp.exp(alog_ref[...]) * _softplus(ab + dtb_ref[...])
    beta_all = _sigmoid(ab)
    ri = lax.broadcasted_iota(jnp.int32, (chunk, chunk), 0)
    ci = lax.broadcasted_iota(jnp.int32, (chunk, chunk), 1)
    causal = ri >= ci
    strict = ri > ci
    ltri = jnp.where(causal, 1.0, 0.0).astype(BF16)
    gcum = _dot_sel(ltri, g_all)
    gcum_t = _dot_sel(_eye(128), gcum, _NT)
    z = z_ref[...]
    nw = nw_ref[...]

    heads = range(G_HEADS)
    q_l, gc_l, eg_l, qkd_l, kdec_l, m_l, sol_l = [], [], [], [], [], [], []
    for h in heads:
        q = _l2norm_rows(qkv[:, h * G_DK:(h + 1) * G_DK]) * (G_DK ** -0.5)
        k = _l2norm_rows(qkv[:, bw + h * G_DK:bw + (h + 1) * G_DK])
        v = qkv[:, 2 * bw + h * G_DK:2 * bw + (h + 1) * G_DK]
        gc = gcum[:, h:h + 1]
        gr = gcum_t[h:h + 1, :]
        beta = beta_all[:, G_HEADS + h:G_HEADS + h + 1]
        decay = jnp.where(causal, jnp.exp(jnp.where(causal, gc - gr, 0.0)), 0.0)
        kb = k.astype(BF16)
        qk_kk = _dg(jnp.concatenate([kb, q.astype(BF16)], axis=0), kb, _NT)
        eg = jnp.exp(gc)
        q_l.append(q * eg)
        gc_l.append(gc)
        qkd_l.append((qk_kk[chunk:] * decay).astype(BF16))
        kdec_l.append((k * jnp.exp(gc[chunk - 1:chunk, :] - gc)).astype(BF16))
        m_l.append(jnp.where(strict, -(beta * qk_kk[:chunk] * decay), 0.0))
        sol_l.append(jnp.concatenate([beta * v, (beta * eg) * k], axis=-1))
    span = 1
    while span < chunk:
        for h in heads:
            if 2 * span < chunk:
                r = _dot3(m_l[h], jnp.concatenate([sol_l[h], m_l[h]], axis=-1))
                sol_l[h], m_l[h] = sol_l[h] + r[:, :2 * G_DK], r[:, 2 * G_DK:]
            else:
                sol_l[h] = sol_l[h] + _dot3(m_l[h], sol_l[h])
        span *= 2
    s_l = [s_ref[h] for h in heads]
    ks_qs_l = [_dg(jnp.concatenate([sol_l[h][:, G_DK:], q_l[h]], axis=0).astype(BF16), s_l[h].astype(BF16), _NN)
               for h in heads]
    wb_l = [(sol_l[h][:, :G_DK] - ks_qs_l[h][:chunk]).astype(BF16) for h in heads]
    for h in heads:
        o = ks_qs_l[h][chunk:] + _dg(qkd_l[h], wb_l[h], _NN)
        zh = z[:, h * G_DK:(h + 1) * G_DK]
        o_ref[:, h * G_DK:(h + 1) * G_DK] = (_rms_rows(o) * nw * _silu(zh)).astype(o_ref.dtype)
    for h in heads:
        g_last = gc_l[h][chunk - 1:chunk, :]
        s_ref[h] = jnp.exp(g_last) * s_l[h] + _dg(kdec_l[h], wb_l[h], _TN)

    @pl.when(c == nc - 1)
    def _():
        s_out_ref[0] = s_ref[...]


def gdn_prompt(p_gdn, conv_w, a_log, dt_bias, norm_w, batch, seq, chunk=G_CHUNK):
    nc = seq // chunk
    bw = BRANCH_WIDTH
    pad = lambda x: jnp.zeros((1, 128), F32).at[0, :G_HEADS].set(x)
    full = lambda shape: pl.BlockSpec(shape, lambda b, c: (0,) * len(shape))
    return pl.pallas_call(
        functools.partial(_gdn_prompt_kernel, nc=nc, chunk=chunk),
        grid=(batch, nc),
        in_specs=[pl.BlockSpec((chunk, 3 * bw), lambda b, c: (b * nc + c, 0)),
                  pl.BlockSpec((chunk, bw), lambda b, c: (b * nc + c, 3)),
                  pl.BlockSpec((chunk, 128), lambda b, c: (b * nc + c, 4 * bw // 128)),
                  full((G_CONV, 3 * bw)), full((1, 128)), full((1, 128)), full((1, G_DK))],
        out_specs=[pl.BlockSpec((chunk, bw), lambda b, c: (b * nc + c, 0)),
                   pl.BlockSpec((1, G_HEADS, G_DK, G_DK), lambda b, c: (b, 0, 0, 0))],
        out_shape=[jax.ShapeDtypeStruct((p_gdn.shape[0], bw), BF16),
                   jax.ShapeDtypeStruct((batch, G_HEADS, G_DK, G_DK), F32)],
        scratch_shapes=[pltpu.VMEM((8 + chunk, 3 * bw), F32), pltpu.VMEM((G_HEADS, G_DK, G_DK), F32)],
        compiler_params=_cparams(("parallel", "arbitrary")),
        name="gdn_prompt",
    )(p_gdn, p_gdn, p_gdn, conv_w, pad(a_log), pad(dt_bias), norm_w.reshape(1, G_DK))


def _gdn_step_kernel(qkv_ref, z_ref, ab_ref, cbuf_ref, cw_ref, alog_ref, dtb_ref, nw_ref, s_ref, *rest, bb):
    o_ref, cbuf_out_ref, s_out_ref = rest[-3:]
    bw = BRANCH_WIDTH
    x = qkv_ref[...]
    conv = x * cw_ref[G_CONV - 1:G_CONV, :]
    for i in range(G_CONV - 1):
        conv = conv + cbuf_ref[i] * cw_ref[i:i + 1, :]
    for i in range(G_CONV - 2):
        cbuf_out_ref[i] = cbuf_ref[i + 1]
    cbuf_out_ref[G_CONV - 2] = x
    qkv = _silu(conv)
    ab = ab_ref[...]
    g_all = -jnp.exp(alog_ref[...]) * _softplus(ab + dtb_ref[...])
    beta_all = _sigmoid(ab)
    eye = _eye(G_DK)
    z = z_ref[...]
    nw = nw_ref[...]
    for h in range(G_HEADS):
        q = _l2norm_rows(qkv[:, h * G_DK:(h + 1) * G_DK]) * (G_DK ** -0.5)
        k = _l2norm_rows(qkv[:, bw + h * G_DK:bw + (h + 1) * G_DK])
        v = qkv[:, 2 * bw + h * G_DK:2 * bw + (h + 1) * G_DK]
        eg = jnp.exp(g_all[:, h:h + 1])
        beta = beta_all[:, G_HEADS + h:G_HEADS + h + 1]
        q_cols = _dot_sel(eye, q, _NT)
        k_cols = _dot_sel(eye, k, _NT)
        rng = range(bb)
        s_l = [s_ref[i, h] for i in rng]
        ks_l = [jnp.sum(k_cols[:, i:i + 1] * s_l[i], axis=0, keepdims=True) for i in rng]
        w_l = [beta[i:i + 1, :] * (v[i:i + 1, :] - eg[i:i + 1, :] * ks_l[i]) for i in rng]
        sn_l = [eg[i:i + 1, :] * s_l[i] + k_cols[:, i:i + 1] * w_l[i] for i in rng]
        for i in rng:
            s_out_ref[i, h] = sn_l[i]
        o = jnp.concatenate([jnp.sum(q_cols[:, i:i + 1] * sn_l[i], axis=0, keepdims=True) for i in rng], axis=0)
        zh = z[:, h * G_DK:(h + 1) * G_DK]
        o_ref[:, h * G_DK:(h + 1) * G_DK] = (_rms_rows(o) * nw * _silu(zh)).astype(o_ref.dtype)


def gdn_step(p_gdn, row0, conv_buf_t, states, layer, conv_w, a_log, dt_bias, norm_w, o_all, s_all, bb=8):
    nb = states.shape[1]
    bw = BRANCH_WIDTH
    r0 = row0 // bb
    pad = lambda x: jnp.zeros((1, 128), F32).at[0, :G_HEADS].set(x)
    full = lambda shape: pl.BlockSpec(shape, lambda j: (0,) * len(shape))
    cb = pl.BlockSpec((G_CONV - 1, bb, 3 * bw), lambda j: (0, j, 0))
    st = pl.BlockSpec((None, bb, G_HEADS, G_DK, G_DK), lambda j: (layer, j, 0, 0, 0))
    extra, extra_specs, aliases = _alias_args(9, {0: o_all, 2: s_all})
    return pl.pallas_call(
        functools.partial(_gdn_step_kernel, bb=bb),
        grid=(nb // bb,),
        in_specs=[pl.BlockSpec((bb, 3 * bw), lambda j: (r0 + j, 0)),
                  pl.BlockSpec((bb, bw), lambda j: (r0 + j, 3)),
                  pl.BlockSpec((bb, 128), lambda j: (r0 + j, 4 * bw // 128)),
                  cb, full((G_CONV, 3 * bw)), full((1, 128)), full((1, 128)), full((1, G_DK)), st, *extra_specs],
        out_specs=[pl.BlockSpec((bb, bw), lambda j: (r0 + j, 0)), cb, st],
        out_shape=[jax.ShapeDtypeStruct(o_all.shape, BF16), jax.ShapeDtypeStruct(conv_buf_t.shape, F32),
                   jax.ShapeDtypeStruct(states.shape, F32)],
        input_output_aliases=aliases,
        compiler_params=_cparams(("parallel",)),
        name="gdn_step",
    )(p_gdn, p_gdn, p_gdn, conv_buf_t, conv_w, pad(a_log), pad(dt_bias), norm_w.reshape(1, G_DK), states, *extra)


def _rope_tables(pos):
    half = R_DK // 2
    inv = ROPE_BASE ** (-jnp.arange(half, dtype=F32) / half)
    ang = pos.astype(F32)[:, None] * inv[None, :]
    return jnp.cos(ang), jnp.sin(ang)


def kernel(x_prompt, x_sample, state_ret, state_s5_re, state_s5_im, state_gdn, state_gdn_conv, norm_mix_w, w_in, s5_lambda_re, s5_lambda_im, s5_log_dt, s5_b_re, s5_b_im, s5_c_re, s5_c_im, s5_d, s5_w_glu, s5_b_glu, gdn_conv_w, gdn_a_log, gdn_dt_bias, gdn_norm_w, w_branch, w_out, norm_ffn_w, w_gate_ffn, w_up_ffn, w_down_ffn, norm_final_w):
    bp, lp, d = x_prompt.shape
    nb = x_sample.shape[0]
    depth = w_in.shape[0]
    tp = bp * lp
    bw = BRANCH_WIDTH
    x = jnp.concatenate([x_prompt.reshape(tp, d), x_sample.reshape(nb, d)], axis=0)

    cos_p, sin_p = _rope_tables(jnp.arange(lp, dtype=jnp.int32))
    cos_s, sin_s = _rope_tables(PAST_LEN + jnp.arange(1, dtype=jnp.int32))
    log_gamma = jnp.log1p(-jnp.exp2(-5.0 - jnp.arange(R_HEADS, dtype=F32)))
    nchb = lp // S5_CHUNK

    new_p = [[], [], [], [], []]
    new_s = [[], [], []]
    ret_s_all = gdn_s_all = None
    xn = rmsnorm_rows(x, norm_mix_w[0], BF16)
    y_final = None
    for l in range(depth):
        wl = w_in[l]
        o_su, o_gqkv, o_ga, o_gz, o_gates = 4 * bw, 5 * bw, 8 * bw, 8 * bw + 2 * G_HEADS, 9 * bw + 2 * G_HEADS
        w_ret = wl[:, :o_su].astype(BF16)
        w_s5 = wl[:, o_su:o_gqkv].astype(BF16)
        w_gdn = jnp.concatenate([wl[:, o_gqkv:o_ga], wl[:, o_gz:o_gates], wl[:, o_ga:o_gz],
                                 jnp.zeros((d, 128 - 2 * G_HEADS), F32)], axis=1).astype(BF16)
        w_gate = wl[:, o_gates:].astype(BF16)

        p_ret = matmul(xn, w_ret, tn=1024, name="proj_ret")
        p_s5 = matmul(xn, w_s5, tn=1024, name="proj_s5")
        p_gdn = matmul(xn, w_gdn, tn=1408, name="proj_gdn")

        r_all, ret_p = retention_prompt(p_ret, cos_p, sin_p, log_gamma, bp, lp)
        r_all, ret_s_all = retention_step(p_ret, tp, cos_s, sin_s, log_gamma, state_ret, l, r_all, ret_s_all)

        s5_raw = (s5_lambda_re[l], s5_lambda_im[l], s5_log_dt[l], s5_b_re[l], s5_b_im[l], s5_c_re[l], s5_c_im[l],
                  s5_d[l])
        h_all, f_re, f_im = s5_prompt(p_s5, s5_prompt_params(*s5_raw, nchb), bp, lp)
        fin = lambda f: f[:, :bp, :].reshape(S5_GROUPS // 2, bp, 2, S5_STATE).transpose(1, 0, 2, 3).reshape(
            bp, S5_GROUPS, S5_STATE)
        h_all, s5re_s, s5im_s = s5_step(p_s5, tp, state_s5_re[l].reshape(nb, -1), state_s5_im[l].reshape(nb, -1),
                                        s5_step_params(*s5_raw), h_all)
        s_all = s5_glu(h_all, s5_w_glu[l].astype(BF16), s5_b_glu[l])

        g_all, gdn_p = gdn_prompt(p_gdn, gdn_conv_w[l], gdn_a_log[l], gdn_dt_bias[l], gdn_norm_w[l], bp, lp)
        g_all, conv_s_t, gdn_s_all = gdn_step(p_gdn, tp, state_gdn_conv[l].transpose(1, 0, 2), state_gdn, l,
                                              gdn_conv_w[l], gdn_a_log[l], gdn_dt_bias[l], gdn_norm_w[l],
                                              g_all, gdn_s_all)
        conv_p = jnp.stack([p_gdn[(b + 1) * lp - (G_CONV - 1):(b + 1) * lp, :3 * bw] for b in range(bp)], axis=0)

        merged = merge_branches(xn, r_all, s_all, g_all, w_gate, w_branch[l].astype(BF16))
        h, hn = out_proj(merged, w_out[l].astype(BF16), x, norm_ffn_w[l])
        last = l == depth - 1
        next_w = norm_final_w if last else norm_mix_w[l + 1]
        x, xn = ffn(hn, h, w_gate_ffn[l].astype(BF16), w_up_ffn[l].astype(BF16), w_down_ffn[l].astype(BF16),
                    next_w, F32 if last else BF16)
        if last:
            y_final = xn

        for lst, val in zip(new_p, (ret_p, fin(f_re), fin(f_im), gdn_p, conv_p)):
            lst.append(val)
        for lst, val in zip(new_s, (s5re_s.reshape(nb, S5_GROUPS, S5_STATE),
                                    s5im_s.reshape(nb, S5_GROUPS, S5_STATE), conv_s_t.transpose(1, 0, 2))):
            lst.append(val)

    y_prompt = y_final[:tp].reshape(bp, lp, d)
    y_sample = y_final[tp:].reshape(nb, 1, d)
    outs_p = [jnp.stack(a, 0) for a in new_p]
    s5re_s, s5im_s, conv_s = [jnp.stack(a, 0) for a in new_s]
    return (y_prompt, y_sample, *outs_p, ret_s_all, s5re_s, s5im_s, gdn_s_all, conv_s)
```

```python
import functools
import math

import jax
import jax.numpy as jnp
from jax import lax
from jax.experimental import pallas as pl
from jax.experimental.pallas import tpu as pltpu

F32 = jnp.float32
BF16 = jnp.bfloat16

D_MODEL = 2048
BRANCH_WIDTH = 1024
N_BRANCH = 3
R_HEADS = 4
R_DK = 256
R_CHUNK = 128
ROPE_BASE = 10000.0
S5_GROUP = 16
S5_GROUPS = 64
S5_STATE = 64
S5_CHUNK = 16
G_HEADS = 8
G_DK = 128
G_CONV = 4
G_CHUNK = 64
NORM_EPS = 1e-6
PAST_LEN = 16384

VMEM_LIMIT_BYTES = 56 * 1024 * 1024
TM = 640


def _cparams(sem):
    return pltpu.CompilerParams(dimension_semantics=sem, vmem_limit_bytes=VMEM_LIMIT_BYTES)


_NN = (((1,), (0,)), ((), ()))
_NT = (((1,), (1,)), ((), ()))
_TN = (((0,), (0,)), ((), ()))


def _dg(a, b, dims):
    return lax.dot_general(a, b, dims, preferred_element_type=F32)


def _split2(a):
    hi = a.astype(BF16)
    lo = (a - hi.astype(F32)).astype(BF16)
    return hi, lo


def _dot3(a, b, dims=_NN):
    ah, al = _split2(a)
    bh, bl = _split2(b)
    return _dg(ah, bh, dims) + (_dg(ah, bl, dims) + _dg(al, bh, dims))


def _dot2(a_bf16, b, dims=_NN):
    bh, bl = _split2(b)
    return _dg(a_bf16, bh, dims) + _dg(a_bf16, bl, dims)


def _dot_sel(sel_bf16, x, dims=_NN):
    h1 = x.astype(BF16)
    r1 = x - h1.astype(F32)
    h2 = r1.astype(BF16)
    h3 = (r1 - h2.astype(F32)).astype(BF16)
    return _dg(sel_bf16, h1, dims) + (_dg(sel_bf16, h2, dims) + _dg(sel_bf16, h3, dims))


def _sel_dot(x, sel_bf16, dims=_NN):
    h1 = x.astype(BF16)
    r1 = x - h1.astype(F32)
    h2 = r1.astype(BF16)
    h3 = (r1 - h2.astype(F32)).astype(BF16)
    return _dg(h1, sel_bf16, dims) + (_dg(h2, sel_bf16, dims) + _dg(h3, sel_bf16, dims))


def _eye(n, dtype=BF16):
    r = lax.broadcasted_iota(jnp.int32, (n, n), 0)
    c = lax.broadcasted_iota(jnp.int32, (n, n), 1)
    return jnp.where(r == c, 1.0, 0.0).astype(dtype)


def _sigmoid(x):
    return 1.0 / (1.0 + jnp.exp(-x))


def _silu(x):
    return x * _sigmoid(x)


def _gelu_tanh(x):
    return 0.5 * x * (1.0 + jnp.tanh(math.sqrt(2.0 / math.pi) * (x + 0.044715 * (x * x * x))))


def _softplus(x):
    return jnp.maximum(x, 0.0) + jnp.log1p(jnp.exp(-jnp.abs(x)))


def _rms_rows(x, eps=NORM_EPS):
    return x * lax.rsqrt(jnp.mean(x * x, axis=-1, keepdims=True) + eps)


def _norm_kernel(x_ref, w_ref, o_ref):
    o_ref[...] = (_rms_rows(x_ref[...]) * w_ref[...]).astype(o_ref.dtype)


def rmsnorm_rows(x, w, out_dtype, tm=TM):
    t, d = x.shape
    return pl.pallas_call(
        _norm_kernel,
        grid=(t // tm,),
        in_specs=[pl.BlockSpec((tm, d), lambda i: (i, 0)), pl.BlockSpec((1, d), lambda i: (0, 0))],
        out_specs=pl.BlockSpec((tm, d), lambda i: (i, 0)),
        out_shape=jax.ShapeDtypeStruct((t, d), out_dtype),
        compiler_params=_cparams(("parallel",)),
        name="rmsnorm",
    )(x, w.reshape(1, d))


def _mm_kernel(a_ref, w_ref, o_ref):
    o_ref[...] = jnp.dot(a_ref[...], w_ref[...], preferred_element_type=F32)


def matmul(a, w, layer, col0, n, tn, tm=TM, name="proj"):
    t, k = a.shape
    c0 = col0 // tn
    return pl.pallas_call(
        _mm_kernel,
        grid=(t // tm, n // tn),
        in_specs=[pl.BlockSpec((tm, k), lambda i, j: (i, 0)),
                  pl.BlockSpec((None, k, tn), lambda i, j: (layer, 0, c0 + j))],
        out_specs=pl.BlockSpec((tm, tn), lambda i, j: (i, j)),
        out_shape=jax.ShapeDtypeStruct((t, n), F32),
        compiler_params=_cparams(("parallel", "arbitrary")),
        name=name,
    )(a, w)


def _merge_kernel(xn_ref, br_r, br_s, br_g, wg0, wg1, wg2, wb_ref, o_ref):
    xn = xn_ref[...]
    acc = None
    for i, (br, wg) in enumerate(((br_r, wg0), (br_s, wg1), (br_g, wg2))):
        gate = _sigmoid(jnp.dot(xn, wg[...], preferred_element_type=F32))
        up = jnp.dot(br[...], wb_ref[i], preferred_element_type=F32)
        acc = gate * up if acc is None else acc + gate * up
    o_ref[...] = acc.astype(o_ref.dtype)


def merge_branches(xn, br_r, br_s, br_g, w_in, gate_col0, w_branch, layer, tn=512, tm=TM):
    t, d = xn.shape
    bw = br_r.shape[1]
    nj = d // tn
    c0 = gate_col0 // tn
    a_spec = pl.BlockSpec((tm, bw), lambda i, j: (i, 0))
    wg_specs = [pl.BlockSpec((None, d, tn), functools.partial(lambda i, j, n: (layer, 0, c0 + n * nj + j), n=n))
                for n in range(N_BRANCH)]
    return pl.pallas_call(
        _merge_kernel,
        grid=(t // tm, nj),
        in_specs=[pl.BlockSpec((tm, d), lambda i, j: (i, 0)), a_spec, a_spec, a_spec, *wg_specs,
                  pl.BlockSpec((None, N_BRANCH, bw, tn), lambda i, j: (layer, 0, 0, j))],
        out_specs=pl.BlockSpec((tm, tn), lambda i, j: (i, j)),
        out_shape=jax.ShapeDtypeStruct((t, d), BF16),
        compiler_params=_cparams(("parallel", "arbitrary")),
        name="merge",
    )(xn, br_r, br_s, br_g, w_in, w_in, w_in, w_branch)


def _outproj_kernel(m_ref, w_ref, x_ref, nw_ref, h_ref, hn_ref):
    h = x_ref[...] + jnp.dot(m_ref[...], w_ref[...], preferred_element_type=F32)
    h_ref[...] = h
    hn_ref[...] = (_rms_rows(h) * nw_ref[...]).astype(hn_ref.dtype)


def out_proj(merged, w_out, layer, x, norm_w, tm=TM):
    t, d = x.shape
    row = pl.BlockSpec((tm, d), lambda i: (i, 0))
    return pl.pallas_call(
        _outproj_kernel,
        grid=(t // tm,),
        in_specs=[row, pl.BlockSpec((None, d, d), lambda i: (layer, 0, 0)), row,
                  pl.BlockSpec((1, d), lambda i: (0, 0))],
        out_specs=[row, row],
        out_shape=[jax.ShapeDtypeStruct((t, d), F32), jax.ShapeDtypeStruct((t, d), BF16)],
        compiler_params=_cparams(("parallel",)),
        name="outproj",
    )(merged, w_out, x, norm_w.reshape(1, d))


def _ffn_kernel(hn_ref, h_ref, wg_ref, wu_ref, wd_ref, nw_ref, y_ref, xn_ref, *, nf):
    f = pl.program_id(1)

    @pl.when(f == 0)
    def _():
        y_ref[...] = h_ref[...]

    hn = hn_ref[...]
    g = jnp.dot(hn, wg_ref[...], preferred_element_type=F32)
    u = jnp.dot(hn, wu_ref[...], preferred_element_type=F32)
    act = (_silu(g) * u).astype(BF16)
    y_ref[...] += jnp.dot(act, wd_ref[...], preferred_element_type=F32)

    @pl.when(f == nf - 1)
    def _():
        xn_ref[...] = (_rms_rows(y_ref[...]) * nw_ref[...]).astype(xn_ref.dtype)


def ffn(hn, h, w_gate, w_up, w_down, layer, next_norm_w, xn_dtype, tf=512, tm=TM):
    t, d = h.shape
    dff = w_gate.shape[2]
    nf = dff // tf
    row = pl.BlockSpec((tm, d), lambda i, f: (i, 0))
    w_in_spec = pl.BlockSpec((None, d, tf), lambda i, f: (layer, 0, f))
    return pl.pallas_call(
        functools.partial(_ffn_kernel, nf=nf),
        grid=(t // tm, nf),
        in_specs=[row, row, w_in_spec, w_in_spec, pl.BlockSpec((None, tf, d), lambda i, f: (layer, f, 0)),
                  pl.BlockSpec((1, d), lambda i, f: (0, 0))],
        out_specs=[row, row],
        out_shape=[jax.ShapeDtypeStruct((t, d), F32), jax.ShapeDtypeStruct((t, d), xn_dtype)],
        compiler_params=_cparams(("parallel", "arbitrary")),
        name="ffn",
    )(hn, h, w_gate, w_up, w_down, next_norm_w.reshape(1, d))


def _glu_kernel(h_ref, w_ref, b_ref, o_ref):
    h = h_ref[...]
    z = jnp.dot(h.astype(BF16), w_ref[...], preferred_element_type=F32) + b_ref[...]
    o_ref[...] = (h * _sigmoid(z)).astype(o_ref.dtype)


def s5_glu(h, w_glu, layer, b_glu, tm=TM):
    t, d = h.shape
    row = pl.BlockSpec((tm, d), lambda i: (i, 0))
    return pl.pallas_call(
        _glu_kernel,
        grid=(t // tm,),
        in_specs=[row, pl.BlockSpec((None, d, d), lambda i: (layer, 0, 0)), pl.BlockSpec((1, d), lambda i: (0, 0))],
        out_specs=row,
        out_shape=jax.ShapeDtypeStruct((t, d), BF16),
        compiler_params=_cparams(("parallel",)),
        name="s5_glu",
    )(h, w_glu, b_glu.reshape(1, d))


def _rotary(x, cos, sin):
    half = x.shape[-1] // 2
    x1, x2 = x[:, :half], x[:, half:]
    return jnp.concatenate([x1 * cos - x2 * sin, x1 * sin + x2 * cos], axis=-1)


def _ret_prompt_kernel(lg_ref, q_ref, k_ref, v_ref, g_ref, cos_ref, sin_ref, o_ref, s_out_ref, s_ref, *, nc, chunk):
    c = pl.program_id(1)

    @pl.when(c == 0)
    def _():
        s_ref[...] = jnp.zeros_like(s_ref)

    cos, sin = cos_ref[...], sin_ref[...]
    ii = lax.broadcasted_iota(jnp.int32, (chunk, chunk), 0)
    jj = lax.broadcasted_iota(jnp.int32, (chunk, chunk), 1)
    diff = (ii - jj).astype(F32)
    idx = lax.broadcasted_iota(jnp.int32, (chunk, 1), 0).astype(F32)
    heads = range(R_HEADS)
    hs = lambda h: slice(h * R_DK, (h + 1) * R_DK)
    lg = [lg_ref[h] for h in heads]
    q_l = [_rotary(q_ref[:, hs(h)], cos, sin) for h in heads]
    k_l = [_rotary(k_ref[:, hs(h)], cos, sin) * (R_DK ** -0.5) for h in heads]
    v_l = [v_ref[:, hs(h)].astype(BF16) for h in heads]
    s_l = [s_ref[h] for h in heads]
    scores = [_dg(q_l[h].astype(BF16), k_l[h].astype(BF16), _NT)
              * jnp.where(diff >= 0, jnp.exp(lg[h] * jnp.maximum(diff, 0.0)), 0.0) for h in heads]
    cross = [_dg((q_l[h] * jnp.exp(lg[h] * (idx + 1.0))).astype(BF16), s_l[h].astype(BF16), _NN) for h in heads]
    for h in heads:
        o = _dg(scores[h].astype(BF16), v_l[h], _NN) + cross[h]
        o_ref[:, hs(h)] = (_silu(g_ref[:, hs(h)]) * _rms_rows(o)).astype(o_ref.dtype)
    for h in heads:
        k_dec = (k_l[h] * jnp.exp(lg[h] * (chunk - 1.0 - idx))).astype(BF16)
        s_ref[h] = jnp.exp(lg[h] * jnp.full((1, 1), float(chunk), F32)) * s_l[h] + _dg(k_dec, v_l[h], _TN)

    @pl.when(c == nc - 1)
    def _():
        s_out_ref[0] = s_ref[...]


def retention_prompt(p_ret, cos, sin, log_gamma, batch, seq, chunk=R_CHUNK):
    nc = seq // chunk
    bw = BRANCH_WIDTH
    rows = lambda off: pl.BlockSpec((chunk, bw), functools.partial(lambda b, c, off: (b * nc + c, off), off=off))
    tab = pl.BlockSpec((chunk, R_DK // 2), lambda b, c: (c, 0))
    return pl.pallas_call(
        functools.partial(_ret_prompt_kernel, nc=nc, chunk=chunk),
        grid=(batch, nc),
        in_specs=[pl.BlockSpec(memory_space=pltpu.SMEM), rows(0), rows(1), rows(2), rows(3), tab, tab],
        out_specs=[pl.BlockSpec((chunk, bw), lambda b, c: (b * nc + c, 0)),
                   pl.BlockSpec((1, R_HEADS, R_DK, R_DK), lambda b, c: (b, 0, 0, 0))],
        out_shape=[jax.ShapeDtypeStruct((p_ret.shape[0], bw), BF16),
                   jax.ShapeDtypeStruct((batch, R_HEADS, R_DK, R_DK), F32)],
        scratch_shapes=[pltpu.VMEM((R_HEADS, R_DK, R_DK), F32)],
        compiler_params=_cparams(("parallel", "arbitrary")),
        name="ret_prompt",
    )(log_gamma, p_ret, p_ret, p_ret, p_ret, cos, sin)


def _ret_step_kernel(lg_ref, q_ref, k_ref, v_ref, g_ref, cos_ref, sin_ref, s_ref, *rest, bb):
    o_ref, s_out_ref = rest[-2:]
    h = pl.program_id(0)
    gamma = jnp.exp(lg_ref[h] * jnp.ones((1, 1), F32))
    cos, sin = cos_ref[...], sin_ref[...]
    q = _rotary(q_ref[...], cos, sin)
    k = _rotary(k_ref[...], cos, sin) * (R_DK ** -0.5)
    v = v_ref[...]
    eye = _eye(R_DK)
    q_cols = _dot_sel(eye, q, _NT)
    k_cols = _dot_sel(eye, k, _NT)
    for i in range(bb):
        s_new = gamma * s_ref[i, 0] + k_cols[:, i:i + 1] * v[i:i + 1, :]
        s_out_ref[i, 0] = s_new
        o = jnp.sum(q_cols[:, i:i + 1] * s_new, axis=0, keepdims=True)
        o_ref[i:i + 1, :] = (_silu(g_ref[i:i + 1, :]) * _rms_rows(o)).astype(o_ref.dtype)


def _alias_args(first_index, outputs):
    args, specs, aliases = [], [], {}
    for out_idx, a in outputs.items():
        if a is not None:
            aliases[first_index + len(args)] = out_idx
            args.append(a)
            specs.append(pl.BlockSpec(memory_space=pl.ANY))
    return args, specs, aliases


def retention_step(p_ret, row0, cos, sin, log_gamma, states, layer, o_all, s_all, bb=16):
    nb = states.shape[1]
    r0 = row0 // bb
    rows = lambda off: pl.BlockSpec((bb, R_DK), functools.partial(lambda h, j, off: (r0 + j, off + h), off=off))
    tab = pl.BlockSpec((1, R_DK // 2), lambda h, j: (0, 0))
    st = pl.BlockSpec((None, bb, 1, R_DK, R_DK), lambda h, j: (layer, j, h, 0, 0))
    extra, extra_specs, aliases = _alias_args(8, {0: o_all, 1: s_all})
    return pl.pallas_call(
        functools.partial(_ret_step_kernel, bb=bb),
        grid=(R_HEADS, nb // bb),
        in_specs=[pl.BlockSpec(memory_space=pltpu.SMEM), rows(0), rows(R_HEADS), rows(2 * R_HEADS),
                  rows(3 * R_HEADS), tab, tab, st, *extra_specs],
        out_specs=[pl.BlockSpec((bb, R_DK), lambda h, j: (r0 + j, h)), st],
        out_shape=[jax.ShapeDtypeStruct(o_all.shape, BF16), jax.ShapeDtypeStruct(states.shape, F32)],
        input_output_aliases=aliases,
        compiler_params=_cparams(("parallel", "parallel")),
        name="ret_step",
    )(log_gamma, p_ret, p_ret, p_ret, p_ret, cos, sin, states, *extra)


def _cmul(ar, ai, br, bi):
    return ar * br - ai * bi, ar * bi + ai * br


def s5_discretize(lam_re, lam_im, log_dt, b_re, b_im):
    dt = jnp.exp(log_dt)[:, None]
    mag = jnp.exp(lam_re * dt)
    lbr, lbi = mag * jnp.cos(lam_im * dt), mag * jnp.sin(lam_im * dt)
    den = lam_re * lam_re + lam_im * lam_im
    nr, ni = lbr - 1.0, lbi
    cr, ci = (nr * lam_re + ni * lam_im) / den, (ni * lam_re - nr * lam_im) / den
    bbr, bbi = _cmul(cr[..., None], ci[..., None], b_re, b_im)
    return dt, lbr, lbi, bbr, bbi


def _lam_power(lam_re, lam_im, dt, n):
    n = jnp.asarray(n, F32).reshape((-1,) + (1,) * lam_re.ndim)
    mag = jnp.exp(lam_re * dt * n)
    return mag * jnp.cos(lam_im * dt * n), mag * jnp.sin(lam_im * dt * n)


def s5_prompt_params(lam_re, lam_im, log_dt, b_re, b_im, c_re, c_im, d_skip):
    g, p, q = S5_GROUPS, S5_STATE, S5_CHUNK
    dt, lbr, lbi, bbr, bbi = s5_discretize(lam_re, lam_im, log_dt, b_re, b_im)
    pr, pi = _lam_power(lam_re, lam_im, dt, jnp.arange(q + 1))
    lb_r, lb_i = _cmul(pr[:q, :, :, None], pi[:q, :, :, None], bbr[None], bbi[None])
    hp = lax.Precision.HIGHEST
    kern = (jnp.einsum('gkp,tgpc->tgkc', c_re, lb_r, precision=hp)
            - jnp.einsum('gkp,tgpc->tgkc', c_im, lb_i, precision=hp))
    zero = jnp.zeros_like(kern)
    m = jnp.stack([jnp.concatenate([zero[:s], kern[:q - s]], axis=0) for s in range(q)], axis=0)
    m = m.transpose(2, 0, 4, 1, 3).reshape(g, q * S5_GROUP, q * S5_GROUP)
    er, ei = _cmul(pr[:q][::-1][:, :, :, None], pi[:q][::-1][:, :, :, None], bbr[None], bbi[None])
    w_re = er.transpose(1, 0, 3, 2).reshape(g, q * S5_GROUP, p)
    w_im = ei.transpose(1, 0, 3, 2).reshape(g, q * S5_GROUP, p)
    vr, vi = _cmul(c_re[None], c_im[None], pr[1:, :, None, :], pi[1:, :, None, :])
    v_re = vr.transpose(1, 3, 0, 2).reshape(g, p, q * S5_GROUP)
    v_im = (-vi).transpose(1, 3, 0, 2).reshape(g, p, q * S5_GROUP)

    def pair_cols(a):
        z = jnp.zeros_like(a)
        even = (jnp.arange(g) % 2 == 0)[:, None, None]
        return jnp.concatenate([jnp.where(even, a, z), jnp.where(even, z, a)], axis=-1)

    def pair_rows(a):
        z = jnp.zeros_like(a)
        even = (jnp.arange(g) % 2 == 0)[:, None, None]
        return jnp.concatenate([jnp.where(even, a, z), jnp.where(even, z, a)], axis=1)

    sr, si = _lam_power(lam_re, lam_im, dt, q * (2.0 ** jnp.arange(8)))
    lam_sc_re = sr.reshape(8, g // 2, 2 * p).transpose(1, 0, 2)
    lam_sc_im = si.reshape(8, g // 2, 2 * p).transpose(1, 0, 2)
    d_vec = jnp.tile(d_skip.reshape(g, 1, S5_GROUP), (1, q, 1)).reshape(g, 1, q * S5_GROUP)
    bf = lambda a: a.astype(BF16)
    return (bf(m), bf(pair_cols(w_re)), bf(pair_cols(w_im)), bf(pair_rows(v_re)), bf(pair_rows(v_im)),
            lam_sc_re, lam_sc_im, d_vec)


S5_TILE_GROUPS = 128 // S5_GROUP


def _s5_prompt_kernel(u_ref, m_ref, wre_ref, wim_ref, vre_ref, vim_ref, lre_ref, lim_ref, d_ref,
                      h_ref, fre_ref, fim_ref, ut_ref, ht_ref, tile_ref, *, nchb, nb):
    rows = nb * nchb
    q, gw, ng = S5_CHUNK, S5_GROUP, S5_TILE_GROUPS
    for s in range(q):
        xs = u_ref[pl.ds(s, rows, stride=q), :]
        for gg in range(ng):
            ut_ref[gg, :, s * gw:(s + 1) * gw] = xs[:, gg * gw:(gg + 1) * gw]

    pos = jnp.bitwise_and(lax.broadcasted_iota(jnp.int32, (rows, 2 * S5_STATE), 0), nchb - 1)
    first = pos == 0
    pairs = range(ng // 2)
    ub_l = [(ut_ref[2 * pr].astype(BF16), ut_ref[2 * pr + 1].astype(BF16)) for pr in pairs]
    e_re = [_dg(ub_l[pr][0], wre_ref[2 * pr], _NN) + _dg(ub_l[pr][1], wre_ref[2 * pr + 1], _NN) for pr in pairs]
    e_im = [_dg(ub_l[pr][0], wim_ref[2 * pr], _NN) + _dg(ub_l[pr][1], wim_ref[2 * pr + 1], _NN) for pr in pairs]
    d, k = 1, 0
    while d < nchb:
        keep = pos >= d
        for pr in pairs:
            lr, li = lre_ref[pr, k:k + 1, :], lim_ref[pr, k:k + 1, :]
            sr, si = pltpu.roll(e_re[pr], d, 0), pltpu.roll(e_im[pr], d, 0)
            e_re[pr], e_im[pr] = (e_re[pr] + jnp.where(keep, lr * sr - li * si, 0.0),
                                  e_im[pr] + jnp.where(keep, lr * si + li * sr, 0.0))
        d, k = d * 2, k + 1
    for pr in pairs:
        for b in range(nb):
            r = (b + 1) * nchb - 1
            fre_ref[pr, b:b + 1, :] = e_re[pr][r:r + 1, :]
            fim_ref[pr, b:b + 1, :] = e_im[pr][r:r + 1, :]
    for pr in pairs:
        xc_re = jnp.where(first, 0.0, pltpu.roll(e_re[pr], 1, 0)).astype(BF16)
        xc_im = jnp.where(first, 0.0, pltpu.roll(e_im[pr], 1, 0)).astype(BF16)
        for gi in range(2):
            gg = 2 * pr + gi
            y = (_dg(ub_l[pr][gi], m_ref[gg], _NN) + _dg(xc_re, vre_ref[gg], _NN) + _dg(xc_im, vim_ref[gg], _NN)
                 ) + ut_ref[gg] * d_ref[gg]
            ht_ref[gg] = _gelu_tanh(y)
    for s in range(q):
        for gg in range(ng):
            tile_ref[:, gg * gw:(gg + 1) * gw] = ht_ref[gg, :, s * gw:(s + 1) * gw]
        h_ref[pl.ds(s, rows, stride=q), :] = tile_ref[...]


def s5_prompt(p_s5, params, layer, batch, seq):
    m, w_re, w_im, v_re, v_im, l_re, l_im, d_vec = params
    ng, qc, p2 = S5_TILE_GROUPS, S5_CHUNK * S5_GROUP, 2 * S5_STATE
    nchb = seq // S5_CHUNK
    rows = batch * nchb
    tp = batch * seq
    nt = BRANCH_WIDTH // 128
    grp = lambda n, *tail: pl.BlockSpec((None, n) + tail, lambda j: (layer, j) + (0,) * len(tail))
    return pl.pallas_call(
        functools.partial(_s5_prompt_kernel, nchb=nchb, nb=batch),
        grid=(nt,),
        in_specs=[pl.BlockSpec((tp, 128), lambda j: (0, j)), grp(ng, qc, qc), grp(ng, qc, p2), grp(ng, qc, p2),
                  grp(ng, p2, qc), grp(ng, p2, qc), grp(ng // 2, 8, p2), grp(ng // 2, 8, p2), grp(ng, 1, qc)],
        out_specs=[pl.BlockSpec((tp, 128), lambda j: (0, j)), pl.BlockSpec((ng // 2, 8, p2), lambda j: (j, 0, 0)),
                   pl.BlockSpec((ng // 2, 8, p2), lambda j: (j, 0, 0))],
        out_shape=[jax.ShapeDtypeStruct(p_s5.shape, F32),
                   jax.ShapeDtypeStruct((S5_GROUPS // 2, 8, p2), F32),
                   jax.ShapeDtypeStruct((S5_GROUPS // 2, 8, p2), F32)],
        scratch_shapes=[pltpu.VMEM((ng, rows, qc), F32), pltpu.VMEM((ng, rows, qc), F32),
                        pltpu.VMEM((rows, 128), F32)],
        compiler_params=_cparams(("parallel",)),
        name="s5_prompt",
    )(p_s5, m, w_re, w_im, v_re, v_im, l_re, l_im, d_vec)


def s5_step_params(lam_re, lam_im, log_dt, b_re, b_im, c_re, c_im, d_skip):
    g, p = S5_GROUPS, S5_STATE
    _, lbr, lbi, bbr, bbi = s5_discretize(lam_re, lam_im, log_dt, b_re, b_im)
    eye8 = jnp.eye(8, dtype=F32)

    def bd_in(bb):
        x = bb.reshape(8, 8, p, S5_GROUP)
        return jnp.einsum('jgpc,gh->jgchp', x, eye8).reshape(8, 8 * S5_GROUP, 8 * p)

    def bd_out(cc):
        x = cc.reshape(8, 8, S5_GROUP, p)
        return jnp.einsum('jgkp,gh->jgphk', x, eye8).reshape(8, 8 * p, 8 * S5_GROUP)

    return (bd_in(bbr), bd_in(bbi), bd_out(c_re), bd_out(-c_im), lbr.reshape(1, g * p), lbi.reshape(1, g * p),
            d_skip.reshape(1, g * S5_GROUP))


def _s5_step_kernel(u_ref, xr_ref, xi_ref, bre_ref, bim_ref, cre_ref, cim_ref, lr_ref, li_ref, d_ref, *rest):
    h_ref, nr_ref, ni_ref = rest[-3:]
    u = u_ref[...]
    lr, li = lr_ref[...], li_ref[...]
    xr, xi = xr_ref[...], xi_ref[...]
    nr = lr * xr - li * xi + _dot3(u, bre_ref[0])
    ni = lr * xi + li * xr + _dot3(u, bim_ref[0])
    nr_ref[...] = nr
    ni_ref[...] = ni
    y = _dot3(nr, cre_ref[0]) + _dot3(ni, cim_ref[0]) + u * d_ref[...]
    h_ref[...] = _gelu_tanh(y)


def s5_step(p_s5, row0, x_re, x_im, params, layer, h_all):
    b_re, b_im, c_re, c_im, l_re, l_im, d_vec = params
    nb, gp = x_re.shape
    lt = 8 * S5_GROUP
    st = 8 * S5_STATE
    r0 = row0 // nb
    u_spec = pl.BlockSpec((nb, lt), lambda j: (r0, j))
    x_spec = pl.BlockSpec((nb, st), lambda j: (0, j))
    mat = lambda a, b: pl.BlockSpec((None, 1, a, b), lambda j: (layer, j, 0, 0))
    vec = lambda n: pl.BlockSpec((None, 1, n), lambda j: (layer, 0, j))
    extra, extra_specs, aliases = _alias_args(10, {0: h_all})
    return pl.pallas_call(
        _s5_step_kernel,
        grid=(gp // st,),
        in_specs=[u_spec, x_spec, x_spec, mat(lt, st), mat(lt, st), mat(st, lt), mat(st, lt),
                  vec(st), vec(st), vec(lt), *extra_specs],
        out_specs=[u_spec, x_spec, x_spec],
        out_shape=[jax.ShapeDtypeStruct(h_all.shape, F32), jax.ShapeDtypeStruct((nb, gp), F32),
                   jax.ShapeDtypeStruct((nb, gp), F32)],
        input_output_aliases=aliases,
        compiler_params=_cparams(("parallel",)),
        name="s5_step",
    )(p_s5, x_re, x_im, b_re, b_im, c_re, c_im, l_re, l_im, d_vec, *extra)


def _l2norm_rows(x):
    return x * lax.rsqrt(jnp.sum(x * x, axis=-1, keepdims=True) + NORM_EPS)


def _gdn_prompt_kernel(qkv_ref, z_ref, ab_ref, cw_ref, alog_ref, dtb_ref, nw_ref, o_ref, s_out_ref,
                       xbuf_ref, s_ref, *, nc, chunk):
    c = pl.program_id(1)
    bw = BRANCH_WIDTH
    halo = 8

    @pl.when(c == 0)
    def _():
        xbuf_ref[0:halo, :] = jnp.zeros((halo, 3 * bw), F32)
        s_ref[...] = jnp.zeros_like(s_ref)

    x = qkv_ref[...]
    xbuf_ref[halo:halo + chunk, :] = x
    conv = x * cw_ref[G_CONV - 1:G_CONV, :]
    for i in range(1, G_CONV):
        conv = conv + xbuf_ref[halo - i:halo - i + chunk, :] * cw_ref[G_CONV - 1 - i:G_CONV - i, :]
    xbuf_ref[0:halo, :] = x[chunk - halo:chunk, :]
    qkv = _silu(conv)

    ab = ab_ref[...]
    g_all = -jnp.exp(alog_ref[...]) * _softplus(ab + dtb_ref[...])
    beta_all = _sigmoid(ab)
    ri = lax.broadcasted_iota(jnp.int32, (chunk, chunk), 0)
    ci = lax.broadcasted_iota(jnp.int32, (chunk, chunk), 1)
    causal = ri >= ci
    strict = ri > ci
    ltri = jnp.where(causal, 1.0, 0.0).astype(BF16)
    gcum = _dot_sel(ltri, g_all)
    gcum_t = _dot_sel(_eye(128), gcum, _NT)
    z = z_ref[...]
    nw = nw_ref[...]

    heads = range(G_HEADS)
    q_l, gc_l, eg_l, qkd_l, kdec_l, m_l, sol_l = [], [], [], [], [], [], []
    for h in heads:
        q = _l2norm_rows(qkv[:, h * G_DK:(h + 1) * G_DK]) * (G_DK ** -0.5)
        k = _l2norm_rows(qkv[:, bw + h * G_DK:bw + (h + 1) * G_DK])
        v = qkv[:, 2 * bw + h * G_DK:2 * bw + (h + 1) * G_DK]
        gc = gcum[:, h:h + 1]
        gr = gcum_t[h:h + 1, :]
        beta = beta_all[:, G_HEADS + h:G_HEADS + h + 1]
        decay = jnp.where(causal, jnp.exp(jnp.where(causal, gc - gr, 0.0)), 0.0)
        kb = k.astype(BF16)
        qk_kk = _dg(jnp.concatenate([kb, q.astype(BF16)], axis=0), kb, _NT)
        eg = jnp.exp(gc)
        q_l.append(q * eg)
        gc_l.append(gc)
        qkd_l.append((qk_kk[chunk:] * decay).astype(BF16))
        kdec_l.append((k * jnp.exp(gc[chunk - 1:chunk, :] - gc)).astype(BF16))
        m_l.append(jnp.where(strict, -(beta * qk_kk[:chunk] * decay), 0.0))
        sol_l.append(jnp.concatenate([beta * v, (beta * eg) * k], axis=-1))
    span = 1
    while span < chunk:
        for h in heads:
            mb = m_l[h].astype(BF16)
            if 2 * span < chunk:
                r = _dot2(mb, jnp.concatenate([sol_l[h], m_l[h]], axis=-1))
                sol_l[h], m_l[h] = sol_l[h] + r[:, :2 * G_DK], r[:, 2 * G_DK:]
            else:
                sol_l[h] = sol_l[h] + _dot2(mb, sol_l[h])
        span *= 2
    s_l = [s_ref[h] for h in heads]
    ks_qs_l = [_dg(jnp.concatenate([sol_l[h][:, G_DK:], q_l[h]], axis=0).astype(BF16), s_l[h].astype(BF16), _NN)
               for h in heads]
    wb_l = [(sol_l[h][:, :G_DK] - ks_qs_l[h][:chunk]).astype(BF16) for h in heads]
    for h in heads:
        o = ks_qs_l[h][chunk:] + _dg(qkd_l[h], wb_l[h], _NN)
        zh = z[:, h * G_DK:(h + 1) * G_DK]
        o_ref[:, h * G_DK:(h + 1) * G_DK] = (_rms_rows(o) * nw * _silu(zh)).astype(o_ref.dtype)
    for h in heads:
        g_last = gc_l[h][chunk - 1:chunk, :]
        s_ref[h] = jnp.exp(g_last) * s_l[h] + _dg(kdec_l[h], wb_l[h], _TN)

    @pl.when(c == nc - 1)
    def _():
        s_out_ref[0] = s_ref[...]


def gdn_prompt(p_gdn, conv_w, a_log, dt_bias, norm_w, batch, seq, chunk=G_CHUNK):
    nc = seq // chunk
    bw = BRANCH_WIDTH
    pad = lambda x: jnp.zeros((1, 128), F32).at[0, :G_HEADS].set(x)
    full = lambda shape: pl.BlockSpec(shape, lambda b, c: (0,) * len(shape))
    return pl.pallas_call(
        functools.partial(_gdn_prompt_kernel, nc=nc, chunk=chunk),
        grid=(batch, nc),
        in_specs=[pl.BlockSpec((chunk, 3 * bw), lambda b, c: (b * nc + c, 0)),
                  pl.BlockSpec((chunk, bw), lambda b, c: (b * nc + c, 3)),
                  pl.BlockSpec((chunk, 128), lambda b, c: (b * nc + c, 4 * bw // 128)),
                  full((G_CONV, 3 * bw)), full((1, 128)), full((1, 128)), full((1, G_DK))],
        out_specs=[pl.BlockSpec((chunk, bw), lambda b, c: (b * nc + c, 0)),
                   pl.BlockSpec((1, G_HEADS, G_DK, G_DK), lambda b, c: (b, 0, 0, 0))],
        out_shape=[jax.ShapeDtypeStruct((p_gdn.shape[0], bw), BF16),
                   jax.ShapeDtypeStruct((batch, G_HEADS, G_DK, G_DK), F32)],
        scratch_shapes=[pltpu.VMEM((8 + chunk, 3 * bw), F32), pltpu.VMEM((G_HEADS, G_DK, G_DK), F32)],
        compiler_params=_cparams(("parallel", "arbitrary")),
        name="gdn_prompt",
    )(p_gdn, p_gdn, p_gdn, conv_w, pad(a_log), pad(dt_bias), norm_w.reshape(1, G_DK))


def _gdn_step_kernel(qkv_ref, z_ref, ab_ref, cbuf_ref, cw_ref, alog_ref, dtb_ref, nw_ref, s_ref, *rest, bb):
    o_ref, cbuf_out_ref, s_out_ref = rest[-3:]
    bw = BRANCH_WIDTH
    x = qkv_ref[...]
    conv = x * cw_ref[G_CONV - 1:G_CONV, :]
    for i in range(G_CONV - 1):
        conv = conv + cbuf_ref[i] * cw_ref[i:i + 1, :]
    for i in range(G_CONV - 2):
        cbuf_out_ref[i] = cbuf_ref[i + 1]
    cbuf_out_ref[G_CONV - 2] = x
    qkv = _silu(conv)
    ab = ab_ref[...]
    g_all = -jnp.exp(alog_ref[...]) * _softplus(ab + dtb_ref[...])
    beta_all = _sigmoid(ab)
    eye = _eye(G_DK)
    z = z_ref[...]
    nw = nw_ref[...]
    for h in range(G_HEADS):
        q = _l2norm_rows(qkv[:, h * G_DK:(h + 1) * G_DK]) * (G_DK ** -0.5)
        k = _l2norm_rows(qkv[:, bw + h * G_DK:bw + (h + 1) * G_DK])
        v = qkv[:, 2 * bw + h * G_DK:2 * bw + (h + 1) * G_DK]
        eg = jnp.exp(g_all[:, h:h + 1])
        beta = beta_all[:, G_HEADS + h:G_HEADS + h + 1]
        q_cols = _dot_sel(eye, q, _NT)
        k_cols = _dot_sel(eye, k, _NT)
        rng = range(bb)
        s_l = [s_ref[i, h] for i in rng]
        ks_l = [jnp.sum(k_cols[:, i:i + 1] * s_l[i], axis=0, keepdims=True) for i in rng]
        w_l = [beta[i:i + 1, :] * (v[i:i + 1, :] - eg[i:i + 1, :] * ks_l[i]) for i in rng]
        sn_l = [eg[i:i + 1, :] * s_l[i] + k_cols[:, i:i + 1] * w_l[i] for i in rng]
        for i in rng:
            s_out_ref[i, h] = sn_l[i]
        o = jnp.concatenate([jnp.sum(q_cols[:, i:i + 1] * sn_l[i], axis=0, keepdims=True) for i in rng], axis=0)
        zh = z[:, h * G_DK:(h + 1) * G_DK]
        o_ref[:, h * G_DK:(h + 1) * G_DK] = (_rms_rows(o) * nw * _silu(zh)).astype(o_ref.dtype)


def gdn_step(p_gdn, row0, conv_buf_t, states, layer, conv_w, a_log, dt_bias, norm_w, o_all, s_all, bb=8):
    nb = states.shape[1]
    bw = BRANCH_WIDTH
    r0 = row0 // bb
    pad = lambda x: jnp.zeros((1, 128), F32).at[0, :G_HEADS].set(x)
    full = lambda shape: pl.BlockSpec(shape, lambda j: (0,) * len(shape))
    cb = pl.BlockSpec((G_CONV - 1, bb, 3 * bw), lambda j: (0, j, 0))
    st = pl.BlockSpec((None, bb, G_HEADS, G_DK, G_DK), lambda j: (layer, j, 0, 0, 0))
    extra, extra_specs, aliases = _alias_args(9, {0: o_all, 2: s_all})
    return pl.pallas_call(
        functools.partial(_gdn_step_kernel, bb=bb),
        grid=(nb // bb,),
        in_specs=[pl.BlockSpec((bb, 3 * bw), lambda j: (r0 + j, 0)),
                  pl.BlockSpec((bb, bw), lambda j: (r0 + j, 3)),
                  pl.BlockSpec((bb, 128), lambda j: (r0 + j, 4 * bw // 128)),
                  cb, full((G_CONV, 3 * bw)), full((1, 128)), full((1, 128)), full((1, G_DK)), st, *extra_specs],
        out_specs=[pl.BlockSpec((bb, bw), lambda j: (r0 + j, 0)), cb, st],
        out_shape=[jax.ShapeDtypeStruct(o_all.shape, BF16), jax.ShapeDtypeStruct(conv_buf_t.shape, F32),
                   jax.ShapeDtypeStruct(states.shape, F32)],
        input_output_aliases=aliases,
        compiler_params=_cparams(("parallel",)),
        name="gdn_step",
    )(p_gdn, p_gdn, p_gdn, conv_buf_t, conv_w, pad(a_log), pad(dt_bias), norm_w.reshape(1, G_DK), states, *extra)


def _rope_tables(pos):
    half = R_DK // 2
    inv = ROPE_BASE ** (-jnp.arange(half, dtype=F32) / half)
    ang = pos.astype(F32)[:, None] * inv[None, :]
    return jnp.cos(ang), jnp.sin(ang)


def kernel(x_prompt, x_sample, state_ret, state_s5_re, state_s5_im, state_gdn, state_gdn_conv, norm_mix_w, w_in, s5_lambda_re, s5_lambda_im, s5_log_dt, s5_b_re, s5_b_im, s5_c_re, s5_c_im, s5_d, s5_w_glu, s5_b_glu, gdn_conv_w, gdn_a_log, gdn_dt_bias, gdn_norm_w, w_branch, w_out, norm_ffn_w, w_gate_ffn, w_up_ffn, w_down_ffn, norm_final_w):
    bp, lp, d = x_prompt.shape
    nb = x_sample.shape[0]
    depth = w_in.shape[0]
    tp = bp * lp
    bw = BRANCH_WIDTH
    x = jnp.concatenate([x_prompt.reshape(tp, d), x_sample.reshape(nb, d)], axis=0)

    cos_p, sin_p = _rope_tables(jnp.arange(lp, dtype=jnp.int32))
    cos_s, sin_s = _rope_tables(PAST_LEN + jnp.arange(1, dtype=jnp.int32))
    log_gamma = jnp.log1p(-jnp.exp2(-5.0 - jnp.arange(R_HEADS, dtype=F32)))

    o_su, o_gqkv, o_ga, o_gz, o_gates = 4 * bw, 5 * bw, 8 * bw, 8 * bw + 2 * G_HEADS, 9 * bw + 2 * G_HEADS
    gdn_w = 4 * bw + 128
    w_in_b = jnp.concatenate([w_in[..., :o_gqkv], w_in[..., o_gates:], w_in[..., o_gqkv:o_ga], w_in[..., o_gz:o_gates],
                              w_in[..., o_ga:o_gz], jnp.zeros((depth, d, 128 - 2 * G_HEADS), F32)],
                             axis=-1).astype(BF16)
    col_gate, col_gdn = o_gqkv, o_gqkv + N_BRANCH * d
    w_branch_b, w_out_b, w_glu_b = w_branch.astype(BF16), w_out.astype(BF16), s5_w_glu.astype(BF16)
    w_gate_b, w_up_b, w_down_b = w_gate_ffn.astype(BF16), w_up_ffn.astype(BF16), w_down_ffn.astype(BF16)
    s5_raw = (s5_lambda_re, s5_lambda_im, s5_log_dt, s5_b_re, s5_b_im, s5_c_re, s5_c_im, s5_d)
    s5_pp = jax.vmap(s5_prompt_params)(*s5_raw)
    s5_sp = jax.vmap(s5_step_params)(*s5_raw)

    new_p = [[], [], [], [], []]
    new_s = [[], [], []]
    ret_s_all = gdn_s_all = None
    xn = rmsnorm_rows(x, norm_mix_w[0], BF16)
    y_final = None
    for l in range(depth):
        p_ret = matmul(xn, w_in_b, l, 0, o_su, tn=1024, name="proj_ret")
        p_s5 = matmul(xn, w_in_b, l, o_su, bw, tn=1024, name="proj_s5")
        p_gdn = matmul(xn, w_in_b, l, col_gdn, gdn_w, tn=gdn_w // 3, name="proj_gdn")

        r_all, ret_p = retention_prompt(p_ret, cos_p, sin_p, log_gamma, bp, lp)
        r_all, ret_s_all = retention_step(p_ret, tp, cos_s, sin_s, log_gamma, state_ret, l, r_all, ret_s_all)

        h_all, f_re, f_im = s5_prompt(p_s5, s5_pp, l, bp, lp)
        fin = lambda f: f[:, :bp, :].reshape(S5_GROUPS // 2, bp, 2, S5_STATE).transpose(1, 0, 2, 3).reshape(
            bp, S5_GROUPS, S5_STATE)
        h_all, s5re_s, s5im_s = s5_step(p_s5, tp, state_s5_re[l].reshape(nb, -1), state_s5_im[l].reshape(nb, -1),
                                        s5_sp, l, h_all)
        s_all = s5_glu(h_all, w_glu_b, l, s5_b_glu[l])

        g_all, gdn_p = gdn_prompt(p_gdn, gdn_conv_w[l], gdn_a_log[l], gdn_dt_bias[l], gdn_norm_w[l], bp, lp)
        g_all, conv_s_t, gdn_s_all = gdn_step(p_gdn, tp, state_gdn_conv[l].transpose(1, 0, 2), state_gdn, l,
                                              gdn_conv_w[l], gdn_a_log[l], gdn_dt_bias[l], gdn_norm_w[l],
                                              g_all, gdn_s_all)
        conv_p = jnp.stack([p_gdn[(b + 1) * lp - (G_CONV - 1):(b + 1) * lp, :3 * bw] for b in range(bp)], axis=0)

        merged = merge_branches(xn, r_all, s_all, g_all, w_in_b, col_gate, w_branch_b, l)
        h, hn = out_proj(merged, w_out_b, l, x, norm_ffn_w[l])
        last = l == depth - 1
        next_w = norm_final_w if last else norm_mix_w[l + 1]
        x, xn = ffn(hn, h, w_gate_b, w_up_b, w_down_b, l, next_w, F32 if last else BF16)
        if last:
            y_final = xn

        for lst, val in zip(new_p, (ret_p, fin(f_re), fin(f_im), gdn_p, conv_p)):
            lst.append(val)
        for lst, val in zip(new_s, (s5re_s.reshape(nb, S5_GROUPS, S5_STATE),
                                    s5im_s.reshape(nb, S5_GROUPS, S5_STATE), conv_s_t.transpose(1, 0, 2))):
            lst.append(val)

    y_prompt = y_final[:tp].reshape(bp, lp, d)
    y_sample = y_final[tp:].reshape(nb, 1, d)
    outs_p = [jnp.stack(a, 0) for a in new_p]
    s5re_s, s5im_s, conv_s = [jnp.stack(a, 0) for a in new_s]
    return (y_prompt, y_sample, *outs_p, ret_s_all, s5re_s, s5im_s, gdn_s_all, conv_s)
```

```python
import functools
import math

import jax
import jax.numpy as jnp
from jax import lax
from jax.experimental import pallas as pl
from jax.experimental.pallas import tpu as pltpu

F32 = jnp.float32
BF16 = jnp.bfloat16

D_MODEL = 2048
BRANCH_WIDTH = 1024
N_BRANCH = 3
R_HEADS = 4
R_DK = 256
R_CHUNK = 128
ROPE_BASE = 10000.0
S5_GROUP = 16
S5_GROUPS = 64
S5_STATE = 64
S5_CHUNK = 16
G_HEADS = 8
G_DK = 128
G_CONV = 4
G_CHUNK = 64
NORM_EPS = 1e-6
PAST_LEN = 16384

VMEM_LIMIT_BYTES = 56 * 1024 * 1024
TM = 640


def _cparams(sem):
    return pltpu.CompilerParams(dimension_semantics=sem, vmem_limit_bytes=VMEM_LIMIT_BYTES)


_NN = (((1,), (0,)), ((), ()))
_NT = (((1,), (1,)), ((), ()))
_TN = (((0,), (0,)), ((), ()))


def _dg(a, b, dims):
    return lax.dot_general(a, b, dims, preferred_element_type=F32)


def _split2(a):
    hi = a.astype(BF16)
    lo = (a - hi.astype(F32)).astype(BF16)
    return hi, lo


def _dot3(a, b, dims=_NN):
    ah, al = _split2(a)
    bh, bl = _split2(b)
    return _dg(ah, bh, dims) + (_dg(ah, bl, dims) + _dg(al, bh, dims))


def _dot2(a_bf16, b, dims=_NN):
    bh, bl = _split2(b)
    return _dg(a_bf16, bh, dims) + _dg(a_bf16, bl, dims)


def _dot_sel(sel_bf16, x, dims=_NN):
    h1 = x.astype(BF16)
    r1 = x - h1.astype(F32)
    h2 = r1.astype(BF16)
    h3 = (r1 - h2.astype(F32)).astype(BF16)
    return _dg(sel_bf16, h1, dims) + (_dg(sel_bf16, h2, dims) + _dg(sel_bf16, h3, dims))


def _sel_dot(x, sel_bf16, dims=_NN):
    h1 = x.astype(BF16)
    r1 = x - h1.astype(F32)
    h2 = r1.astype(BF16)
    h3 = (r1 - h2.astype(F32)).astype(BF16)
    return _dg(h1, sel_bf16, dims) + (_dg(h2, sel_bf16, dims) + _dg(h3, sel_bf16, dims))


def _eye(n, dtype=BF16):
    r = lax.broadcasted_iota(jnp.int32, (n, n), 0)
    c = lax.broadcasted_iota(jnp.int32, (n, n), 1)
    return jnp.where(r == c, 1.0, 0.0).astype(dtype)


def _sigmoid(x):
    return 1.0 / (1.0 + jnp.exp(-x))


def _silu(x):
    return x * _sigmoid(x)


def _gelu_tanh(x):
    return 0.5 * x * (1.0 + jnp.tanh(math.sqrt(2.0 / math.pi) * (x + 0.044715 * (x * x * x))))


def _softplus(x):
    return jnp.maximum(x, 0.0) + jnp.log1p(jnp.exp(-jnp.abs(x)))


def _rms_rows(x, eps=NORM_EPS):
    return x * lax.rsqrt(jnp.mean(x * x, axis=-1, keepdims=True) + eps)


def _norm_kernel(x_ref, w_ref, o_ref):
    o_ref[...] = (_rms_rows(x_ref[...]) * w_ref[...]).astype(o_ref.dtype)


def rmsnorm_rows(x, w, out_dtype, tm=TM):
    t, d = x.shape
    return pl.pallas_call(
        _norm_kernel,
        grid=(t // tm,),
        in_specs=[pl.BlockSpec((tm, d), lambda i: (i, 0)), pl.BlockSpec((1, d), lambda i: (0, 0))],
        out_specs=pl.BlockSpec((tm, d), lambda i: (i, 0)),
        out_shape=jax.ShapeDtypeStruct((t, d), out_dtype),
        compiler_params=_cparams(("parallel",)),
        name="rmsnorm",
    )(x, w.reshape(1, d))


def _mm_kernel(a_ref, w_ref, o_ref):
    o_ref[...] = jnp.dot(a_ref[...], w_ref[...], preferred_element_type=F32)


def matmul(a, w, layer, col0, n, tn, tm=TM, name="proj"):
    t, k = a.shape
    c0 = col0 // tn
    return pl.pallas_call(
        _mm_kernel,
        grid=(t // tm, n // tn),
        in_specs=[pl.BlockSpec((tm, k), lambda i, j: (i, 0)),
                  pl.BlockSpec((None, k, tn), lambda i, j: (layer, 0, c0 + j))],
        out_specs=pl.BlockSpec((tm, tn), lambda i, j: (i, j)),
        out_shape=jax.ShapeDtypeStruct((t, n), F32),
        compiler_params=_cparams(("parallel", "arbitrary")),
        name=name,
    )(a, w)


LANE = 128


def _shifted_cols(w_ref, wn_ref, shift):
    if shift == 0:
        return w_ref[...]
    return jnp.concatenate([w_ref[:, shift:], wn_ref[:, :shift]], axis=1)


def _shifted_specs(d, tn, layer, col0, nidx):
    base, shift = (col0 // tn) * tn, col0 % tn
    assert base % tn == 0 and shift < LANE and tn % LANE == 0
    main = pl.BlockSpec((None, d, tn), lambda i, j: (layer, 0, base // tn + nidx(i, j)))
    nxt = pl.BlockSpec((None, d, LANE), lambda i, j: (layer, 0, (base + (nidx(i, j) + 1) * tn) // LANE))
    return main, nxt, shift


def _mm_shift_kernel(a_ref, w_ref, wn_ref, o_ref, *, shift):
    o_ref[...] = jnp.dot(a_ref[...], _shifted_cols(w_ref, wn_ref, shift), preferred_element_type=F32)


def matmul_shifted(a, w, layer, col0, n, tn, tm=TM, name="proj"):
    t, k = a.shape
    main, nxt, shift = _shifted_specs(k, tn, layer, col0, lambda i, j: j)
    return pl.pallas_call(
        functools.partial(_mm_shift_kernel, shift=shift),
        grid=(t // tm, n // tn),
        in_specs=[pl.BlockSpec((tm, k), lambda i, j: (i, 0)), main, nxt],
        out_specs=pl.BlockSpec((tm, tn), lambda i, j: (i, j)),
        out_shape=jax.ShapeDtypeStruct((t, n), F32),
        compiler_params=_cparams(("parallel", "arbitrary")),
        name=name,
    )(a, w, w)


def _merge_kernel(xn_ref, br_r, br_s, br_g, wg0, wn0, wg1, wn1, wg2, wn2, wb_ref, o_ref, *, shift):
    xn = xn_ref[...]
    acc = None
    for i, (br, wg, wn) in enumerate(((br_r, wg0, wn0), (br_s, wg1, wn1), (br_g, wg2, wn2))):
        gate = _sigmoid(jnp.dot(xn, _shifted_cols(wg, wn, shift), preferred_element_type=F32))
        up = jnp.dot(br[...], wb_ref[i], preferred_element_type=F32)
        acc = gate * up if acc is None else acc + gate * up
    o_ref[...] = acc.astype(o_ref.dtype)


def merge_branches(xn, br_r, br_s, br_g, w_in, gate_col0, w_branch, layer, tn=512, tm=TM):
    t, d = xn.shape
    bw = br_r.shape[1]
    nj = d // tn
    a_spec = pl.BlockSpec((tm, bw), lambda i, j: (i, 0))
    wg_specs = []
    for n in range(N_BRANCH):
        main, nxt, shift = _shifted_specs(d, tn, layer, gate_col0, functools.partial(lambda i, j, n: n * nj + j, n=n))
        wg_specs += [main, nxt]
    return pl.pallas_call(
        functools.partial(_merge_kernel, shift=shift),
        grid=(t // tm, nj),
        in_specs=[pl.BlockSpec((tm, d), lambda i, j: (i, 0)), a_spec, a_spec, a_spec, *wg_specs,
                  pl.BlockSpec((None, N_BRANCH, bw, tn), lambda i, j: (layer, 0, 0, j))],
        out_specs=pl.BlockSpec((tm, tn), lambda i, j: (i, j)),
        out_shape=jax.ShapeDtypeStruct((t, d), BF16),
        compiler_params=_cparams(("parallel", "arbitrary")),
        name="merge",
    )(xn, br_r, br_s, br_g, *([w_in] * (2 * N_BRANCH)), w_branch)


def _outproj_kernel(m_ref, w_ref, x_ref, nw_ref, h_ref, hn_ref):
    h = x_ref[...] + jnp.dot(m_ref[...], w_ref[...], preferred_element_type=F32)
    h_ref[...] = h
    hn_ref[...] = (_rms_rows(h) * nw_ref[...]).astype(hn_ref.dtype)


def out_proj(merged, w_out, layer, x, norm_w, tm=TM):
    t, d = x.shape
    row = pl.BlockSpec((tm, d), lambda i: (i, 0))
    return pl.pallas_call(
        _outproj_kernel,
        grid=(t // tm,),
        in_specs=[row, pl.BlockSpec((None, d, d), lambda i: (layer, 0, 0)), row,
                  pl.BlockSpec((1, d), lambda i: (0, 0))],
        out_specs=[row, row],
        out_shape=[jax.ShapeDtypeStruct((t, d), F32), jax.ShapeDtypeStruct((t, d), BF16)],
        compiler_params=_cparams(("parallel",)),
        name="outproj",
    )(merged, w_out, x, norm_w.reshape(1, d))


def _ffn_kernel(hn_ref, h_ref, wg_ref, wu_ref, wd_ref, nw_ref, y_ref, xn_ref, *, nf):
    f = pl.program_id(1)

    @pl.when(f == 0)
    def _():
        y_ref[...] = h_ref[...]

    hn = hn_ref[...]
    g = jnp.dot(hn, wg_ref[...], preferred_element_type=F32)
    u = jnp.dot(hn, wu_ref[...], preferred_element_type=F32)
    act = (_silu(g) * u).astype(BF16)
    y_ref[...] += jnp.dot(act, wd_ref[...], preferred_element_type=F32)

    @pl.when(f == nf - 1)
    def _():
        xn_ref[...] = (_rms_rows(y_ref[...]) * nw_ref[...]).astype(xn_ref.dtype)


def ffn(hn, h, w_gate, w_up, w_down, layer, next_norm_w, xn_dtype, tf=512, tm=TM):
    t, d = h.shape
    dff = w_gate.shape[2]
    nf = dff // tf
    row = pl.BlockSpec((tm, d), lambda i, f: (i, 0))
    w_in_spec = pl.BlockSpec((None, d, tf), lambda i, f: (layer, 0, f))
    return pl.pallas_call(
        functools.partial(_ffn_kernel, nf=nf),
        grid=(t // tm, nf),
        in_specs=[row, row, w_in_spec, w_in_spec, pl.BlockSpec((None, tf, d), lambda i, f: (layer, f, 0)),
                  pl.BlockSpec((1, d), lambda i, f: (0, 0))],
        out_specs=[row, row],
        out_shape=[jax.ShapeDtypeStruct((t, d), F32), jax.ShapeDtypeStruct((t, d), xn_dtype)],
        compiler_params=_cparams(("parallel", "arbitrary")),
        name="ffn",
    )(hn, h, w_gate, w_up, w_down, next_norm_w.reshape(1, d))


def _glu_kernel(h_ref, w_ref, b_ref, o_ref):
    h = h_ref[...]
    z = jnp.dot(h.astype(BF16), w_ref[...], preferred_element_type=F32) + b_ref[...]
    o_ref[...] = (h * _sigmoid(z)).astype(o_ref.dtype)


def s5_glu(h, w_glu, layer, b_glu, tm=TM):
    t, d = h.shape
    row = pl.BlockSpec((tm, d), lambda i: (i, 0))
    return pl.pallas_call(
        _glu_kernel,
        grid=(t // tm,),
        in_specs=[row, pl.BlockSpec((None, d, d), lambda i: (layer, 0, 0)), pl.BlockSpec((1, d), lambda i: (0, 0))],
        out_specs=row,
        out_shape=jax.ShapeDtypeStruct((t, d), BF16),
        compiler_params=_cparams(("parallel",)),
        name="s5_glu",
    )(h, w_glu, b_glu.reshape(1, d))


def _rotary(x, cos, sin):
    half = x.shape[-1] // 2
    x1, x2 = x[:, :half], x[:, half:]
    return jnp.concatenate([x1 * cos - x2 * sin, x1 * sin + x2 * cos], axis=-1)


def _ret_prompt_kernel(lg_ref, q_ref, k_ref, v_ref, g_ref, cos_ref, sin_ref, o_ref, s_out_ref, s_ref, *, nc, chunk):
    c = pl.program_id(1)

    @pl.when(c == 0)
    def _():
        s_ref[...] = jnp.zeros_like(s_ref)

    cos, sin = cos_ref[...], sin_ref[...]
    ii = lax.broadcasted_iota(jnp.int32, (chunk, chunk), 0)
    jj = lax.broadcasted_iota(jnp.int32, (chunk, chunk), 1)
    diff = (ii - jj).astype(F32)
    idx = lax.broadcasted_iota(jnp.int32, (chunk, 1), 0).astype(F32)
    heads = range(R_HEADS)
    hs = lambda h: slice(h * R_DK, (h + 1) * R_DK)
    lg = [lg_ref[h] for h in heads]
    q_l = [_rotary(q_ref[:, hs(h)], cos, sin) for h in heads]
    k_l = [_rotary(k_ref[:, hs(h)], cos, sin) * (R_DK ** -0.5) for h in heads]
    v_l = [v_ref[:, hs(h)].astype(BF16) for h in heads]
    s_l = [s_ref[h] for h in heads]
    scores = [_dg(q_l[h].astype(BF16), k_l[h].astype(BF16), _NT)
              * jnp.where(diff >= 0, jnp.exp(lg[h] * jnp.maximum(diff, 0.0)), 0.0) for h in heads]
    cross = [_dg((q_l[h] * jnp.exp(lg[h] * (idx + 1.0))).astype(BF16), s_l[h].astype(BF16), _NN) for h in heads]
    for h in heads:
        o = _dg(scores[h].astype(BF16), v_l[h], _NN) + cross[h]
        o_ref[:, hs(h)] = (_silu(g_ref[:, hs(h)]) * _rms_rows(o)).astype(o_ref.dtype)
    for h in heads:
        k_dec = (k_l[h] * jnp.exp(lg[h] * (chunk - 1.0 - idx))).astype(BF16)
        s_ref[h] = jnp.exp(lg[h] * jnp.full((1, 1), float(chunk), F32)) * s_l[h] + _dg(k_dec, v_l[h], _TN)

    @pl.when(c == nc - 1)
    def _():
        s_out_ref[0] = s_ref[...]


def retention_prompt(p_ret, cos, sin, log_gamma, batch, seq, chunk=R_CHUNK):
    nc = seq // chunk
    bw = BRANCH_WIDTH
    rows = lambda off: pl.BlockSpec((chunk, bw), functools.partial(lambda b, c, off: (b * nc + c, off), off=off))
    tab = pl.BlockSpec((chunk, R_DK // 2), lambda b, c: (c, 0))
    return pl.pallas_call(
        functools.partial(_ret_prompt_kernel, nc=nc, chunk=chunk),
        grid=(batch, nc),
        in_specs=[pl.BlockSpec(memory_space=pltpu.SMEM), rows(0), rows(1), rows(2), rows(3), tab, tab],
        out_specs=[pl.BlockSpec((chunk, bw), lambda b, c: (b * nc + c, 0)),
                   pl.BlockSpec((1, R_HEADS, R_DK, R_DK), lambda b, c: (b, 0, 0, 0))],
        out_shape=[jax.ShapeDtypeStruct((p_ret.shape[0], bw), BF16),
                   jax.ShapeDtypeStruct((batch, R_HEADS, R_DK, R_DK), F32)],
        scratch_shapes=[pltpu.VMEM((R_HEADS, R_DK, R_DK), F32)],
        compiler_params=_cparams(("parallel", "arbitrary")),
        name="ret_prompt",
    )(log_gamma, p_ret, p_ret, p_ret, p_ret, cos, sin)


def _ret_step_kernel(lg_ref, q_ref, k_ref, v_ref, g_ref, cos_ref, sin_ref, s_ref, *rest, bb):
    o_ref, s_out_ref = rest[-2:]
    h = pl.program_id(0)
    gamma = jnp.exp(lg_ref[h] * jnp.ones((1, 1), F32))
    cos, sin = cos_ref[...], sin_ref[...]
    q = _rotary(q_ref[...], cos, sin)
    k = _rotary(k_ref[...], cos, sin) * (R_DK ** -0.5)
    v = v_ref[...]
    eye = _eye(R_DK)
    q_cols = _dot_sel(eye, q, _NT)
    k_cols = _dot_sel(eye, k, _NT)
    for i in range(bb):
        s_new = gamma * s_ref[i, 0] + k_cols[:, i:i + 1] * v[i:i + 1, :]
        s_out_ref[i, 0] = s_new
        o = jnp.sum(q_cols[:, i:i + 1] * s_new, axis=0, keepdims=True)
        o_ref[i:i + 1, :] = (_silu(g_ref[i:i + 1, :]) * _rms_rows(o)).astype(o_ref.dtype)


def _alias_args(first_index, outputs):
    args, specs, aliases = [], [], {}
    for out_idx, a in outputs.items():
        if a is not None:
            aliases[first_index + len(args)] = out_idx
            args.append(a)
            specs.append(pl.BlockSpec(memory_space=pl.ANY))
    return args, specs, aliases


def retention_step(p_ret, row0, cos, sin, log_gamma, states, layer, o_all, s_all, bb=16):
    nb = states.shape[1]
    r0 = row0 // bb
    rows = lambda off: pl.BlockSpec((bb, R_DK), functools.partial(lambda h, j, off: (r0 + j, off + h), off=off))
    tab = pl.BlockSpec((1, R_DK // 2), lambda h, j: (0, 0))
    st = pl.BlockSpec((None, bb, 1, R_DK, R_DK), lambda h, j: (layer, j, h, 0, 0))
    extra, extra_specs, aliases = _alias_args(8, {0: o_all, 1: s_all})
    return pl.pallas_call(
        functools.partial(_ret_step_kernel, bb=bb),
        grid=(R_HEADS, nb // bb),
        in_specs=[pl.BlockSpec(memory_space=pltpu.SMEM), rows(0), rows(R_HEADS), rows(2 * R_HEADS),
                  rows(3 * R_HEADS), tab, tab, st, *extra_specs],
        out_specs=[pl.BlockSpec((bb, R_DK), lambda h, j: (r0 + j, h)), st],
        out_shape=[jax.ShapeDtypeStruct(o_all.shape, BF16), jax.ShapeDtypeStruct(states.shape, F32)],
        input_output_aliases=aliases,
        compiler_params=_cparams(("parallel", "parallel")),
        name="ret_step",
    )(log_gamma, p_ret, p_ret, p_ret, p_ret, cos, sin, states, *extra)


def _cmul(ar, ai, br, bi):
    return ar * br - ai * bi, ar * bi + ai * br


def s5_discretize(lam_re, lam_im, log_dt, b_re, b_im):
    dt = jnp.exp(log_dt)[:, None]
    mag = jnp.exp(lam_re * dt)
    lbr, lbi = mag * jnp.cos(lam_im * dt), mag * jnp.sin(lam_im * dt)
    den = lam_re * lam_re + lam_im * lam_im
    nr, ni = lbr - 1.0, lbi
    cr, ci = (nr * lam_re + ni * lam_im) / den, (ni * lam_re - nr * lam_im) / den
    bbr, bbi = _cmul(cr[..., None], ci[..., None], b_re, b_im)
    return dt, lbr, lbi, bbr, bbi


def _lam_power(lam_re, lam_im, dt, n):
    n = jnp.asarray(n, F32).reshape((-1,) + (1,) * lam_re.ndim)
    mag = jnp.exp(lam_re * dt * n)
    return mag * jnp.cos(lam_im * dt * n), mag * jnp.sin(lam_im * dt * n)


def s5_prompt_params(lam_re, lam_im, log_dt, b_re, b_im, c_re, c_im, d_skip):
    g, p, q = S5_GROUPS, S5_STATE, S5_CHUNK
    dt, lbr, lbi, bbr, bbi = s5_discretize(lam_re, lam_im, log_dt, b_re, b_im)
    pr, pi = _lam_power(lam_re, lam_im, dt, jnp.arange(q + 1))
    lb_r, lb_i = _cmul(pr[:q, :, :, None], pi[:q, :, :, None], bbr[None], bbi[None])
    hp = lax.Precision.HIGHEST
    kern = (jnp.einsum('gkp,tgpc->tgkc', c_re, lb_r, precision=hp)
            - jnp.einsum('gkp,tgpc->tgkc', c_im, lb_i, precision=hp))
    krow = kern.transpose(1, 3, 0, 2).reshape(g, S5_GROUP, q * S5_GROUP)
    er, ei = _cmul(pr[:q][::-1][:, :, :, None], pi[:q][::-1][:, :, :, None], bbr[None], bbi[None])
    w_re = er.transpose(1, 0, 3, 2).reshape(g, q * S5_GROUP, p)
    w_im = ei.transpose(1, 0, 3, 2).reshape(g, q * S5_GROUP, p)
    vr, vi = _cmul(c_re[None], c_im[None], pr[1:, :, None, :], pi[1:, :, None, :])
    v_re = vr.transpose(1, 3, 0, 2).reshape(g, p, q * S5_GROUP)
    v_im = (-vi).transpose(1, 3, 0, 2).reshape(g, p, q * S5_GROUP)

    def pair_cols(a):
        z = jnp.zeros_like(a)
        even = (jnp.arange(g) % 2 == 0)[:, None, None]
        return jnp.concatenate([jnp.where(even, a, z), jnp.where(even, z, a)], axis=-1)

    def pair_rows(a):
        z = jnp.zeros_like(a)
        even = (jnp.arange(g) % 2 == 0)[:, None, None]
        return jnp.concatenate([jnp.where(even, a, z), jnp.where(even, z, a)], axis=1)

    sr, si = _lam_power(lam_re, lam_im, dt, q * (2.0 ** jnp.arange(8)))
    lam_sc_re = sr.reshape(8, g // 2, 2 * p).transpose(1, 0, 2)
    lam_sc_im = si.reshape(8, g // 2, 2 * p).transpose(1, 0, 2)
    d_vec = jnp.tile(d_skip.reshape(g, 1, S5_GROUP), (1, q, 1)).reshape(g, 1, q * S5_GROUP)
    bf = lambda a: a.astype(BF16)
    return (krow, bf(pair_cols(w_re)), bf(pair_cols(w_im)), bf(pair_rows(v_re)), bf(pair_rows(v_im)),
            lam_sc_re, lam_sc_im, d_vec)


S5_TILE_GROUPS = 128 // S5_GROUP


def _toeplitz(krow, q, gw):
    blocks = [krow] + [jnp.concatenate([jnp.zeros((gw, s * gw), F32), krow[:, :(q - s) * gw]], axis=1)
                       for s in range(1, q)]
    return jnp.concatenate(blocks, axis=0).astype(BF16)


def _s5_prompt_kernel(u_ref, krow_ref, wre_ref, wim_ref, vre_ref, vim_ref, lre_ref, lim_ref, d_ref,
                      h_ref, fre_ref, fim_ref, ut_ref, ht_ref, tile_ref, *, nchb, nb):
    rows = nb * nchb
    q, gw, ng = S5_CHUNK, S5_GROUP, S5_TILE_GROUPS
    for s in range(q):
        xs = u_ref[pl.ds(s, rows, stride=q), :]
        for gg in range(ng):
            ut_ref[gg, :, s * gw:(s + 1) * gw] = xs[:, gg * gw:(gg + 1) * gw]

    pos = jnp.bitwise_and(lax.broadcasted_iota(jnp.int32, (rows, 2 * S5_STATE), 0), nchb - 1)
    first = pos == 0
    pairs = range(ng // 2)
    ub_l = [(ut_ref[2 * pr].astype(BF16), ut_ref[2 * pr + 1].astype(BF16)) for pr in pairs]
    e_re = [_dg(ub_l[pr][0], wre_ref[2 * pr], _NN) + _dg(ub_l[pr][1], wre_ref[2 * pr + 1], _NN) for pr in pairs]
    e_im = [_dg(ub_l[pr][0], wim_ref[2 * pr], _NN) + _dg(ub_l[pr][1], wim_ref[2 * pr + 1], _NN) for pr in pairs]
    d, k = 1, 0
    while d < nchb:
        keep = pos >= d
        for pr in pairs:
            lr, li = lre_ref[pr, k:k + 1, :], lim_ref[pr, k:k + 1, :]
            sr, si = pltpu.roll(e_re[pr], d, 0), pltpu.roll(e_im[pr], d, 0)
            e_re[pr], e_im[pr] = (e_re[pr] + jnp.where(keep, lr * sr - li * si, 0.0),
                                  e_im[pr] + jnp.where(keep, lr * si + li * sr, 0.0))
        d, k = d * 2, k + 1
    for pr in pairs:
        for b in range(nb):
            r = (b + 1) * nchb - 1
            fre_ref[pr, b:b + 1, :] = e_re[pr][r:r + 1, :]
            fim_ref[pr, b:b + 1, :] = e_im[pr][r:r + 1, :]
    for pr in pairs:
        xc_re = jnp.where(first, 0.0, pltpu.roll(e_re[pr], 1, 0)).astype(BF16)
        xc_im = jnp.where(first, 0.0, pltpu.roll(e_im[pr], 1, 0)).astype(BF16)
        for gi in range(2):
            gg = 2 * pr + gi
            y = (_dg(ub_l[pr][gi], _toeplitz(krow_ref[gg], q, gw), _NN)
                 + _dg(xc_re, vre_ref[gg], _NN) + _dg(xc_im, vim_ref[gg], _NN)
                 ) + ut_ref[gg] * d_ref[gg]
            ht_ref[gg] = _gelu_tanh(y)
    for s in range(q):
        for gg in range(ng):
            tile_ref[:, gg * gw:(gg + 1) * gw] = ht_ref[gg, :, s * gw:(s + 1) * gw]
        h_ref[pl.ds(s, rows, stride=q), :] = tile_ref[...]


def s5_prompt(p_s5, params, layer, batch, seq):
    krow, w_re, w_im, v_re, v_im, l_re, l_im, d_vec = params
    ng, qc, p2 = S5_TILE_GROUPS, S5_CHUNK * S5_GROUP, 2 * S5_STATE
    nchb = seq // S5_CHUNK
    rows = batch * nchb
    tp = batch * seq
    nt = BRANCH_WIDTH // 128
    grp = lambda n, *tail: pl.BlockSpec((None, n) + tail, lambda j: (layer, j) + (0,) * len(tail))
    fin = pl.BlockSpec((ng // 2, batch, p2), lambda j: (j, 0, 0))
    return pl.pallas_call(
        functools.partial(_s5_prompt_kernel, nchb=nchb, nb=batch),
        grid=(nt,),
        in_specs=[pl.BlockSpec((tp, 128), lambda j: (0, j)), grp(ng, S5_GROUP, qc), grp(ng, qc, p2), grp(ng, qc, p2),
                  grp(ng, p2, qc), grp(ng, p2, qc), grp(ng // 2, 8, p2), grp(ng // 2, 8, p2), grp(ng, 1, qc)],
        out_specs=[pl.BlockSpec((tp, 128), lambda j: (0, j)), fin, fin],
        out_shape=[jax.ShapeDtypeStruct(p_s5.shape, F32),
                   jax.ShapeDtypeStruct((S5_GROUPS // 2, batch, p2), F32),
                   jax.ShapeDtypeStruct((S5_GROUPS // 2, batch, p2), F32)],
        scratch_shapes=[pltpu.VMEM((ng, rows, qc), F32), pltpu.VMEM((ng, rows, qc), F32),
                        pltpu.VMEM((rows, 128), F32)],
        compiler_params=_cparams(("parallel",)),
        name="s5_prompt",
    )(p_s5, krow, w_re, w_im, v_re, v_im, l_re, l_im, d_vec)


def s5_step_params(lam_re, lam_im, log_dt, b_re, b_im, c_re, c_im, d_skip):
    g, p = S5_GROUPS, S5_STATE
    _, lbr, lbi, bbr, bbi = s5_discretize(lam_re, lam_im, log_dt, b_re, b_im)
    eye8 = jnp.eye(8, dtype=F32)

    def bd_in(bb):
        x = bb.reshape(8, 8, p, S5_GROUP)
        return jnp.einsum('jgpc,gh->jgchp', x, eye8).reshape(8, 8 * S5_GROUP, 8 * p)

    def bd_out(cc):
        x = cc.reshape(8, 8, S5_GROUP, p)
        return jnp.einsum('jgkp,gh->jgphk', x, eye8).reshape(8, 8 * p, 8 * S5_GROUP)

    return (bd_in(bbr), bd_in(bbi), bd_out(c_re), bd_out(-c_im), lbr.reshape(1, g * p), lbi.reshape(1, g * p),
            d_skip.reshape(1, g * S5_GROUP))


def _s5_step_kernel(u_ref, xr_ref, xi_ref, bre_ref, bim_ref, cre_ref, cim_ref, lr_ref, li_ref, d_ref, *rest):
    h_ref, nr_ref, ni_ref = rest[-3:]
    u = u_ref[...]
    lr, li = lr_ref[...], li_ref[...]
    xr, xi = xr_ref[...], xi_ref[...]
    nr = lr * xr - li * xi + _dot3(u, bre_ref[0])
    ni = lr * xi + li * xr + _dot3(u, bim_ref[0])
    nr_ref[...] = nr
    ni_ref[...] = ni
    y = _dot3(nr, cre_ref[0]) + _dot3(ni, cim_ref[0]) + u * d_ref[...]
    h_ref[...] = _gelu_tanh(y)


def s5_step(p_s5, row0, x_re, x_im, params, layer, h_all):
    b_re, b_im, c_re, c_im, l_re, l_im, d_vec = params
    nb, gp = x_re.shape
    lt = 8 * S5_GROUP
    st = 8 * S5_STATE
    r0 = row0 // nb
    u_spec = pl.BlockSpec((nb, lt), lambda j: (r0, j))
    x_spec = pl.BlockSpec((nb, st), lambda j: (0, j))
    mat = lambda a, b: pl.BlockSpec((None, 1, a, b), lambda j: (layer, j, 0, 0))
    vec = lambda n: pl.BlockSpec((None, 1, n), lambda j: (layer, 0, j))
    extra, extra_specs, aliases = _alias_args(10, {0: h_all})
    return pl.pallas_call(
        _s5_step_kernel,
        grid=(gp // st,),
        in_specs=[u_spec, x_spec, x_spec, mat(lt, st), mat(lt, st), mat(st, lt), mat(st, lt),
                  vec(st), vec(st), vec(lt), *extra_specs],
        out_specs=[u_spec, x_spec, x_spec],
        out_shape=[jax.ShapeDtypeStruct(h_all.shape, F32), jax.ShapeDtypeStruct((nb, gp), F32),
                   jax.ShapeDtypeStruct((nb, gp), F32)],
        input_output_aliases=aliases,
        compiler_params=_cparams(("parallel",)),
        name="s5_step",
    )(p_s5, x_re, x_im, b_re, b_im, c_re, c_im, l_re, l_im, d_vec, *extra)


def _l2norm_rows(x):
    return x * lax.rsqrt(jnp.sum(x * x, axis=-1, keepdims=True) + NORM_EPS)


def _gdn_prompt_kernel(qkv_ref, z_ref, ab_ref, cw_ref, alog_ref, dtb_ref, nw_ref, o_ref, s_out_ref,
                       xbuf_ref, s_ref, *, nc, chunk):
    c = pl.program_id(1)
    bw = BRANCH_WIDTH
    halo = 8

    @pl.when(c == 0)
    def _():
        xbuf_ref[0:halo, :] = jnp.zeros((halo, 3 * bw), F32)
        s_ref[...] = jnp.zeros_like(s_ref)

    x = qkv_ref[...]
    xbuf_ref[halo:halo + chunk, :] = x
    conv = x * cw_ref[G_CONV - 1:G_CONV, :]
    for i in range(1, G_CONV):
        conv = conv + xbuf_ref[halo - i:halo - i + chunk, :] * cw_ref[G_CONV - 1 - i:G_CONV - i, :]
    xbuf_ref[0:halo, :] = x[chunk - halo:chunk, :]
    qkv = _silu(conv)

    ab = ab_ref[...]
    g_all = -jnp.exp(alog_ref[...]) * _softplus(ab + dtb_ref[...])
    beta_all = _sigmoid(ab)
    ri = lax.broadcasted_iota(jnp.int32, (chunk, chunk), 0)
    ci = lax.broadcasted_iota(jnp.int32, (chunk, chunk), 1)
    causal = ri >= ci
    strict = ri > ci
    ltri = jnp.where(causal, 1.0, 0.0).astype(BF16)
    gcum = _dot_sel(ltri, g_all)
    gcum_t = _dot_sel(_eye(128), gcum, _NT)
    z = z_ref[...]
    nw = nw_ref[...]

    heads = range(G_HEADS)
    q_l, gc_l, eg_l, qkd_l, kdec_l, m_l, sol_l = [], [], [], [], [], [], []
    for h in heads:
        q = _l2norm_rows(qkv[:, h * G_DK:(h + 1) * G_DK]) * (G_DK ** -0.5)
        k = _l2norm_rows(qkv[:, bw + h * G_DK:bw + (h + 1) * G_DK])
        v = qkv[:, 2 * bw + h * G_DK:2 * bw + (h + 1) * G_DK]
        gc = gcum[:, h:h + 1]
        gr = gcum_t[h:h + 1, :]
        beta = beta_all[:, G_HEADS + h:G_HEADS + h + 1]
        decay = jnp.where(causal, jnp.exp(jnp.where(causal, gc - gr, 0.0)), 0.0)
        kb = k.astype(BF16)
        qk_kk = _dg(jnp.concatenate([kb, q.astype(BF16)], axis=0), kb, _NT)
        eg = jnp.exp(gc)
        q_l.append(q * eg)
        gc_l.append(gc)
        qkd_l.append((qk_kk[chunk:] * decay).astype(BF16))
        kdec_l.append((k * jnp.exp(gc[chunk - 1:chunk, :] - gc)).astype(BF16))
        m_l.append(jnp.where(strict, -(beta * qk_kk[:chunk] * decay), 0.0))
        sol_l.append(jnp.concatenate([beta * v, (beta * eg) * k], axis=-1))
    span = 1
    while span < chunk:
        for h in heads:
            mb = m_l[h].astype(BF16)
            if 2 * span < chunk:
                sol_hi, sol_lo = _split2(sol_l[h])
                r = _dg(mb, jnp.concatenate([sol_hi, mb], axis=-1), _NN)
                sol_l[h] = sol_l[h] + (r[:, :2 * G_DK] + _dg(mb, sol_lo, _NN))
                m_l[h] = r[:, 2 * G_DK:]
            else:
                sol_l[h] = sol_l[h] + _dot2(mb, sol_l[h])
        span *= 2
    s_l = [s_ref[h] for h in heads]
    ks_qs_l = [_dg(jnp.concatenate([sol_l[h][:, G_DK:], q_l[h]], axis=0).astype(BF16), s_l[h].astype(BF16), _NN)
               for h in heads]
    wb_l = [(sol_l[h][:, :G_DK] - ks_qs_l[h][:chunk]).astype(BF16) for h in heads]
    for h in heads:
        o = ks_qs_l[h][chunk:] + _dg(qkd_l[h], wb_l[h], _NN)
        zh = z[:, h * G_DK:(h + 1) * G_DK]
        o_ref[:, h * G_DK:(h + 1) * G_DK] = (_rms_rows(o) * nw * _silu(zh)).astype(o_ref.dtype)
    for h in heads:
        g_last = gc_l[h][chunk - 1:chunk, :]
        s_ref[h] = jnp.exp(g_last) * s_l[h] + _dg(kdec_l[h], wb_l[h], _TN)

    @pl.when(c == nc - 1)
    def _():
        s_out_ref[0] = s_ref[...]


def gdn_prompt(p_qkv, p_z, p_ab, conv_w, a_log, dt_bias, norm_w, batch, seq, chunk=G_CHUNK):
    nc = seq // chunk
    bw = BRANCH_WIDTH
    pad = lambda x: jnp.zeros((1, 128), F32).at[0, :G_HEADS].set(x)
    full = lambda shape: pl.BlockSpec(shape, lambda b, c: (0,) * len(shape))
    rows = lambda width: pl.BlockSpec((chunk, width), lambda b, c: (b * nc + c, 0))
    return pl.pallas_call(
        functools.partial(_gdn_prompt_kernel, nc=nc, chunk=chunk),
        grid=(batch, nc),
        in_specs=[rows(3 * bw), rows(bw), rows(128),
                  full((G_CONV, 3 * bw)), full((1, 128)), full((1, 128)), full((1, G_DK))],
        out_specs=[rows(bw), pl.BlockSpec((1, G_HEADS, G_DK, G_DK), lambda b, c: (b, 0, 0, 0))],
        out_shape=[jax.ShapeDtypeStruct((p_qkv.shape[0], bw), BF16),
                   jax.ShapeDtypeStruct((batch, G_HEADS, G_DK, G_DK), F32)],
        scratch_shapes=[pltpu.VMEM((8 + chunk, 3 * bw), F32), pltpu.VMEM((G_HEADS, G_DK, G_DK), F32)],
        compiler_params=_cparams(("parallel", "arbitrary")),
        name="gdn_prompt",
    )(p_qkv, p_z, p_ab, conv_w, pad(a_log), pad(dt_bias), norm_w.reshape(1, G_DK))


def _gdn_step_kernel(qkv_ref, z_ref, ab_ref, cbuf_ref, cw_ref, alog_ref, dtb_ref, nw_ref, s_ref, *rest, bb):
    o_ref, cbuf_out_ref, s_out_ref = rest[-3:]
    bw = BRANCH_WIDTH
    x = qkv_ref[...]
    conv = x * cw_ref[G_CONV - 1:G_CONV, :]
    for i in range(G_CONV - 1):
        conv = conv + cbuf_ref[i] * cw_ref[i:i + 1, :]
    for i in range(G_CONV - 2):
        cbuf_out_ref[i] = cbuf_ref[i + 1]
    cbuf_out_ref[G_CONV - 2] = x
    qkv = _silu(conv)
    ab = ab_ref[...]
    g_all = -jnp.exp(alog_ref[...]) * _softplus(ab + dtb_ref[...])
    beta_all = _sigmoid(ab)
    eye = _eye(G_DK)
    z = z_ref[...]
    nw = nw_ref[...]
    for h in range(G_HEADS):
        q = _l2norm_rows(qkv[:, h * G_DK:(h + 1) * G_DK]) * (G_DK ** -0.5)
        k = _l2norm_rows(qkv[:, bw + h * G_DK:bw + (h + 1) * G_DK])
        v = qkv[:, 2 * bw + h * G_DK:2 * bw + (h + 1) * G_DK]
        eg = jnp.exp(g_all[:, h:h + 1])
        beta = beta_all[:, G_HEADS + h:G_HEADS + h + 1]
        q_cols = _dot_sel(eye, q, _NT)
        k_cols = _dot_sel(eye, k, _NT)
        rng = range(bb)
        s_l = [s_ref[i, h] for i in rng]
        ks_l = [jnp.sum(k_cols[:, i:i + 1] * s_l[i], axis=0, keepdims=True) for i in rng]
        w_l = [beta[i:i + 1, :] * (v[i:i + 1, :] - eg[i:i + 1, :] * ks_l[i]) for i in rng]
        sn_l = [eg[i:i + 1, :] * s_l[i] + k_cols[:, i:i + 1] * w_l[i] for i in rng]
        for i in rng:
            s_out_ref[i, h] = sn_l[i]
        o = jnp.concatenate([jnp.sum(q_cols[:, i:i + 1] * sn_l[i], axis=0, keepdims=True) for i in rng], axis=0)
        zh = z[:, h * G_DK:(h + 1) * G_DK]
        o_ref[:, h * G_DK:(h + 1) * G_DK] = (_rms_rows(o) * nw * _silu(zh)).astype(o_ref.dtype)


def gdn_step(p_qkv, p_z, p_ab, row0, conv_buf_t, states, layer, conv_w, a_log, dt_bias, norm_w, o_all, s_all, bb=8):
    nb = states.shape[1]
    bw = BRANCH_WIDTH
    r0 = row0 // bb
    pad = lambda x: jnp.zeros((1, 128), F32).at[0, :G_HEADS].set(x)
    full = lambda shape: pl.BlockSpec(shape, lambda j: (0,) * len(shape))
    cb = pl.BlockSpec((G_CONV - 1, bb, 3 * bw), lambda j: (0, j, 0))
    st = pl.BlockSpec((None, bb, G_HEADS, G_DK, G_DK), lambda j: (layer, j, 0, 0, 0))
    extra, extra_specs, aliases = _alias_args(9, {0: o_all, 2: s_all})
    return pl.pallas_call(
        functools.partial(_gdn_step_kernel, bb=bb),
        grid=(nb // bb,),
        in_specs=[pl.BlockSpec((bb, 3 * bw), lambda j: (r0 + j, 0)),
                  pl.BlockSpec((bb, bw), lambda j: (r0 + j, 0)),
                  pl.BlockSpec((bb, 128), lambda j: (r0 + j, 0)),
                  cb, full((G_CONV, 3 * bw)), full((1, 128)), full((1, 128)), full((1, G_DK)), st, *extra_specs],
        out_specs=[pl.BlockSpec((bb, bw), lambda j: (r0 + j, 0)), cb, st],
        out_shape=[jax.ShapeDtypeStruct(o_all.shape, BF16), jax.ShapeDtypeStruct(conv_buf_t.shape, F32),
                   jax.ShapeDtypeStruct(states.shape, F32)],
        input_output_aliases=aliases,
        compiler_params=_cparams(("parallel",)),
        name="gdn_step",
    )(p_qkv, p_z, p_ab, conv_buf_t, conv_w, pad(a_log), pad(dt_bias), norm_w.reshape(1, G_DK), states, *extra)


def _rope_tables(pos):
    half = R_DK // 2
    inv = ROPE_BASE ** (-jnp.arange(half, dtype=F32) / half)
    ang = pos.astype(F32)[:, None] * inv[None, :]
    return jnp.cos(ang), jnp.sin(ang)


def kernel(x_prompt, x_sample, state_ret, state_s5_re, state_s5_im, state_gdn, state_gdn_conv, norm_mix_w, w_in, s5_lambda_re, s5_lambda_im, s5_log_dt, s5_b_re, s5_b_im, s5_c_re, s5_c_im, s5_d, s5_w_glu, s5_b_glu, gdn_conv_w, gdn_a_log, gdn_dt_bias, gdn_norm_w, w_branch, w_out, norm_ffn_w, w_gate_ffn, w_up_ffn, w_down_ffn, norm_final_w):
    bp, lp, d = x_prompt.shape
    nb = x_sample.shape[0]
    depth = w_in.shape[0]
    tp = bp * lp
    bw = BRANCH_WIDTH
    x = jnp.concatenate([x_prompt.reshape(tp, d), x_sample.reshape(nb, d)], axis=0)

    cos_p, sin_p = _rope_tables(jnp.arange(lp, dtype=jnp.int32))
    cos_s, sin_s = _rope_tables(PAST_LEN + jnp.arange(1, dtype=jnp.int32))
    log_gamma = jnp.log1p(-jnp.exp2(-5.0 - jnp.arange(R_HEADS, dtype=F32)))

    o_su, o_gqkv, o_ga, o_gz, o_gates = 4 * bw, 5 * bw, 8 * bw, 8 * bw + 2 * G_HEADS, 9 * bw + 2 * G_HEADS
    w_in_b = w_in.astype(BF16)
    w_branch_b, w_out_b, w_glu_b = w_branch.astype(BF16), w_out.astype(BF16), s5_w_glu.astype(BF16)
    w_gate_b, w_up_b, w_down_b = w_gate_ffn.astype(BF16), w_up_ffn.astype(BF16), w_down_ffn.astype(BF16)
    s5_raw = (s5_lambda_re, s5_lambda_im, s5_log_dt, s5_b_re, s5_b_im, s5_c_re, s5_c_im, s5_d)
    s5_pp = jax.vmap(s5_prompt_params)(*s5_raw)
    s5_sp = jax.vmap(s5_step_params)(*s5_raw)

    new_p = [[], [], [], [], []]
    new_s = [[], [], []]
    ret_s_all = gdn_s_all = None
    xn = rmsnorm_rows(x, norm_mix_w[0], BF16)
    y_final = None
    for l in range(depth):
        p_ret = matmul(xn, w_in_b, l, 0, o_su, tn=1024, name="proj_ret")
        p_s5 = matmul(xn, w_in_b, l, o_su, bw, tn=1024, name="proj_s5")
        p_qkv = matmul(xn, w_in_b, l, o_gqkv, 3 * bw, tn=1024, name="proj_qkv")
        p_ab = matmul(xn, w_in_b, l, o_ga, LANE, tn=LANE, name="proj_ab")
        p_z = matmul_shifted(xn, w_in_b, l, o_gz, bw, tn=512, name="proj_z")

        r_all, ret_p = retention_prompt(p_ret, cos_p, sin_p, log_gamma, bp, lp)
        r_all, ret_s_all = retention_step(p_ret, tp, cos_s, sin_s, log_gamma, state_ret, l, r_all, ret_s_all)

        h_all, f_re, f_im = s5_prompt(p_s5, s5_pp, l, bp, lp)
        fin = lambda f: f[:, :bp, :].reshape(S5_GROUPS // 2, bp, 2, S5_STATE).transpose(1, 0, 2, 3).reshape(
            bp, S5_GROUPS, S5_STATE)
        h_all, s5re_s, s5im_s = s5_step(p_s5, tp, state_s5_re[l].reshape(nb, -1), state_s5_im[l].reshape(nb, -1),
                                        s5_sp, l, h_all)
        s_all = s5_glu(h_all, w_glu_b, l, s5_b_glu[l])

        g_all, gdn_p = gdn_prompt(p_qkv, p_z, p_ab, gdn_conv_w[l], gdn_a_log[l], gdn_dt_bias[l], gdn_norm_w[l], bp, lp)
        g_all, conv_s_t, gdn_s_all = gdn_step(p_qkv, p_z, p_ab, tp, state_gdn_conv[l].transpose(1, 0, 2), state_gdn, l,
                                              gdn_conv_w[l], gdn_a_log[l], gdn_dt_bias[l], gdn_norm_w[l],
                                              g_all, gdn_s_all)
        conv_p = jnp.stack([p_qkv[(b + 1) * lp - (G_CONV - 1):(b + 1) * lp] for b in range(bp)], axis=0)

        merged = merge_branches(xn, r_all, s_all, g_all, w_in_b, o_gates, w_branch_b, l)
        h, hn = out_proj(merged, w_out_b, l, x, norm_ffn_w[l])
        last = l == depth - 1
        next_w = norm_final_w if last else norm_mix_w[l + 1]
        x, xn = ffn(hn, h, w_gate_b, w_up_b, w_down_b, l, next_w, F32 if last else BF16)
        if last:
            y_final = xn

        for lst, val in zip(new_p, (ret_p, fin(f_re), fin(f_im), gdn_p, conv_p)):
            lst.append(val)
        for lst, val in zip(new_s, (s5re_s.reshape(nb, S5_GROUPS, S5_STATE),
                                    s5im_s.reshape(nb, S5_GROUPS, S5_STATE), conv_s_t.transpose(1, 0, 2))):
            lst.append(val)

    y_prompt = y_final[:tp].reshape(bp, lp, d)
    y_sample = y_final[tp:].reshape(nb, 1, d)
    outs_p = [jnp.stack(a, 0) for a in new_p]
    s5re_s, s5im_s, conv_s = [jnp.stack(a, 0) for a in new_s]
    return (y_prompt, y_sample, *outs_p, ret_s_all, s5re_s, s5im_s, gdn_s_all, conv_s)
```

```python
import functools
import math

import jax
import jax.numpy as jnp
from jax import lax
from jax.experimental import pallas as pl
from jax.experimental.pallas import tpu as pltpu

F32 = jnp.float32
BF16 = jnp.bfloat16

D_MODEL = 2048
BRANCH_WIDTH = 1024
N_BRANCH = 3
R_HEADS = 4
R_DK = 256
R_CHUNK = 128
ROPE_BASE = 10000.0
S5_GROUP = 16
S5_GROUPS = 64
S5_STATE = 64
S5_CHUNK = 16
G_HEADS = 8
G_DK = 128
G_CONV = 4
G_CHUNK = 64
NORM_EPS = 1e-6
PAST_LEN = 16384

VMEM_LIMIT_BYTES = 56 * 1024 * 1024
TM = 640


def _cparams(sem):
    return pltpu.CompilerParams(dimension_semantics=sem, vmem_limit_bytes=VMEM_LIMIT_BYTES)


_NN = (((1,), (0,)), ((), ()))
_NT = (((1,), (1,)), ((), ()))
_TN = (((0,), (0,)), ((), ()))


def _dg(a, b, dims):
    return lax.dot_general(a, b, dims, preferred_element_type=F32)


def _split2(a):
    hi = a.astype(BF16)
    lo = (a - hi.astype(F32)).astype(BF16)
    return hi, lo


def _dot3(a, b, dims=_NN):
    ah, al = _split2(a)
    bh, bl = _split2(b)
    return _dg(ah, bh, dims) + (_dg(ah, bl, dims) + _dg(al, bh, dims))


def _dot2(a_bf16, b, dims=_NN):
    bh, bl = _split2(b)
    return _dg(a_bf16, bh, dims) + _dg(a_bf16, bl, dims)


def _dot_sel(sel_bf16, x, dims=_NN):
    h1 = x.astype(BF16)
    r1 = x - h1.astype(F32)
    h2 = r1.astype(BF16)
    h3 = (r1 - h2.astype(F32)).astype(BF16)
    return _dg(sel_bf16, h1, dims) + (_dg(sel_bf16, h2, dims) + _dg(sel_bf16, h3, dims))


def _sel_dot(x, sel_bf16, dims=_NN):
    h1 = x.astype(BF16)
    r1 = x - h1.astype(F32)
    h2 = r1.astype(BF16)
    h3 = (r1 - h2.astype(F32)).astype(BF16)
    return _dg(h1, sel_bf16, dims) + (_dg(h2, sel_bf16, dims) + _dg(h3, sel_bf16, dims))


def _eye(n, dtype=BF16):
    r = lax.broadcasted_iota(jnp.int32, (n, n), 0)
    c = lax.broadcasted_iota(jnp.int32, (n, n), 1)
    return jnp.where(r == c, 1.0, 0.0).astype(dtype)


def _sigmoid(x):
    return 1.0 / (1.0 + jnp.exp(-x))


def _silu(x):
    return x * _sigmoid(x)


def _gelu_tanh(x):
    return 0.5 * x * (1.0 + jnp.tanh(math.sqrt(2.0 / math.pi) * (x + 0.044715 * (x * x * x))))


def _softplus(x):
    return jnp.maximum(x, 0.0) + jnp.log1p(jnp.exp(-jnp.abs(x)))


def _rms_rows(x, eps=NORM_EPS):
    return x * lax.rsqrt(jnp.mean(x * x, axis=-1, keepdims=True) + eps)


def _norm_kernel(x_ref, w_ref, o_ref):
    o_ref[...] = (_rms_rows(x_ref[...]) * w_ref[...]).astype(o_ref.dtype)


def rmsnorm_rows(x, w, out_dtype, tm=TM):
    t, d = x.shape
    return pl.pallas_call(
        _norm_kernel,
        grid=(t // tm,),
        in_specs=[pl.BlockSpec((tm, d), lambda i: (i, 0)), pl.BlockSpec((1, d), lambda i: (0, 0))],
        out_specs=pl.BlockSpec((tm, d), lambda i: (i, 0)),
        out_shape=jax.ShapeDtypeStruct((t, d), out_dtype),
        compiler_params=_cparams(("parallel",)),
        name="rmsnorm",
    )(x, w.reshape(1, d))


def _mm_kernel(a_ref, w_ref, o_ref):
    o_ref[...] = jnp.dot(a_ref[...], w_ref[...], preferred_element_type=F32)


def matmul(a, w, layer, col0, n, tn, tm=TM, name="proj"):
    t, k = a.shape
    c0 = col0 // tn
    return pl.pallas_call(
        _mm_kernel,
        grid=(t // tm, n // tn),
        in_specs=[pl.BlockSpec((tm, k), lambda i, j: (i, 0)),
                  pl.BlockSpec((None, k, tn), lambda i, j: (layer, 0, c0 + j))],
        out_specs=pl.BlockSpec((tm, tn), lambda i, j: (i, j)),
        out_shape=jax.ShapeDtypeStruct((t, n), F32),
        compiler_params=_cparams(("parallel", "arbitrary")),
        name=name,
    )(a, w)


LANE = 128


def _shifted_cols(w_ref, wn_ref, shift):
    if shift == 0:
        return w_ref[...]
    return jnp.concatenate([w_ref[:, shift:], wn_ref[:, :shift]], axis=1)


def _shifted_specs(d, tn, layer, col0, nidx):
    base, shift = (col0 // tn) * tn, col0 % tn
    assert base % tn == 0 and shift < LANE and tn % LANE == 0
    main = pl.BlockSpec((None, d, tn), lambda i, j: (layer, 0, base // tn + nidx(i, j)))
    nxt = pl.BlockSpec((None, d, LANE), lambda i, j: (layer, 0, (base + (nidx(i, j) + 1) * tn) // LANE))
    return main, nxt, shift


def _mm_shift_kernel(a_ref, w_ref, wn_ref, o_ref, *, shift):
    o_ref[...] = jnp.dot(a_ref[...], _shifted_cols(w_ref, wn_ref, shift), preferred_element_type=F32)


def matmul_shifted(a, w, layer, col0, n, tn, tm=TM, name="proj"):
    t, k = a.shape
    main, nxt, shift = _shifted_specs(k, tn, layer, col0, lambda i, j: j)
    return pl.pallas_call(
        functools.partial(_mm_shift_kernel, shift=shift),
        grid=(t // tm, n // tn),
        in_specs=[pl.BlockSpec((tm, k), lambda i, j: (i, 0)), main, nxt],
        out_specs=pl.BlockSpec((tm, tn), lambda i, j: (i, j)),
        out_shape=jax.ShapeDtypeStruct((t, n), F32),
        compiler_params=_cparams(("parallel", "arbitrary")),
        name=name,
    )(a, w, w)


def _merge_kernel(xn_ref, br_r, br_s, br_g, wg0, wn0, wg1, wn1, wg2, wn2, wb_ref, o_ref, *, shift):
    xn = xn_ref[...]
    acc = None
    for i, (br, wg, wn) in enumerate(((br_r, wg0, wn0), (br_s, wg1, wn1), (br_g, wg2, wn2))):
        gate = _sigmoid(jnp.dot(xn, _shifted_cols(wg, wn, shift), preferred_element_type=F32))
        up = jnp.dot(br[...], wb_ref[i], preferred_element_type=F32)
        acc = gate * up if acc is None else acc + gate * up
    o_ref[...] = acc.astype(o_ref.dtype)


def merge_branches(xn, br_r, br_s, br_g, w_in, gate_col0, w_branch, layer, tn=512, tm=TM):
    t, d = xn.shape
    bw = br_r.shape[1]
    nj = d // tn
    a_spec = pl.BlockSpec((tm, bw), lambda i, j: (i, 0))
    wg_specs = []
    for n in range(N_BRANCH):
        main, nxt, shift = _shifted_specs(d, tn, 0, gate_col0, functools.partial(lambda i, j, n: n * nj + j, n=n))
        wg_specs += [main, nxt]
    return pl.pallas_call(
        functools.partial(_merge_kernel, shift=shift),
        grid=(t // tm, nj),
        in_specs=[pl.BlockSpec((tm, d), lambda i, j: (i, 0)), a_spec, a_spec, a_spec, *wg_specs,
                  pl.BlockSpec((None, N_BRANCH, bw, tn), lambda i, j: (layer, 0, 0, j))],
        out_specs=pl.BlockSpec((tm, tn), lambda i, j: (i, j)),
        out_shape=jax.ShapeDtypeStruct((t, d), BF16),
        compiler_params=_cparams(("parallel", "arbitrary")),
        name="merge",
    )(xn, br_r, br_s, br_g, *([w_in] * (2 * N_BRANCH)), w_branch)


def _outproj_kernel(m_ref, w_ref, x_ref, nw_ref, h_ref, hn_ref):
    h = x_ref[...] + jnp.dot(m_ref[...], w_ref[...], preferred_element_type=F32)
    h_ref[...] = h
    hn_ref[...] = (_rms_rows(h) * nw_ref[...]).astype(hn_ref.dtype)


def out_proj(merged, w_out, layer, x, norm_w, tm=TM):
    t, d = x.shape
    row = pl.BlockSpec((tm, d), lambda i: (i, 0))
    return pl.pallas_call(
        _outproj_kernel,
        grid=(t // tm,),
        in_specs=[row, pl.BlockSpec((None, d, d), lambda i: (layer, 0, 0)), row,
                  pl.BlockSpec((1, d), lambda i: (0, 0))],
        out_specs=[row, row],
        out_shape=[jax.ShapeDtypeStruct((t, d), F32), jax.ShapeDtypeStruct((t, d), BF16)],
        compiler_params=_cparams(("parallel",)),
        name="outproj",
    )(merged, w_out, x, norm_w.reshape(1, d))


def _ffn_kernel(hn_ref, h_ref, wg_ref, wu_ref, wd_ref, nw_ref, *rest, nf, n_cast):
    src, (y_ref, xn_ref), dst = rest[:n_cast], rest[n_cast:n_cast + 2], rest[n_cast + 2:]
    f = pl.program_id(1)

    @pl.when(f == 0)
    def _():
        y_ref[...] = h_ref[...]

    hn = hn_ref[...]
    g = jnp.dot(hn, wg_ref[...], preferred_element_type=F32)
    u = jnp.dot(hn, wu_ref[...], preferred_element_type=F32)
    act = (_silu(g) * u).astype(BF16)
    y_ref[...] += jnp.dot(act, wd_ref[...], preferred_element_type=F32)
    for s_ref, d_ref in zip(src, dst):
        d_ref[...] = s_ref[...].astype(d_ref.dtype)

    @pl.when(f == nf - 1)
    def _():
        xn_ref[...] = (_rms_rows(y_ref[...]) * nw_ref[...]).astype(xn_ref.dtype)


CAST_ROWS = 256
CAST_ROWS_IN = 16


def ffn(hn, h, w_gate, w_up, w_down, next_norm_w, xn_dtype, cast_next=None, tf=512, tm=TM):
    t, d = h.shape
    dff = w_gate.shape[1]
    nf = dff // tf
    ni = t // tm
    row = pl.BlockSpec((tm, d), lambda i, f: (i, 0))
    in_specs = [row, row, pl.BlockSpec((d, tf), lambda i, f: (0, f)), pl.BlockSpec((d, tf), lambda i, f: (0, f)),
                pl.BlockSpec((tf, d), lambda i, f: (f, 0)), pl.BlockSpec((1, d), lambda i, f: (0, 0))]
    out_specs = [row, row]
    out_shape = [jax.ShapeDtypeStruct((t, d), F32), jax.ShapeDtypeStruct((t, d), xn_dtype)]
    args = [hn, h, w_gate, w_up, w_down, next_norm_w.reshape(1, d)]
    n_cast = 0
    if cast_next is not None:
        nl, src_g, src_u, src_d, src_in = cast_next
        n_in = src_in.shape[2]
        nwb = (d // CAST_ROWS) * nf
        nib = d // CAST_ROWS_IN
        assert nwb <= ni * nf and nib <= ni * nf
        wb = lambda i, f: jnp.minimum(i * nf + f, nwb - 1)
        ib = lambda i, f: jnp.minimum(i * nf + f, nib - 1)
        in_specs += [pl.BlockSpec((None, CAST_ROWS, tf), lambda i, f: (nl, wb(i, f) // nf, wb(i, f) % nf)),
                     pl.BlockSpec((None, CAST_ROWS, tf), lambda i, f: (nl, wb(i, f) // nf, wb(i, f) % nf)),
                     pl.BlockSpec((None, tf, CAST_ROWS), lambda i, f: (nl, wb(i, f) % nf, wb(i, f) // nf)),
                     pl.BlockSpec((None, CAST_ROWS_IN, n_in), lambda i, f: (nl, ib(i, f), 0))]
        out_specs += [pl.BlockSpec((CAST_ROWS, tf), lambda i, f: (wb(i, f) // nf, wb(i, f) % nf)),
                      pl.BlockSpec((CAST_ROWS, tf), lambda i, f: (wb(i, f) // nf, wb(i, f) % nf)),
                      pl.BlockSpec((tf, CAST_ROWS), lambda i, f: (wb(i, f) % nf, wb(i, f) // nf)),
                      pl.BlockSpec((CAST_ROWS_IN, n_in), lambda i, f: (ib(i, f), 0))]
        out_shape += [jax.ShapeDtypeStruct((d, dff), BF16), jax.ShapeDtypeStruct((d, dff), BF16),
                      jax.ShapeDtypeStruct((dff, d), BF16), jax.ShapeDtypeStruct((d, n_in), BF16)]
        args += [src_g, src_u, src_d, src_in]
        n_cast = 4
    return pl.pallas_call(
        functools.partial(_ffn_kernel, nf=nf, n_cast=n_cast),
        grid=(ni, nf),
        in_specs=in_specs,
        out_specs=out_specs,
        out_shape=out_shape,
        compiler_params=_cparams(("arbitrary", "arbitrary")),
        name="ffn",
    )(*args)


def _glu_kernel(h_ref, w_ref, b_ref, o_ref):
    h = h_ref[...]
    z = jnp.dot(h.astype(BF16), w_ref[...], preferred_element_type=F32) + b_ref[...]
    o_ref[...] = (h * _sigmoid(z)).astype(o_ref.dtype)


def s5_glu(h, w_glu, layer, b_glu, tm=TM):
    t, d = h.shape
    row = pl.BlockSpec((tm, d), lambda i: (i, 0))
    return pl.pallas_call(
        _glu_kernel,
        grid=(t // tm,),
        in_specs=[row, pl.BlockSpec((None, d, d), lambda i: (layer, 0, 0)), pl.BlockSpec((1, d), lambda i: (0, 0))],
        out_specs=row,
        out_shape=jax.ShapeDtypeStruct((t, d), BF16),
        compiler_params=_cparams(("parallel",)),
        name="s5_glu",
    )(h, w_glu, b_glu.reshape(1, d))


def _rotary(x, cos, sin):
    half = x.shape[-1] // 2
    x1, x2 = x[:, :half], x[:, half:]
    return jnp.concatenate([x1 * cos - x2 * sin, x1 * sin + x2 * cos], axis=-1)


def _ret_prompt_kernel(lg_ref, q_ref, k_ref, v_ref, g_ref, cos_ref, sin_ref, o_ref, s_out_ref, s_ref, *, nc, chunk):
    c = pl.program_id(1)

    @pl.when(c == 0)
    def _():
        s_ref[...] = jnp.zeros_like(s_ref)

    cos, sin = cos_ref[...], sin_ref[...]
    ii = lax.broadcasted_iota(jnp.int32, (chunk, chunk), 0)
    jj = lax.broadcasted_iota(jnp.int32, (chunk, chunk), 1)
    diff = (ii - jj).astype(F32)
    idx = lax.broadcasted_iota(jnp.int32, (chunk, 1), 0).astype(F32)
    heads = range(R_HEADS)
    hs = lambda h: slice(h * R_DK, (h + 1) * R_DK)
    lg = [lg_ref[h] for h in heads]
    q_l = [_rotary(q_ref[:, hs(h)], cos, sin) for h in heads]
    k_l = [_rotary(k_ref[:, hs(h)], cos, sin) * (R_DK ** -0.5) for h in heads]
    v_l = [v_ref[:, hs(h)].astype(BF16) for h in heads]
    s_l = [s_ref[h] for h in heads]
    scores = [_dg(q_l[h].astype(BF16), k_l[h].astype(BF16), _NT)
              * jnp.where(diff >= 0, jnp.exp(lg[h] * jnp.maximum(diff, 0.0)), 0.0) for h in heads]
    cross = [_dg((q_l[h] * jnp.exp(lg[h] * (idx + 1.0))).astype(BF16), s_l[h].astype(BF16), _NN) for h in heads]
    for h in heads:
        o = _dg(scores[h].astype(BF16), v_l[h], _NN) + cross[h]
        o_ref[:, hs(h)] = (_silu(g_ref[:, hs(h)]) * _rms_rows(o)).astype(o_ref.dtype)
    for h in heads:
        k_dec = (k_l[h] * jnp.exp(lg[h] * (chunk - 1.0 - idx))).astype(BF16)
        s_ref[h] = jnp.exp(lg[h] * jnp.full((1, 1), float(chunk), F32)) * s_l[h] + _dg(k_dec, v_l[h], _TN)

    @pl.when(c == nc - 1)
    def _():
        s_out_ref[0] = s_ref[...]


def retention_prompt(p_ret, cos, sin, log_gamma, batch, seq, chunk=R_CHUNK):
    nc = seq // chunk
    bw = BRANCH_WIDTH
    rows = lambda off: pl.BlockSpec((chunk, bw), functools.partial(lambda b, c, off: (b * nc + c, off), off=off))
    tab = pl.BlockSpec((chunk, R_DK // 2), lambda b, c: (c, 0))
    return pl.pallas_call(
        functools.partial(_ret_prompt_kernel, nc=nc, chunk=chunk),
        grid=(batch, nc),
        in_specs=[pl.BlockSpec(memory_space=pltpu.SMEM), rows(0), rows(1), rows(2), rows(3), tab, tab],
        out_specs=[pl.BlockSpec((chunk, bw), lambda b, c: (b * nc + c, 0)),
                   pl.BlockSpec((1, R_HEADS, R_DK, R_DK), lambda b, c: (b, 0, 0, 0))],
        out_shape=[jax.ShapeDtypeStruct((p_ret.shape[0], bw), BF16),
                   jax.ShapeDtypeStruct((batch, R_HEADS, R_DK, R_DK), F32)],
        scratch_shapes=[pltpu.VMEM((R_HEADS, R_DK, R_DK), F32)],
        compiler_params=_cparams(("parallel", "arbitrary")),
        name="ret_prompt",
    )(log_gamma, p_ret, p_ret, p_ret, p_ret, cos, sin)


def _ret_step_kernel(lg_ref, q_ref, k_ref, v_ref, g_ref, cos_ref, sin_ref, s_ref, *rest, bb):
    o_ref, s_out_ref = rest[-2:]
    h = pl.program_id(0)
    gamma = jnp.exp(lg_ref[h] * jnp.ones((1, 1), F32))
    cos, sin = cos_ref[...], sin_ref[...]
    q = _rotary(q_ref[...], cos, sin)
    k = _rotary(k_ref[...], cos, sin) * (R_DK ** -0.5)
    v = v_ref[...]
    eye = _eye(R_DK)
    q_cols = _dot_sel(eye, q, _NT)
    k_cols = _dot_sel(eye, k, _NT)
    for i in range(bb):
        s_new = gamma * s_ref[i, 0] + k_cols[:, i:i + 1] * v[i:i + 1, :]
        s_out_ref[i, 0] = s_new
        o = jnp.sum(q_cols[:, i:i + 1] * s_new, axis=0, keepdims=True)
        o_ref[i:i + 1, :] = (_silu(g_ref[i:i + 1, :]) * _rms_rows(o)).astype(o_ref.dtype)


def _alias_args(first_index, outputs):
    args, specs, aliases = [], [], {}
    for out_idx, a in outputs.items():
        if a is not None:
            aliases[first_index + len(args)] = out_idx
            args.append(a)
            specs.append(pl.BlockSpec(memory_space=pl.ANY))
    return args, specs, aliases


def retention_step(p_ret, row0, cos, sin, log_gamma, states, layer, o_all, s_all, bb=16):
    nb = states.shape[1]
    r0 = row0 // bb
    rows = lambda off: pl.BlockSpec((bb, R_DK), functools.partial(lambda h, j, off: (r0 + j, off + h), off=off))
    tab = pl.BlockSpec((1, R_DK // 2), lambda h, j: (0, 0))
    st = pl.BlockSpec((None, bb, 1, R_DK, R_DK), lambda h, j: (layer, j, h, 0, 0))
    extra, extra_specs, aliases = _alias_args(8, {0: o_all, 1: s_all})
    return pl.pallas_call(
        functools.partial(_ret_step_kernel, bb=bb),
        grid=(R_HEADS, nb // bb),
        in_specs=[pl.BlockSpec(memory_space=pltpu.SMEM), rows(0), rows(R_HEADS), rows(2 * R_HEADS),
                  rows(3 * R_HEADS), tab, tab, st, *extra_specs],
        out_specs=[pl.BlockSpec((bb, R_DK), lambda h, j: (r0 + j, h)), st],
        out_shape=[jax.ShapeDtypeStruct(o_all.shape, BF16), jax.ShapeDtypeStruct(states.shape, F32)],
        input_output_aliases=aliases,
        compiler_params=_cparams(("parallel", "parallel")),
        name="ret_step",
    )(log_gamma, p_ret, p_ret, p_ret, p_ret, cos, sin, states, *extra)


def _cmul(ar, ai, br, bi):
    return ar * br - ai * bi, ar * bi + ai * br


def s5_discretize(lam_re, lam_im, log_dt, b_re, b_im):
    dt = jnp.exp(log_dt)[:, None]
    mag = jnp.exp(lam_re * dt)
    lbr, lbi = mag * jnp.cos(lam_im * dt), mag * jnp.sin(lam_im * dt)
    den = lam_re * lam_re + lam_im * lam_im
    nr, ni = lbr - 1.0, lbi
    cr, ci = (nr * lam_re + ni * lam_im) / den, (ni * lam_re - nr * lam_im) / den
    bbr, bbi = _cmul(cr[..., None], ci[..., None], b_re, b_im)
    return dt, lbr, lbi, bbr, bbi


def _lam_power(lam_re, lam_im, dt, n):
    n = jnp.asarray(n, F32).reshape((-1,) + (1,) * lam_re.ndim)
    mag = jnp.exp(lam_re * dt * n)
    return mag * jnp.cos(lam_im * dt * n), mag * jnp.sin(lam_im * dt * n)


def s5_prompt_params(lam_re, lam_im, log_dt, b_re, b_im, c_re, c_im, d_skip):
    g, p, q = S5_GROUPS, S5_STATE, S5_CHUNK
    dt, lbr, lbi, bbr, bbi = s5_discretize(lam_re, lam_im, log_dt, b_re, b_im)
    pr, pi = _lam_power(lam_re, lam_im, dt, jnp.arange(q + 1))
    lb_r, lb_i = _cmul(pr[:q, :, :, None], pi[:q, :, :, None], bbr[None], bbi[None])
    hp = lax.Precision.HIGHEST
    kern = (jnp.einsum('gkp,tgpc->tgkc', c_re, lb_r, precision=hp)
            - jnp.einsum('gkp,tgpc->tgkc', c_im, lb_i, precision=hp))
    krow = kern.transpose(1, 3, 0, 2).reshape(g, S5_GROUP, q * S5_GROUP)
    er, ei = _cmul(pr[:q][::-1][:, :, :, None], pi[:q][::-1][:, :, :, None], bbr[None], bbi[None])
    w_re = er.transpose(1, 0, 3, 2).reshape(g, q * S5_GROUP, p)
    w_im = ei.transpose(1, 0, 3, 2).reshape(g, q * S5_GROUP, p)
    vr, vi = _cmul(c_re[None], c_im[None], pr[1:, :, None, :], pi[1:, :, None, :])
    v_re = vr.transpose(1, 3, 0, 2).reshape(g, p, q * S5_GROUP)
    v_im = (-vi).transpose(1, 3, 0, 2).reshape(g, p, q * S5_GROUP)

    def pair_cols(a):
        z = jnp.zeros_like(a)
        even = (jnp.arange(g) % 2 == 0)[:, None, None]
        return jnp.concatenate([jnp.where(even, a, z), jnp.where(even, z, a)], axis=-1)

    def pair_rows(a):
        z = jnp.zeros_like(a)
        even = (jnp.arange(g) % 2 == 0)[:, None, None]
        return jnp.concatenate([jnp.where(even, a, z), jnp.where(even, z, a)], axis=1)

    sr, si = _lam_power(lam_re, lam_im, dt, q * (2.0 ** jnp.arange(8)))
    lam_sc_re = sr.reshape(8, g // 2, 2 * p).transpose(1, 0, 2)
    lam_sc_im = si.reshape(8, g // 2, 2 * p).transpose(1, 0, 2)
    d_vec = jnp.tile(d_skip.reshape(g, 1, S5_GROUP), (1, q, 1)).reshape(g, 1, q * S5_GROUP)
    bf = lambda a: a.astype(BF16)
    return (krow, bf(pair_cols(w_re)), bf(pair_cols(w_im)), bf(pair_rows(v_re)), bf(pair_rows(v_im)),
            lam_sc_re, lam_sc_im, d_vec)


S5_TILE_GROUPS = 128 // S5_GROUP


def _toeplitz(krow, q, gw):
    blocks = [krow] + [jnp.concatenate([jnp.zeros((gw, s * gw), F32), krow[:, :(q - s) * gw]], axis=1)
                       for s in range(1, q)]
    return jnp.concatenate(blocks, axis=0).astype(BF16)


def _s5_prompt_kernel(u_ref, krow_ref, wre_ref, wim_ref, vre_ref, vim_ref, lre_ref, lim_ref, d_ref,
                      h_ref, fre_ref, fim_ref, ut_ref, ht_ref, tile_ref, *, nchb, nb):
    rows = nb * nchb
    q, gw, ng = S5_CHUNK, S5_GROUP, S5_TILE_GROUPS
    for s in range(q):
        xs = u_ref[pl.ds(s, rows, stride=q), :]
        for gg in range(ng):
            ut_ref[gg, :, s * gw:(s + 1) * gw] = xs[:, gg * gw:(gg + 1) * gw]

    pos = jnp.bitwise_and(lax.broadcasted_iota(jnp.int32, (rows, 2 * S5_STATE), 0), nchb - 1)
    first = pos == 0
    pairs = range(ng // 2)
    ub_l = [(ut_ref[2 * pr].astype(BF16), ut_ref[2 * pr + 1].astype(BF16)) for pr in pairs]
    e_re = [_dg(ub_l[pr][0], wre_ref[2 * pr], _NN) + _dg(ub_l[pr][1], wre_ref[2 * pr + 1], _NN) for pr in pairs]
    e_im = [_dg(ub_l[pr][0], wim_ref[2 * pr], _NN) + _dg(ub_l[pr][1], wim_ref[2 * pr + 1], _NN) for pr in pairs]
    d, k = 1, 0
    while d < nchb:
        keep = pos >= d
        for pr in pairs:
            lr, li = lre_ref[pr, k:k + 1, :], lim_ref[pr, k:k + 1, :]
            sr, si = pltpu.roll(e_re[pr], d, 0), pltpu.roll(e_im[pr], d, 0)
            e_re[pr], e_im[pr] = (e_re[pr] + jnp.where(keep, lr * sr - li * si, 0.0),
                                  e_im[pr] + jnp.where(keep, lr * si + li * sr, 0.0))
        d, k = d * 2, k + 1
    for pr in pairs:
        for b in range(nb):
            r = (b + 1) * nchb - 1
            fre_ref[pr, b:b + 1, :] = e_re[pr][r:r + 1, :]
            fim_ref[pr, b:b + 1, :] = e_im[pr][r:r + 1, :]
    for pr in pairs:
        xc_re = jnp.where(first, 0.0, pltpu.roll(e_re[pr], 1, 0)).astype(BF16)
        xc_im = jnp.where(first, 0.0, pltpu.roll(e_im[pr], 1, 0)).astype(BF16)
        for gi in range(2):
            gg = 2 * pr + gi
            y = (_dg(ub_l[pr][gi], _toeplitz(krow_ref[gg], q, gw), _NN)
                 + _dg(xc_re, vre_ref[gg], _NN) + _dg(xc_im, vim_ref[gg], _NN)
                 ) + ut_ref[gg] * d_ref[gg]
            ht_ref[gg] = _gelu_tanh(y)
    for s in range(q):
        for gg in range(ng):
            tile_ref[:, gg * gw:(gg + 1) * gw] = ht_ref[gg, :, s * gw:(s + 1) * gw]
        h_ref[pl.ds(s, rows, stride=q), :] = tile_ref[...]


def s5_prompt(p_s5, params, layer, batch, seq):
    krow, w_re, w_im, v_re, v_im, l_re, l_im, d_vec = params
    ng, qc, p2 = S5_TILE_GROUPS, S5_CHUNK * S5_GROUP, 2 * S5_STATE
    nchb = seq // S5_CHUNK
    rows = batch * nchb
    tp = batch * seq
    nt = BRANCH_WIDTH // 128
    grp = lambda n, *tail: pl.BlockSpec((None, n) + tail, lambda j: (layer, j) + (0,) * len(tail))
    fin = pl.BlockSpec((ng // 2, batch, p2), lambda j: (j, 0, 0))
    return pl.pallas_call(
        functools.partial(_s5_prompt_kernel, nchb=nchb, nb=batch),
        grid=(nt,),
        in_specs=[pl.BlockSpec((tp, 128), lambda j: (0, j)), grp(ng, S5_GROUP, qc), grp(ng, qc, p2), grp(ng, qc, p2),
                  grp(ng, p2, qc), grp(ng, p2, qc), grp(ng // 2, 8, p2), grp(ng // 2, 8, p2), grp(ng, 1, qc)],
        out_specs=[pl.BlockSpec((tp, 128), lambda j: (0, j)), fin, fin],
        out_shape=[jax.ShapeDtypeStruct(p_s5.shape, F32),
                   jax.ShapeDtypeStruct((S5_GROUPS // 2, batch, p2), F32),
                   jax.ShapeDtypeStruct((S5_GROUPS // 2, batch, p2), F32)],
        scratch_shapes=[pltpu.VMEM((ng, rows, qc), F32), pltpu.VMEM((ng, rows, qc), F32),
                        pltpu.VMEM((rows, 128), F32)],
        compiler_params=_cparams(("parallel",)),
        name="s5_prompt",
    )(p_s5, krow, w_re, w_im, v_re, v_im, l_re, l_im, d_vec)


def s5_step_params(lam_re, lam_im, log_dt, b_re, b_im, c_re, c_im, d_skip):
    g, p = S5_GROUPS, S5_STATE
    _, lbr, lbi, bbr, bbi = s5_discretize(lam_re, lam_im, log_dt, b_re, b_im)
    eye8 = jnp.eye(8, dtype=F32)

    def bd_in(bb):
        x = bb.reshape(8, 8, p, S5_GROUP)
        return jnp.einsum('jgpc,gh->jgchp', x, eye8).reshape(8, 8 * S5_GROUP, 8 * p)

    def bd_out(cc):
        x = cc.reshape(8, 8, S5_GROUP, p)
        return jnp.einsum('jgkp,gh->jgphk', x, eye8).reshape(8, 8 * p, 8 * S5_GROUP)

    return (bd_in(bbr), bd_in(bbi), bd_out(c_re), bd_out(-c_im), lbr.reshape(1, g * p), lbi.reshape(1, g * p),
            d_skip.reshape(1, g * S5_GROUP))


def _s5_step_kernel(u_ref, xr_ref, xi_ref, bre_ref, bim_ref, cre_ref, cim_ref, lr_ref, li_ref, d_ref, *rest):
    h_ref, nr_ref, ni_ref = rest[-3:]
    u = u_ref[...]
    lr, li = lr_ref[...], li_ref[...]
    xr, xi = xr_ref[...], xi_ref[...]
    nr = lr * xr - li * xi + _dot3(u, bre_ref[0])
    ni = lr * xi + li * xr + _dot3(u, bim_ref[0])
    nr_ref[...] = nr
    ni_ref[...] = ni
    y = _dot3(nr, cre_ref[0]) + _dot3(ni, cim_ref[0]) + u * d_ref[...]
    h_ref[...] = _gelu_tanh(y)


def s5_step(p_s5, row0, x_re, x_im, params, layer, h_all):
    b_re, b_im, c_re, c_im, l_re, l_im, d_vec = params
    nb, gp = x_re.shape
    lt = 8 * S5_GROUP
    st = 8 * S5_STATE
    r0 = row0 // nb
    u_spec = pl.BlockSpec((nb, lt), lambda j: (r0, j))
    x_spec = pl.BlockSpec((nb, st), lambda j: (0, j))
    mat = lambda a, b: pl.BlockSpec((None, 1, a, b), lambda j: (layer, j, 0, 0))
    vec = lambda n: pl.BlockSpec((None, 1, n), lambda j: (layer, 0, j))
    extra, extra_specs, aliases = _alias_args(10, {0: h_all})
    return pl.pallas_call(
        _s5_step_kernel,
        grid=(gp // st,),
        in_specs=[u_spec, x_spec, x_spec, mat(lt, st), mat(lt, st), mat(st, lt), mat(st, lt),
                  vec(st), vec(st), vec(lt), *extra_specs],
        out_specs=[u_spec, x_spec, x_spec],
        out_shape=[jax.ShapeDtypeStruct(h_all.shape, F32), jax.ShapeDtypeStruct((nb, gp), F32),
                   jax.ShapeDtypeStruct((nb, gp), F32)],
        input_output_aliases=aliases,
        compiler_params=_cparams(("parallel",)),
        name="s5_step",
    )(p_s5, x_re, x_im, b_re, b_im, c_re, c_im, l_re, l_im, d_vec, *extra)


def _l2norm_rows(x):
    return x * lax.rsqrt(jnp.sum(x * x, axis=-1, keepdims=True) + NORM_EPS)


def _gdn_prompt_kernel(qkv_ref, z_ref, ab_ref, cw_ref, alog_ref, dtb_ref, nw_ref, o_ref, s_out_ref,
                       xbuf_ref, s_ref, m_s, sol_s, qe_s, qkd_s, kdec_s, egl_s, *, nc, chunk):
    c = pl.program_id(1)
    bw = BRANCH_WIDTH
    halo = 8
    heads = range(G_HEADS)

    @pl.when(c == 0)
    def _():
        xbuf_ref[0:halo, :] = jnp.zeros((halo, 3 * bw), F32)
        s_ref[...] = jnp.zeros_like(s_ref)
        for ref in (m_s, sol_s, qe_s, qkd_s, kdec_s, egl_s):
            ref[...] = jnp.zeros_like(ref)

    m_l = [m_s[h] for h in heads]
    sol_l = [sol_s[h] for h in heads]
    qe_l = [qe_s[h] for h in heads]
    qkd_l = [qkd_s[h] for h in heads]
    kdec_l = [kdec_s[h] for h in heads]
    egl_l = [egl_s[h] for h in heads]

    x = qkv_ref[...]
    xbuf_ref[halo:halo + chunk, :] = x
    conv = x * cw_ref[G_CONV - 1:G_CONV, :]
    for i in range(1, G_CONV):
        conv = conv + xbuf_ref[halo - i:halo - i + chunk, :] * cw_ref[G_CONV - 1 - i:G_CONV - i, :]
    xbuf_ref[0:halo, :] = x[chunk - halo:chunk, :]
    qkv = _silu(conv)

    ab = ab_ref[...]
    g_all = -jnp.exp(alog_ref[...]) * _softplus(ab + dtb_ref[...])
    beta_all = _sigmoid(ab)
    ri = lax.broadcasted_iota(jnp.int32, (chunk, chunk), 0)
    ci = lax.broadcasted_iota(jnp.int32, (chunk, chunk), 1)
    causal = ri >= ci
    strict = ri > ci
    ltri = jnp.where(causal, 1.0, 0.0).astype(BF16)
    gcum = _dot_sel(ltri, g_all)
    gcum_t = _dot_sel(_eye(128), gcum, _NT)
    z = z_ref[...]
    nw = nw_ref[...]

    def prepare(h):
        q = _l2norm_rows(qkv[:, h * G_DK:(h + 1) * G_DK]) * (G_DK ** -0.5)
        k = _l2norm_rows(qkv[:, bw + h * G_DK:bw + (h + 1) * G_DK])
        v = qkv[:, 2 * bw + h * G_DK:2 * bw + (h + 1) * G_DK]
        gc = gcum[:, h:h + 1]
        gr = gcum_t[h:h + 1, :]
        beta = beta_all[:, G_HEADS + h:G_HEADS + h + 1]
        decay = jnp.where(causal, jnp.exp(jnp.where(causal, gc - gr, 0.0)), 0.0)
        kb = k.astype(BF16)
        qk_kk = _dg(jnp.concatenate([kb, q.astype(BF16)], axis=0), kb, _NT)
        eg = jnp.exp(gc)
        g_last = gc[chunk - 1:chunk, :]
        qe_s[h] = (q * eg).astype(BF16)
        qkd_s[h] = (qk_kk[chunk:] * decay).astype(BF16)
        kdec_s[h] = (k * jnp.exp(g_last - gc)).astype(BF16)
        egl_s[h] = jnp.broadcast_to(jnp.exp(g_last), (1, G_DK))
        m_s[h] = jnp.where(strict, -(beta * qk_kk[:chunk] * decay), 0.0)
        sol_s[h] = jnp.concatenate([beta * v, (beta * eg) * k], axis=-1)

    todo = list(heads)
    span = 1
    while span < chunk:
        for h in heads:
            mb = m_l[h].astype(BF16)
            if 2 * span < chunk:
                sol_hi, sol_lo = _split2(sol_l[h])
                r = _dg(mb, jnp.concatenate([sol_hi, mb], axis=-1), _NN)
                sol_l[h] = sol_l[h] + (r[:, :2 * G_DK] + _dg(mb, sol_lo, _NN))
                m_l[h] = r[:, 2 * G_DK:]
            else:
                sol_l[h] = sol_l[h] + _dot2(mb, sol_l[h])
        prepare(todo.pop(0))
        span *= 2
    s_l = [s_ref[h] for h in heads]
    ks_qs_l = [_dg(jnp.concatenate([sol_l[h][:, G_DK:].astype(BF16), qe_l[h]], axis=0), s_l[h].astype(BF16), _NN)
               for h in heads]
    wb_l = [(sol_l[h][:, :G_DK] - ks_qs_l[h][:chunk]).astype(BF16) for h in heads]
    while todo:
        prepare(todo.pop(0))
    for h in heads:
        o = ks_qs_l[h][chunk:] + _dg(qkd_l[h], wb_l[h], _NN)
        zh = z[:, h * G_DK:(h + 1) * G_DK]
        o_ref[:, h * G_DK:(h + 1) * G_DK] = (_rms_rows(o) * nw * _silu(zh)).astype(o_ref.dtype)
    for h in heads:
        s_ref[h] = egl_l[h] * s_l[h] + _dg(kdec_l[h], wb_l[h], _TN)

    @pl.when(c == nc)
    def _():
        s_out_ref[0] = s_ref[...]


def gdn_prompt(p_qkv, p_z, p_ab, conv_w, a_log, dt_bias, norm_w, batch, seq, chunk=G_CHUNK):
    nc = seq // chunk
    bw = BRANCH_WIDTH
    pad = lambda x: jnp.zeros((1, 128), F32).at[0, :G_HEADS].set(x)
    full = lambda shape: pl.BlockSpec(shape, lambda b, c: (0,) * len(shape))
    cur = lambda width: pl.BlockSpec((chunk, width), lambda b, c: (b * nc + jnp.minimum(c, nc - 1), 0))
    prev = lambda width: pl.BlockSpec((chunk, width), lambda b, c: (b * nc + jnp.maximum(c - 1, 0), 0))
    hd = lambda width, dtype: pltpu.VMEM((G_HEADS, chunk, width), dtype)
    return pl.pallas_call(
        functools.partial(_gdn_prompt_kernel, nc=nc, chunk=chunk),
        grid=(batch, nc + 1),
        in_specs=[cur(3 * bw), prev(bw), cur(128),
                  full((G_CONV, 3 * bw)), full((1, 128)), full((1, 128)), full((1, G_DK))],
        out_specs=[prev(bw), pl.BlockSpec((1, G_HEADS, G_DK, G_DK), lambda b, c: (b, 0, 0, 0))],
        out_shape=[jax.ShapeDtypeStruct((p_qkv.shape[0], bw), BF16),
                   jax.ShapeDtypeStruct((batch, G_HEADS, G_DK, G_DK), F32)],
        scratch_shapes=[pltpu.VMEM((8 + chunk, 3 * bw), F32), pltpu.VMEM((G_HEADS, G_DK, G_DK), F32),
                        hd(chunk, F32), hd(2 * G_DK, F32), hd(G_DK, BF16), hd(chunk, BF16), hd(G_DK, BF16),
                        pltpu.VMEM((G_HEADS, 1, G_DK), F32)],
        compiler_params=_cparams(("parallel", "arbitrary")),
        name="gdn_prompt",
    )(p_qkv, p_z, p_ab, conv_w, pad(a_log), pad(dt_bias), norm_w.reshape(1, G_DK))


def _gdn_step_kernel(qkv_ref, z_ref, ab_ref, cbuf_ref, cw_ref, alog_ref, dtb_ref, nw_ref, s_ref, *rest, bb):
    o_ref, cbuf_out_ref, s_out_ref = rest[-3:]
    bw = BRANCH_WIDTH
    x = qkv_ref[...]
    conv = x * cw_ref[G_CONV - 1:G_CONV, :]
    for i in range(G_CONV - 1):
        conv = conv + cbuf_ref[i] * cw_ref[i:i + 1, :]
    for i in range(G_CONV - 2):
        cbuf_out_ref[i] = cbuf_ref[i + 1]
    cbuf_out_ref[G_CONV - 2] = x
    qkv = _silu(conv)
    ab = ab_ref[...]
    g_all = -jnp.exp(alog_ref[...]) * _softplus(ab + dtb_ref[...])
    beta_all = _sigmoid(ab)
    eye = _eye(G_DK)
    z = z_ref[...]
    nw = nw_ref[...]
    for h in range(G_HEADS):
        q = _l2norm_rows(qkv[:, h * G_DK:(h + 1) * G_DK]) * (G_DK ** -0.5)
        k = _l2norm_rows(qkv[:, bw + h * G_DK:bw + (h + 1) * G_DK])
        v = qkv[:, 2 * bw + h * G_DK:2 * bw + (h + 1) * G_DK]
        eg = jnp.exp(g_all[:, h:h + 1])
        beta = beta_all[:, G_HEADS + h:G_HEADS + h + 1]
        q_cols = _dot_sel(eye, q, _NT)
        k_cols = _dot_sel(eye, k, _NT)
        rng = range(bb)
        s_l = [s_ref[i, h] for i in rng]
        ks_l = [jnp.sum(k_cols[:, i:i + 1] * s_l[i], axis=0, keepdims=True) for i in rng]
        w_l = [beta[i:i + 1, :] * (v[i:i + 1, :] - eg[i:i + 1, :] * ks_l[i]) for i in rng]
        sn_l = [eg[i:i + 1, :] * s_l[i] + k_cols[:, i:i + 1] * w_l[i] for i in rng]
        for i in rng:
            s_out_ref[i, h] = sn_l[i]
        o = jnp.concatenate([jnp.sum(q_cols[:, i:i + 1] * sn_l[i], axis=0, keepdims=True) for i in rng], axis=0)
        zh = z[:, h * G_DK:(h + 1) * G_DK]
        o_ref[:, h * G_DK:(h + 1) * G_DK] = (_rms_rows(o) * nw * _silu(zh)).astype(o_ref.dtype)


def gdn_step(p_qkv, p_z, p_ab, row0, conv_buf_t, states, layer, conv_w, a_log, dt_bias, norm_w, o_all, s_all, bb=8):
    nb = states.shape[1]
    bw = BRANCH_WIDTH
    r0 = row0 // bb
    pad = lambda x: jnp.zeros((1, 128), F32).at[0, :G_HEADS].set(x)
    full = lambda shape: pl.BlockSpec(shape, lambda j: (0,) * len(shape))
    cb = pl.BlockSpec((G_CONV - 1, bb, 3 * bw), lambda j: (0, j, 0))
    st = pl.BlockSpec((None, bb, G_HEADS, G_DK, G_DK), lambda j: (layer, j, 0, 0, 0))
    extra, extra_specs, aliases = _alias_args(9, {0: o_all, 2: s_all})
    return pl.pallas_call(
        functools.partial(_gdn_step_kernel, bb=bb),
        grid=(nb // bb,),
        in_specs=[pl.BlockSpec((bb, 3 * bw), lambda j: (r0 + j, 0)),
                  pl.BlockSpec((bb, bw), lambda j: (r0 + j, 0)),
                  pl.BlockSpec((bb, 128), lambda j: (r0 + j, 0)),
                  cb, full((G_CONV, 3 * bw)), full((1, 128)), full((1, 128)), full((1, G_DK)), st, *extra_specs],
        out_specs=[pl.BlockSpec((bb, bw), lambda j: (r0 + j, 0)), cb, st],
        out_shape=[jax.ShapeDtypeStruct(o_all.shape, BF16), jax.ShapeDtypeStruct(conv_buf_t.shape, F32),
                   jax.ShapeDtypeStruct(states.shape, F32)],
        input_output_aliases=aliases,
        compiler_params=_cparams(("parallel",)),
        name="gdn_step",
    )(p_qkv, p_z, p_ab, conv_buf_t, conv_w, pad(a_log), pad(dt_bias), norm_w.reshape(1, G_DK), states, *extra)


def _rope_tables(pos):
    half = R_DK // 2
    inv = ROPE_BASE ** (-jnp.arange(half, dtype=F32) / half)
    ang = pos.astype(F32)[:, None] * inv[None, :]
    return jnp.cos(ang), jnp.sin(ang)


def kernel(x_prompt, x_sample, state_ret, state_s5_re, state_s5_im, state_gdn, state_gdn_conv, norm_mix_w, w_in, s5_lambda_re, s5_lambda_im, s5_log_dt, s5_b_re, s5_b_im, s5_c_re, s5_c_im, s5_d, s5_w_glu, s5_b_glu, gdn_conv_w, gdn_a_log, gdn_dt_bias, gdn_norm_w, w_branch, w_out, norm_ffn_w, w_gate_ffn, w_up_ffn, w_down_ffn, norm_final_w):
    bp, lp, d = x_prompt.shape
    nb = x_sample.shape[0]
    depth = w_in.shape[0]
    tp = bp * lp
    bw = BRANCH_WIDTH
    x = jnp.concatenate([x_prompt.reshape(tp, d), x_sample.reshape(nb, d)], axis=0)

    cos_p, sin_p = _rope_tables(jnp.arange(lp, dtype=jnp.int32))
    cos_s, sin_s = _rope_tables(PAST_LEN + jnp.arange(1, dtype=jnp.int32))
    log_gamma = jnp.log1p(-jnp.exp2(-5.0 - jnp.arange(R_HEADS, dtype=F32)))

    o_su, o_gqkv, o_ga, o_gz, o_gates = 4 * bw, 5 * bw, 8 * bw, 8 * bw + 2 * G_HEADS, 9 * bw + 2 * G_HEADS
    w_in_b = w_in[0].astype(BF16)[None]
    w_gate_b, w_up_b, w_down_b = w_gate_ffn[0].astype(BF16), w_up_ffn[0].astype(BF16), w_down_ffn[0].astype(BF16)
    w_branch_b, w_out_b, w_glu_b = w_branch.astype(BF16), w_out.astype(BF16), s5_w_glu.astype(BF16)
    s5_raw = (s5_lambda_re, s5_lambda_im, s5_log_dt, s5_b_re, s5_b_im, s5_c_re, s5_c_im, s5_d)
    s5_pp = jax.vmap(s5_prompt_params)(*s5_raw)
    s5_sp = jax.vmap(s5_step_params)(*s5_raw)

    new_p = [[], [], [], [], []]
    new_s = [[], [], []]
    ret_s_all = gdn_s_all = None
    xn = rmsnorm_rows(x, norm_mix_w[0], BF16)
    y_final = None
    for l in range(depth):
        p_ret = matmul(xn, w_in_b, 0, 0, o_su, tn=1024, name="proj_ret")
        p_s5 = matmul(xn, w_in_b, 0, o_su, bw, tn=1024, name="proj_s5")
        p_qkv = matmul(xn, w_in_b, 0, o_gqkv, 3 * bw, tn=1024, name="proj_qkv")
        p_ab = matmul(xn, w_in_b, 0, o_ga, LANE, tn=LANE, name="proj_ab")
        p_z = matmul_shifted(xn, w_in_b, 0, o_gz, bw, tn=512, name="proj_z")

        r_all, ret_p = retention_prompt(p_ret, cos_p, sin_p, log_gamma, bp, lp)
        r_all, ret_s_all = retention_step(p_ret, tp, cos_s, sin_s, log_gamma, state_ret, l, r_all, ret_s_all)

        h_all, f_re, f_im = s5_prompt(p_s5, s5_pp, l, bp, lp)
        fin = lambda f: f[:, :bp, :].reshape(S5_GROUPS // 2, bp, 2, S5_STATE).transpose(1, 0, 2, 3).reshape(
            bp, S5_GROUPS, S5_STATE)
        h_all, s5re_s, s5im_s = s5_step(p_s5, tp, state_s5_re[l].reshape(nb, -1), state_s5_im[l].reshape(nb, -1),
                                        s5_sp, l, h_all)
        s_all = s5_glu(h_all, w_glu_b, l, s5_b_glu[l])

        g_all, gdn_p = gdn_prompt(p_qkv, p_z, p_ab, gdn_conv_w[l], gdn_a_log[l], gdn_dt_bias[l], gdn_norm_w[l], bp, lp)
        g_all, conv_s_t, gdn_s_all = gdn_step(p_qkv, p_z, p_ab, tp, state_gdn_conv[l].transpose(1, 0, 2), state_gdn, l,
                                              gdn_conv_w[l], gdn_a_log[l], gdn_dt_bias[l], gdn_norm_w[l],
                                              g_all, gdn_s_all)
        conv_p = jnp.stack([p_qkv[(b + 1) * lp - (G_CONV - 1):(b + 1) * lp] for b in range(bp)], axis=0)

        merged = merge_branches(xn, r_all, s_all, g_all, w_in_b, o_gates, w_branch_b, l)
        h, hn = out_proj(merged, w_out_b, l, x, norm_ffn_w[l])
        last = l == depth - 1
        if last:
            x, y_final = ffn(hn, h, w_gate_b, w_up_b, w_down_b, norm_final_w, F32)
        else:
            x, xn, w_gate_b, w_up_b, w_down_b, w_in_next = ffn(
                hn, h, w_gate_b, w_up_b, w_down_b, norm_mix_w[l + 1], BF16,
                cast_next=(l + 1, w_gate_ffn, w_up_ffn, w_down_ffn, w_in))
            w_in_b = w_in_next[None]

        for lst, val in zip(new_p, (ret_p, fin(f_re), fin(f_im), gdn_p, conv_p)):
            lst.append(val)
        for lst, val in zip(new_s, (s5re_s.reshape(nb, S5_GROUPS, S5_STATE),
                                    s5im_s.reshape(nb, S5_GROUPS, S5_STATE), conv_s_t.transpose(1, 0, 2))):
            lst.append(val)

    y_prompt = y_final[:tp].reshape(bp, lp, d)
    y_sample = y_final[tp:].reshape(nb, 1, d)
    outs_p = [jnp.stack(a, 0) for a in new_p]
    s5re_s, s5im_s, conv_s = [jnp.stack(a, 0) for a in new_s]
    return (y_prompt, y_sample, *outs_p, ret_s_all, s5re_s, s5im_s, gdn_s_all, conv_s)
```

```python
import functools
import math

import jax
import jax.numpy as jnp
from jax import lax
from jax.experimental import pallas as pl
from jax.experimental.pallas import tpu as pltpu

F32 = jnp.float32
BF16 = jnp.bfloat16

D_MODEL = 2048
BRANCH_WIDTH = 1024
N_BRANCH = 3
R_HEADS = 4
R_DK = 256
R_CHUNK = 128
ROPE_BASE = 10000.0
S5_GROUP = 16
S5_GROUPS = 64
S5_STATE = 64
S5_CHUNK = 16
G_HEADS = 8
G_DK = 128
G_CONV = 4
G_CHUNK = 64
NORM_EPS = 1e-6
PAST_LEN = 16384

VMEM_LIMIT_BYTES = 56 * 1024 * 1024
TM = 640


def _cparams(sem):
    return pltpu.CompilerParams(dimension_semantics=sem, vmem_limit_bytes=VMEM_LIMIT_BYTES)


_NN = (((1,), (0,)), ((), ()))
_NT = (((1,), (1,)), ((), ()))
_TN = (((0,), (0,)), ((), ()))


def _dg(a, b, dims):
    return lax.dot_general(a, b, dims, preferred_element_type=F32)


def _split2(a):
    hi = a.astype(BF16)
    lo = (a - hi.astype(F32)).astype(BF16)
    return hi, lo


def _dot3(a, b, dims=_NN):
    ah, al = _split2(a)
    bh, bl = _split2(b)
    return _dg(ah, bh, dims) + (_dg(ah, bl, dims) + _dg(al, bh, dims))


def _dot2(a_bf16, b, dims=_NN):
    bh, bl = _split2(b)
    return _dg(a_bf16, bh, dims) + _dg(a_bf16, bl, dims)


def _dot_sel(sel_bf16, x, dims=_NN):
    h1 = x.astype(BF16)
    r1 = x - h1.astype(F32)
    h2 = r1.astype(BF16)
    h3 = (r1 - h2.astype(F32)).astype(BF16)
    return _dg(sel_bf16, h1, dims) + (_dg(sel_bf16, h2, dims) + _dg(sel_bf16, h3, dims))


def _sel_dot(x, sel_bf16, dims=_NN):
    h1 = x.astype(BF16)
    r1 = x - h1.astype(F32)
    h2 = r1.astype(BF16)
    h3 = (r1 - h2.astype(F32)).astype(BF16)
    return _dg(h1, sel_bf16, dims) + (_dg(h2, sel_bf16, dims) + _dg(h3, sel_bf16, dims))


def _eye(n, dtype=BF16):
    r = lax.broadcasted_iota(jnp.int32, (n, n), 0)
    c = lax.broadcasted_iota(jnp.int32, (n, n), 1)
    return jnp.where(r == c, 1.0, 0.0).astype(dtype)


def _sigmoid(x):
    return 1.0 / (1.0 + jnp.exp(-x))


def _silu(x):
    return x * _sigmoid(x)


def _gelu_tanh(x):
    return 0.5 * x * (1.0 + jnp.tanh(math.sqrt(2.0 / math.pi) * (x + 0.044715 * (x * x * x))))


def _softplus(x):
    return jnp.maximum(x, 0.0) + jnp.log1p(jnp.exp(-jnp.abs(x)))


def _rms_rows(x, eps=NORM_EPS):
    return x * lax.rsqrt(jnp.mean(x * x, axis=-1, keepdims=True) + eps)


def _norm_kernel(x_ref, w_ref, o_ref):
    o_ref[...] = (_rms_rows(x_ref[...]) * w_ref[...]).astype(o_ref.dtype)


def rmsnorm_rows(x, w, out_dtype, tm=TM):
    t, d = x.shape
    return pl.pallas_call(
        _norm_kernel,
        grid=(t // tm,),
        in_specs=[pl.BlockSpec((tm, d), lambda i: (i, 0)), pl.BlockSpec((1, d), lambda i: (0, 0))],
        out_specs=pl.BlockSpec((tm, d), lambda i: (i, 0)),
        out_shape=jax.ShapeDtypeStruct((t, d), out_dtype),
        compiler_params=_cparams(("parallel",)),
        name="rmsnorm",
    )(x, w.reshape(1, d))


LANE = 128
ROW_ALIGN = 16


def _shifted_rows(w_ref, wn_ref, shift):
    if shift == 0:
        return w_ref[...]
    return jnp.concatenate([w_ref[shift:, :], wn_ref[:shift, :]], axis=0)


def _shifted_specs(k, tn, layer, row0, nidx):
    base, shift = (row0 // tn) * tn, row0 % tn
    assert shift % ROW_ALIGN == 0 and tn % max(shift, ROW_ALIGN) == 0
    rows = max(shift, ROW_ALIGN)
    main = pl.BlockSpec((None, tn, k), lambda i, j: (layer, base // tn + nidx(i, j), 0))
    nxt = pl.BlockSpec((None, rows, k), lambda i, j: (layer, (base + (nidx(i, j) + 1) * tn) // rows, 0))
    return main, nxt, shift


def _mm_kernel(a_ref, w_ref, wn_ref, o_ref, *, shift):
    o_ref[...] = _dg(a_ref[...], _shifted_rows(w_ref, wn_ref, shift), _NT)


def matmul(a, wt, layer, row0, n, tn, tm=TM, name="proj"):
    t, k = a.shape
    main, nxt, shift = _shifted_specs(k, tn, layer, row0, lambda i, j: j)
    return pl.pallas_call(
        functools.partial(_mm_kernel, shift=shift),
        grid=(t // tm, n // tn),
        in_specs=[pl.BlockSpec((tm, k), lambda i, j: (i, 0)), main, nxt],
        out_specs=pl.BlockSpec((tm, tn), lambda i, j: (i, j)),
        out_shape=jax.ShapeDtypeStruct((t, n), F32),
        compiler_params=_cparams(("parallel", "arbitrary")),
        name=name,
    )(a, wt, wt)


def _merge_kernel(xn_ref, br_r, br_s, br_g, wg0, wn0, wg1, wn1, wg2, wn2, wb_ref, o_ref, *, shift):
    xn = xn_ref[...]
    acc = None
    for i, (br, wg, wn) in enumerate(((br_r, wg0, wn0), (br_s, wg1, wn1), (br_g, wg2, wn2))):
        gate = _sigmoid(_dg(xn, _shifted_rows(wg, wn, shift), _NT))
        up = jnp.dot(br[...], wb_ref[i], preferred_element_type=F32)
        acc = gate * up if acc is None else acc + gate * up
    o_ref[...] = acc.astype(o_ref.dtype)


def merge_branches(xn, br_r, br_s, br_g, w_in, gate_col0, w_branch, layer, tn=512, tm=TM):
    t, d = xn.shape
    bw = br_r.shape[1]
    nj = d // tn
    a_spec = pl.BlockSpec((tm, bw), lambda i, j: (i, 0))
    wg_specs = []
    for n in range(N_BRANCH):
        main, nxt, shift = _shifted_specs(d, tn, 0, gate_col0, functools.partial(lambda i, j, n: n * nj + j, n=n))
        wg_specs += [main, nxt]
    return pl.pallas_call(
        functools.partial(_merge_kernel, shift=shift),
        grid=(t // tm, nj),
        in_specs=[pl.BlockSpec((tm, d), lambda i, j: (i, 0)), a_spec, a_spec, a_spec, *wg_specs,
                  pl.BlockSpec((None, N_BRANCH, bw, tn), lambda i, j: (layer, 0, 0, j))],
        out_specs=pl.BlockSpec((tm, tn), lambda i, j: (i, j)),
        out_shape=jax.ShapeDtypeStruct((t, d), BF16),
        compiler_params=_cparams(("parallel", "arbitrary")),
        name="merge",
    )(xn, br_r, br_s, br_g, *([w_in] * (2 * N_BRANCH)), w_branch)


def _outproj_kernel(m_ref, w_ref, x_ref, nw_ref, h_ref, hn_ref):
    h = x_ref[...] + jnp.dot(m_ref[...], w_ref[...], preferred_element_type=F32)
    h_ref[...] = h
    hn_ref[...] = (_rms_rows(h) * nw_ref[...]).astype(hn_ref.dtype)


def out_proj(merged, w_out, layer, x, norm_w, tm=TM):
    t, d = x.shape
    row = pl.BlockSpec((tm, d), lambda i: (i, 0))
    return pl.pallas_call(
        _outproj_kernel,
        grid=(t // tm,),
        in_specs=[row, pl.BlockSpec((None, d, d), lambda i: (layer, 0, 0)), row,
                  pl.BlockSpec((1, d), lambda i: (0, 0))],
        out_specs=[row, row],
        out_shape=[jax.ShapeDtypeStruct((t, d), F32), jax.ShapeDtypeStruct((t, d), BF16)],
        compiler_params=_cparams(("parallel",)),
        name="outproj",
    )(merged, w_out, x, norm_w.reshape(1, d))


def _ffn_kernel(hn_ref, h_ref, wg_ref, wu_ref, wd_ref, nw_ref, *rest, nf, n_cast):
    src, (y_ref, xn_ref), dst = rest[:n_cast], rest[n_cast:n_cast + 2], rest[n_cast + 2:]
    f = pl.program_id(1)

    @pl.when(f == 0)
    def _():
        y_ref[...] = h_ref[...]

    hn = hn_ref[...]
    g = jnp.dot(hn, wg_ref[...], preferred_element_type=F32)
    u = jnp.dot(hn, wu_ref[...], preferred_element_type=F32)
    act = (_silu(g) * u).astype(BF16)
    y_ref[...] += jnp.dot(act, wd_ref[...], preferred_element_type=F32)
    for s_ref, d_ref in zip(src, dst):
        d_ref[...] = s_ref[...].astype(d_ref.dtype)

    @pl.when(f == nf - 1)
    def _():
        xn_ref[...] = (_rms_rows(y_ref[...]) * nw_ref[...]).astype(xn_ref.dtype)


CAST_ROWS = 256
CAST_COLS_IN = 512


def ffn(hn, h, w_gate, w_up, w_down, next_norm_w, xn_dtype, cast_next=None, tf=512, tm=TM):
    t, d = h.shape
    dff = w_gate.shape[1]
    nf = dff // tf
    ni = t // tm
    row = pl.BlockSpec((tm, d), lambda i, f: (i, 0))
    in_specs = [row, row, pl.BlockSpec((d, tf), lambda i, f: (0, f)), pl.BlockSpec((d, tf), lambda i, f: (0, f)),
                pl.BlockSpec((tf, d), lambda i, f: (f, 0)), pl.BlockSpec((1, d), lambda i, f: (0, 0))]
    out_specs = [row, row]
    out_shape = [jax.ShapeDtypeStruct((t, d), F32), jax.ShapeDtypeStruct((t, d), xn_dtype)]
    args = [hn, h, w_gate, w_up, w_down, next_norm_w.reshape(1, d)]
    n_cast = 0
    if cast_next is not None:
        nl, src_g, src_u, src_d, src_in = cast_next
        n_in = src_in.shape[1]
        nrb = ni * nf // (d // CAST_COLS_IN)
        rows_in = -(-n_in // nrb)
        rows_in += (-rows_in) % ROW_ALIGN
        while n_in % rows_in:
            rows_in += ROW_ALIGN
        ncb = d // CAST_COLS_IN
        nwb = (d // CAST_ROWS) * nf
        nib = (n_in // rows_in) * ncb
        assert nwb <= ni * nf and nib <= ni * nf
        wb = lambda i, f: jnp.minimum(i * nf + f, nwb - 1)
        ib = lambda i, f: jnp.minimum(i * nf + f, nib - 1)
        in_specs += [pl.BlockSpec((None, CAST_ROWS, tf), lambda i, f: (nl, wb(i, f) // nf, wb(i, f) % nf)),
                     pl.BlockSpec((None, CAST_ROWS, tf), lambda i, f: (nl, wb(i, f) // nf, wb(i, f) % nf)),
                     pl.BlockSpec((None, tf, CAST_ROWS), lambda i, f: (nl, wb(i, f) % nf, wb(i, f) // nf)),
                     pl.BlockSpec((None, rows_in, CAST_COLS_IN), lambda i, f: (nl, ib(i, f) // ncb, ib(i, f) % ncb))]
        out_specs += [pl.BlockSpec((CAST_ROWS, tf), lambda i, f: (wb(i, f) // nf, wb(i, f) % nf)),
                      pl.BlockSpec((CAST_ROWS, tf), lambda i, f: (wb(i, f) // nf, wb(i, f) % nf)),
                      pl.BlockSpec((tf, CAST_ROWS), lambda i, f: (wb(i, f) % nf, wb(i, f) // nf)),
                      pl.BlockSpec((rows_in, CAST_COLS_IN), lambda i, f: (ib(i, f) // ncb, ib(i, f) % ncb))]
        out_shape += [jax.ShapeDtypeStruct((d, dff), BF16), jax.ShapeDtypeStruct((d, dff), BF16),
                      jax.ShapeDtypeStruct((dff, d), BF16), jax.ShapeDtypeStruct((n_in, d), BF16)]
        args += [src_g, src_u, src_d, src_in]
        n_cast = 4
    return pl.pallas_call(
        functools.partial(_ffn_kernel, nf=nf, n_cast=n_cast),
        grid=(ni, nf),
        in_specs=in_specs,
        out_specs=out_specs,
        out_shape=out_shape,
        compiler_params=_cparams(("arbitrary", "arbitrary")),
        name="ffn",
    )(*args)


def _glu_kernel(h_ref, w_ref, b_ref, o_ref):
    h = h_ref[...]
    z = jnp.dot(h.astype(BF16), w_ref[...], preferred_element_type=F32) + b_ref[...]
    o_ref[...] = (h * _sigmoid(z)).astype(o_ref.dtype)


def s5_glu(h, w_glu, layer, b_glu, tm=TM):
    t, d = h.shape
    row = pl.BlockSpec((tm, d), lambda i: (i, 0))
    return pl.pallas_call(
        _glu_kernel,
        grid=(t // tm,),
        in_specs=[row, pl.BlockSpec((None, d, d), lambda i: (layer, 0, 0)), pl.BlockSpec((1, d), lambda i: (0, 0))],
        out_specs=row,
        out_shape=jax.ShapeDtypeStruct((t, d), BF16),
        compiler_params=_cparams(("parallel",)),
        name="s5_glu",
    )(h, w_glu, b_glu.reshape(1, d))


def _rotary(x, cos, sin):
    half = x.shape[-1] // 2
    x1, x2 = x[:, :half], x[:, half:]
    return jnp.concatenate([x1 * cos - x2 * sin, x1 * sin + x2 * cos], axis=-1)


def _ret_prompt_kernel(lg_ref, q_ref, k_ref, v_ref, g_ref, cos_ref, sin_ref, o_ref, s_out_ref, s_ref, *, nc, chunk):
    c = pl.program_id(1)

    @pl.when(c == 0)
    def _():
        s_ref[...] = jnp.zeros_like(s_ref)

    cos, sin = cos_ref[...], sin_ref[...]
    ii = lax.broadcasted_iota(jnp.int32, (chunk, chunk), 0)
    jj = lax.broadcasted_iota(jnp.int32, (chunk, chunk), 1)
    diff = (ii - jj).astype(F32)
    idx = lax.broadcasted_iota(jnp.int32, (chunk, 1), 0).astype(F32)
    heads = range(R_HEADS)
    hs = lambda h: slice(h * R_DK, (h + 1) * R_DK)
    lg = [lg_ref[h] for h in heads]
    q_l = [_rotary(q_ref[:, hs(h)], cos, sin) for h in heads]
    k_l = [_rotary(k_ref[:, hs(h)], cos, sin) * (R_DK ** -0.5) for h in heads]
    v_l = [v_ref[:, hs(h)].astype(BF16) for h in heads]
    s_l = [s_ref[h] for h in heads]
    scores = [_dg(q_l[h].astype(BF16), k_l[h].astype(BF16), _NT)
              * jnp.where(diff >= 0, jnp.exp(lg[h] * jnp.maximum(diff, 0.0)), 0.0) for h in heads]
    cross = [_dg((q_l[h] * jnp.exp(lg[h] * (idx + 1.0))).astype(BF16), s_l[h].astype(BF16), _NN) for h in heads]
    for h in heads:
        o = _dg(scores[h].astype(BF16), v_l[h], _NN) + cross[h]
        o_ref[:, hs(h)] = (_silu(g_ref[:, hs(h)]) * _rms_rows(o)).astype(o_ref.dtype)
    for h in heads:
        k_dec = (k_l[h] * jnp.exp(lg[h] * (chunk - 1.0 - idx))).astype(BF16)
        s_ref[h] = jnp.exp(lg[h] * jnp.full((1, 1), float(chunk), F32)) * s_l[h] + _dg(k_dec, v_l[h], _TN)

    @pl.when(c == nc - 1)
    def _():
        s_out_ref[0] = s_ref[...]


def retention_prompt(p_ret, cos, sin, log_gamma, batch, seq, chunk=R_CHUNK):
    nc = seq // chunk
    bw = BRANCH_WIDTH
    rows = lambda off: pl.BlockSpec((chunk, bw), functools.partial(lambda b, c, off: (b * nc + c, off), off=off))
    tab = pl.BlockSpec((chunk, R_DK // 2), lambda b, c: (c, 0))
    return pl.pallas_call(
        functools.partial(_ret_prompt_kernel, nc=nc, chunk=chunk),
        grid=(batch, nc),
        in_specs=[pl.BlockSpec(memory_space=pltpu.SMEM), rows(0), rows(1), rows(2), rows(3), tab, tab],
        out_specs=[pl.BlockSpec((chunk, bw), lambda b, c: (b * nc + c, 0)),
                   pl.BlockSpec((1, R_HEADS, R_DK, R_DK), lambda b, c: (b, 0, 0, 0))],
        out_shape=[jax.ShapeDtypeStruct((p_ret.shape[0], bw), BF16),
                   jax.ShapeDtypeStruct((batch, R_HEADS, R_DK, R_DK), F32)],
        scratch_shapes=[pltpu.VMEM((R_HEADS, R_DK, R_DK), F32)],
        compiler_params=_cparams(("parallel", "arbitrary")),
        name="ret_prompt",
    )(log_gamma, p_ret, p_ret, p_ret, p_ret, cos, sin)


def _ret_step_kernel(lg_ref, q_ref, k_ref, v_ref, g_ref, cos_ref, sin_ref, s_ref, *rest, bb):
    o_ref, s_out_ref = rest[-2:]
    h = pl.program_id(0)
    gamma = jnp.exp(lg_ref[h] * jnp.ones((1, 1), F32))
    cos, sin = cos_ref[...], sin_ref[...]
    q = _rotary(q_ref[...], cos, sin)
    k = _rotary(k_ref[...], cos, sin) * (R_DK ** -0.5)
    v = v_ref[...]
    eye = _eye(R_DK)
    q_cols = _dot_sel(eye, q, _NT)
    k_cols = _dot_sel(eye, k, _NT)
    for i in range(bb):
        s_new = gamma * s_ref[i, 0] + k_cols[:, i:i + 1] * v[i:i + 1, :]
        s_out_ref[i, 0] = s_new
        o = jnp.sum(q_cols[:, i:i + 1] * s_new, axis=0, keepdims=True)
        o_ref[i:i + 1, :] = (_silu(g_ref[i:i + 1, :]) * _rms_rows(o)).astype(o_ref.dtype)


def _alias_args(first_index, outputs):
    args, specs, aliases = [], [], {}
    for out_idx, a in outputs.items():
        if a is not None:
            aliases[first_index + len(args)] = out_idx
            args.append(a)
            specs.append(pl.BlockSpec(memory_space=pl.ANY))
    return args, specs, aliases


def retention_step(p_ret, row0, cos, sin, log_gamma, states, layer, o_all, s_all, bb=16):
    nb = states.shape[1]
    r0 = row0 // bb
    rows = lambda off: pl.BlockSpec((bb, R_DK), functools.partial(lambda h, j, off: (r0 + j, off + h), off=off))
    tab = pl.BlockSpec((1, R_DK // 2), lambda h, j: (0, 0))
    st = pl.BlockSpec((None, bb, 1, R_DK, R_DK), lambda h, j: (layer, j, h, 0, 0))
    extra, extra_specs, aliases = _alias_args(8, {0: o_all, 1: s_all})
    return pl.pallas_call(
        functools.partial(_ret_step_kernel, bb=bb),
        grid=(R_HEADS, nb // bb),
        in_specs=[pl.BlockSpec(memory_space=pltpu.SMEM), rows(0), rows(R_HEADS), rows(2 * R_HEADS),
                  rows(3 * R_HEADS), tab, tab, st, *extra_specs],
        out_specs=[pl.BlockSpec((bb, R_DK), lambda h, j: (r0 + j, h)), st],
        out_shape=[jax.ShapeDtypeStruct(o_all.shape, BF16), jax.ShapeDtypeStruct(states.shape, F32)],
        input_output_aliases=aliases,
        compiler_params=_cparams(("parallel", "parallel")),
        name="ret_step",
    )(log_gamma, p_ret, p_ret, p_ret, p_ret, cos, sin, states, *extra)


def _cmul(ar, ai, br, bi):
    return ar * br - ai * bi, ar * bi + ai * br


def s5_discretize(lam_re, lam_im, log_dt, b_re, b_im):
    dt = jnp.exp(log_dt)[:, None]
    mag = jnp.exp(lam_re * dt)
    lbr, lbi = mag * jnp.cos(lam_im * dt), mag * jnp.sin(lam_im * dt)
    den = lam_re * lam_re + lam_im * lam_im
    nr, ni = lbr - 1.0, lbi
    cr, ci = (nr * lam_re + ni * lam_im) / den, (ni * lam_re - nr * lam_im) / den
    bbr, bbi = _cmul(cr[..., None], ci[..., None], b_re, b_im)
    return dt, lbr, lbi, bbr, bbi


def _lam_power(lam_re, lam_im, dt, n):
    n = jnp.asarray(n, F32).reshape((-1,) + (1,) * lam_re.ndim)
    mag = jnp.exp(lam_re * dt * n)
    return mag * jnp.cos(lam_im * dt * n), mag * jnp.sin(lam_im * dt * n)


def s5_prompt_params(lam_re, lam_im, log_dt, b_re, b_im, c_re, c_im, d_skip):
    g, p, q = S5_GROUPS, S5_STATE, S5_CHUNK
    dt, lbr, lbi, bbr, bbi = s5_discretize(lam_re, lam_im, log_dt, b_re, b_im)
    pr, pi = _lam_power(lam_re, lam_im, dt, jnp.arange(q + 1))
    lb_r, lb_i = _cmul(pr[:q, :, :, None], pi[:q, :, :, None], bbr[None], bbi[None])
    hp = lax.Precision.HIGHEST
    kern = (jnp.einsum('gkp,tgpc->tgkc', c_re, lb_r, precision=hp)
            - jnp.einsum('gkp,tgpc->tgkc', c_im, lb_i, precision=hp))
    krow = kern.transpose(1, 3, 0, 2).reshape(g, S5_GROUP, q * S5_GROUP)
    er, ei = _cmul(pr[:q][::-1][:, :, :, None], pi[:q][::-1][:, :, :, None], bbr[None], bbi[None])
    w_re = er.transpose(1, 0, 3, 2).reshape(g, q * S5_GROUP, p)
    w_im = ei.transpose(1, 0, 3, 2).reshape(g, q * S5_GROUP, p)
    vr, vi = _cmul(c_re[None], c_im[None], pr[1:, :, None, :], pi[1:, :, None, :])
    v_re = vr.transpose(1, 3, 0, 2).reshape(g, p, q * S5_GROUP)
    v_im = (-vi).transpose(1, 3, 0, 2).reshape(g, p, q * S5_GROUP)

    def pair_cols(a):
        z = jnp.zeros_like(a)
        even = (jnp.arange(g) % 2 == 0)[:, None, None]
        return jnp.concatenate([jnp.where(even, a, z), jnp.where(even, z, a)], axis=-1)

    def pair_rows(a):
        z = jnp.zeros_like(a)
        even = (jnp.arange(g) % 2 == 0)[:, None, None]
        return jnp.concatenate([jnp.where(even, a, z), jnp.where(even, z, a)], axis=1)

    sr, si = _lam_power(lam_re, lam_im, dt, q * (2.0 ** jnp.arange(8)))
    lam_sc_re = sr.reshape(8, g // 2, 2 * p).transpose(1, 0, 2)
    lam_sc_im = si.reshape(8, g // 2, 2 * p).transpose(1, 0, 2)
    d_vec = jnp.tile(d_skip.reshape(g, 1, S5_GROUP), (1, q, 1)).reshape(g, 1, q * S5_GROUP)
    bf = lambda a: a.astype(BF16)
    return (krow, bf(pair_cols(w_re)), bf(pair_cols(w_im)), bf(pair_rows(v_re)), bf(pair_rows(v_im)),
            lam_sc_re, lam_sc_im, d_vec)


S5_TILE_GROUPS = 128 // S5_GROUP


def _toeplitz(krow, q, gw):
    blocks = [krow] + [jnp.concatenate([jnp.zeros((gw, s * gw), F32), krow[:, :(q - s) * gw]], axis=1)
                       for s in range(1, q)]
    return jnp.concatenate(blocks, axis=0).astype(BF16)


def _s5_prompt_kernel(u_ref, krow_ref, wre_ref, wim_ref, vre_ref, vim_ref, lre_ref, lim_ref, d_ref,
                      h_ref, fre_ref, fim_ref, ut_ref, ht_ref, tile_ref, *, nchb, nb):
    rows = nb * nchb
    q, gw, ng = S5_CHUNK, S5_GROUP, S5_TILE_GROUPS
    for s in range(q):
        xs = u_ref[pl.ds(s, rows, stride=q), :]
        for gg in range(ng):
            ut_ref[gg, :, s * gw:(s + 1) * gw] = xs[:, gg * gw:(gg + 1) * gw]

    pos = jnp.bitwise_and(lax.broadcasted_iota(jnp.int32, (rows, 2 * S5_STATE), 0), nchb - 1)
    first = pos == 0
    pairs = range(ng // 2)
    ub_l = [(ut_ref[2 * pr].astype(BF16), ut_ref[2 * pr + 1].astype(BF16)) for pr in pairs]
    e_re = [_dg(ub_l[pr][0], wre_ref[2 * pr], _NN) + _dg(ub_l[pr][1], wre_ref[2 * pr + 1], _NN) for pr in pairs]
    e_im = [_dg(ub_l[pr][0], wim_ref[2 * pr], _NN) + _dg(ub_l[pr][1], wim_ref[2 * pr + 1], _NN) for pr in pairs]
    d, k = 1, 0
    while d < nchb:
        keep = pos >= d
        for pr in pairs:
            lr, li = lre_ref[pr, k:k + 1, :], lim_ref[pr, k:k + 1, :]
            sr, si = pltpu.roll(e_re[pr], d, 0), pltpu.roll(e_im[pr], d, 0)
            e_re[pr], e_im[pr] = (e_re[pr] + jnp.where(keep, lr * sr - li * si, 0.0),
                                  e_im[pr] + jnp.where(keep, lr * si + li * sr, 0.0))
        d, k = d * 2, k + 1
    for pr in pairs:
        for b in range(nb):
            r = (b + 1) * nchb - 1
            fre_ref[pr, b:b + 1, :] = e_re[pr][r:r + 1, :]
            fim_ref[pr, b:b + 1, :] = e_im[pr][r:r + 1, :]
    for pr in pairs:
        xc_re = jnp.where(first, 0.0, pltpu.roll(e_re[pr], 1, 0)).astype(BF16)
        xc_im = jnp.where(first, 0.0, pltpu.roll(e_im[pr], 1, 0)).astype(BF16)
        for gi in range(2):
            gg = 2 * pr + gi
            y = (_dg(ub_l[pr][gi], _toeplitz(krow_ref[gg], q, gw), _NN)
                 + _dg(xc_re, vre_ref[gg], _NN) + _dg(xc_im, vim_ref[gg], _NN)
                 ) + ut_ref[gg] * d_ref[gg]
            ht_ref[gg] = _gelu_tanh(y)
    for s in range(q):
        for gg in range(ng):
            tile_ref[:, gg * gw:(gg + 1) * gw] = ht_ref[gg, :, s * gw:(s + 1) * gw]
        h_ref[pl.ds(s, rows, stride=q), :] = tile_ref[...]


def s5_prompt(p_s5, params, layer, batch, seq):
    krow, w_re, w_im, v_re, v_im, l_re, l_im, d_vec = params
    ng, qc, p2 = S5_TILE_GROUPS, S5_CHUNK * S5_GROUP, 2 * S5_STATE
    nchb = seq // S5_CHUNK
    rows = batch * nchb
    tp = batch * seq
    nt = BRANCH_WIDTH // 128
    grp = lambda n, *tail: pl.BlockSpec((None, n) + tail, lambda j: (layer, j) + (0,) * len(tail))
    fin = pl.BlockSpec((ng // 2, batch, p2), lambda j: (j, 0, 0))
    return pl.pallas_call(
        functools.partial(_s5_prompt_kernel, nchb=nchb, nb=batch),
        grid=(nt,),
        in_specs=[pl.BlockSpec((tp, 128), lambda j: (0, j)), grp(ng, S5_GROUP, qc), grp(ng, qc, p2), grp(ng, qc, p2),
                  grp(ng, p2, qc), grp(ng, p2, qc), grp(ng // 2, 8, p2), grp(ng // 2, 8, p2), grp(ng, 1, qc)],
        out_specs=[pl.BlockSpec((tp, 128), lambda j: (0, j)), fin, fin],
        out_shape=[jax.ShapeDtypeStruct(p_s5.shape, F32),
                   jax.ShapeDtypeStruct((S5_GROUPS // 2, batch, p2), F32),
                   jax.ShapeDtypeStruct((S5_GROUPS // 2, batch, p2), F32)],
        scratch_shapes=[pltpu.VMEM((ng, rows, qc), F32), pltpu.VMEM((ng, rows, qc), F32),
                        pltpu.VMEM((rows, 128), F32)],
        compiler_params=_cparams(("parallel",)),
        name="s5_prompt",
    )(p_s5, krow, w_re, w_im, v_re, v_im, l_re, l_im, d_vec)


def s5_step_params(lam_re, lam_im, log_dt, b_re, b_im, c_re, c_im, d_skip):
    g, p = S5_GROUPS, S5_STATE
    _, lbr, lbi, bbr, bbi = s5_discretize(lam_re, lam_im, log_dt, b_re, b_im)
    eye8 = jnp.eye(8, dtype=F32)

    def bd_in(bb):
        x = bb.reshape(8, 8, p, S5_GROUP)
        return jnp.einsum('jgpc,gh->jgchp', x, eye8).reshape(8, 8 * S5_GROUP, 8 * p)

    def bd_out(cc):
        x = cc.reshape(8, 8, S5_GROUP, p)
        return jnp.einsum('jgkp,gh->jgphk', x, eye8).reshape(8, 8 * p, 8 * S5_GROUP)

    return (bd_in(bbr), bd_in(bbi), bd_out(c_re), bd_out(-c_im), lbr.reshape(1, g * p), lbi.reshape(1, g * p),
            d_skip.reshape(1, g * S5_GROUP))


def _s5_step_kernel(u_ref, xr_ref, xi_ref, bre_ref, bim_ref, cre_ref, cim_ref, lr_ref, li_ref, d_ref, *rest):
    h_ref, nr_ref, ni_ref = rest[-3:]
    u = u_ref[...]
    lr, li = lr_ref[...], li_ref[...]
    xr, xi = xr_ref[...], xi_ref[...]
    nr = lr * xr - li * xi + _dot3(u, bre_ref[0])
    ni = lr * xi + li * xr + _dot3(u, bim_ref[0])
    nr_ref[...] = nr
    ni_ref[...] = ni
    y = _dot3(nr, cre_ref[0]) + _dot3(ni, cim_ref[0]) + u * d_ref[...]
    h_ref[...] = _gelu_tanh(y)


def s5_step(p_s5, row0, x_re, x_im, params, layer, h_all):
    b_re, b_im, c_re, c_im, l_re, l_im, d_vec = params
    nb, gp = x_re.shape
    lt = 8 * S5_GROUP
    st = 8 * S5_STATE
    r0 = row0 // nb
    u_spec = pl.BlockSpec((nb, lt), lambda j: (r0, j))
    x_spec = pl.BlockSpec((nb, st), lambda j: (0, j))
    mat = lambda a, b: pl.BlockSpec((None, 1, a, b), lambda j: (layer, j, 0, 0))
    vec = lambda n: pl.BlockSpec((None, 1, n), lambda j: (layer, 0, j))
    extra, extra_specs, aliases = _alias_args(10, {0: h_all})
    return pl.pallas_call(
        _s5_step_kernel,
        grid=(gp // st,),
        in_specs=[u_spec, x_spec, x_spec, mat(lt, st), mat(lt, st), mat(st, lt), mat(st, lt),
                  vec(st), vec(st), vec(lt), *extra_specs],
        out_specs=[u_spec, x_spec, x_spec],
        out_shape=[jax.ShapeDtypeStruct(h_all.shape, F32), jax.ShapeDtypeStruct((nb, gp), F32),
                   jax.ShapeDtypeStruct((nb, gp), F32)],
        input_output_aliases=aliases,
        compiler_params=_cparams(("parallel",)),
        name="s5_step",
    )(p_s5, x_re, x_im, b_re, b_im, c_re, c_im, l_re, l_im, d_vec, *extra)


def _l2norm_rows(x):
    return x * lax.rsqrt(jnp.sum(x * x, axis=-1, keepdims=True) + NORM_EPS)


def _gdn_prompt_kernel(qkv_ref, z_ref, ab_ref, cw_ref, alog_ref, dtb_ref, nw_ref, o_ref, s_out_ref,
                       xbuf_ref, s_ref, m_s, sol_s, qe_s, qkd_s, kdec_s, egl_s, *, nc, chunk):
    c = pl.program_id(1)
    bw = BRANCH_WIDTH
    halo = 8
    heads = range(G_HEADS)

    @pl.when(c == 0)
    def _():
        xbuf_ref[0:halo, :] = jnp.zeros((halo, 3 * bw), F32)
        s_ref[...] = jnp.zeros_like(s_ref)
        for ref in (m_s, sol_s, qe_s, qkd_s, kdec_s, egl_s):
            ref[...] = jnp.zeros_like(ref)

    m_l = [m_s[h] for h in heads]
    sol_l = [sol_s[h] for h in heads]
    qe_l = [qe_s[h] for h in heads]
    qkd_l = [qkd_s[h] for h in heads]
    kdec_l = [kdec_s[h] for h in heads]
    egl_l = [egl_s[h] for h in heads]

    x = qkv_ref[...]
    xbuf_ref[halo:halo + chunk, :] = x
    conv = x * cw_ref[G_CONV - 1:G_CONV, :]
    for i in range(1, G_CONV):
        conv = conv + xbuf_ref[halo - i:halo - i + chunk, :] * cw_ref[G_CONV - 1 - i:G_CONV - i, :]
    xbuf_ref[0:halo, :] = x[chunk - halo:chunk, :]
    qkv = _silu(conv)

    ab = ab_ref[...]
    g_all = -jnp.exp(alog_ref[...]) * _softplus(ab + dtb_ref[...])
    beta_all = _sigmoid(ab)
    ri = lax.broadcasted_iota(jnp.int32, (chunk, chunk), 0)
    ci = lax.broadcasted_iota(jnp.int32, (chunk, chunk), 1)
    causal = ri >= ci
    strict = ri > ci
    ltri = jnp.where(causal, 1.0, 0.0).astype(BF16)
    gcum = _dot_sel(ltri, g_all)
    gcum_t = _dot_sel(_eye(128), gcum, _NT)
    z = z_ref[...]
    nw = nw_ref[...]

    def prepare(h):
        q = _l2norm_rows(qkv[:, h * G_DK:(h + 1) * G_DK]) * (G_DK ** -0.5)
        k = _l2norm_rows(qkv[:, bw + h * G_DK:bw + (h + 1) * G_DK])
        v = qkv[:, 2 * bw + h * G_DK:2 * bw + (h + 1) * G_DK]
        gc = gcum[:, h:h + 1]
        gr = gcum_t[h:h + 1, :]
        beta = beta_all[:, G_HEADS + h:G_HEADS + h + 1]
        decay = jnp.where(causal, jnp.exp(jnp.where(causal, gc - gr, 0.0)), 0.0)
        kb = k.astype(BF16)
        qk_kk = _dg(jnp.concatenate([kb, q.astype(BF16)], axis=0), kb, _NT)
        eg = jnp.exp(gc)
        g_last = gc[chunk - 1:chunk, :]
        qe_s[h] = (q * eg).astype(BF16)
        qkd_s[h] = (qk_kk[chunk:] * decay).astype(BF16)
        kdec_s[h] = (k * jnp.exp(g_last - gc)).astype(BF16)
        egl_s[h] = jnp.broadcast_to(jnp.exp(g_last), (1, G_DK))
        m_s[h] = jnp.where(strict, -(beta * qk_kk[:chunk] * decay), 0.0)
        sol_s[h] = jnp.concatenate([beta * v, (beta * eg) * k], axis=-1)

    todo = list(heads)
    span = 1
    while span < chunk:
        for h in heads:
            mb = m_l[h].astype(BF16)
            if 2 * span < chunk:
                sol_hi, sol_lo = _split2(sol_l[h])
                r = _dg(mb, jnp.concatenate([sol_hi, mb], axis=-1), _NN)
                sol_l[h] = sol_l[h] + (r[:, :2 * G_DK] + _dg(mb, sol_lo, _NN))
                m_l[h] = r[:, 2 * G_DK:]
            else:
                sol_l[h] = sol_l[h] + _dot2(mb, sol_l[h])
        prepare(todo.pop(0))
        span *= 2
    s_l = [s_ref[h] for h in heads]
    ks_qs_l = [_dg(jnp.concatenate([sol_l[h][:, G_DK:].astype(BF16), qe_l[h]], axis=0), s_l[h].astype(BF16), _NN)
               for h in heads]
    wb_l = [(sol_l[h][:, :G_DK] - ks_qs_l[h][:chunk]).astype(BF16) for h in heads]
    while todo:
        prepare(todo.pop(0))
    for h in heads:
        o = ks_qs_l[h][chunk:] + _dg(qkd_l[h], wb_l[h], _NN)
        zh = z[:, h * G_DK:(h + 1) * G_DK]
        o_ref[:, h * G_DK:(h + 1) * G_DK] = (_rms_rows(o) * nw * _silu(zh)).astype(o_ref.dtype)
    for h in heads:
        s_ref[h] = egl_l[h] * s_l[h] + _dg(kdec_l[h], wb_l[h], _TN)

    @pl.when(c == nc)
    def _():
        s_out_ref[0] = s_ref[...]


def gdn_prompt(p_qkv, p_z, p_ab, conv_w, a_log, dt_bias, norm_w, batch, seq, chunk=G_CHUNK):
    nc = seq // chunk
    bw = BRANCH_WIDTH
    pad = lambda x: jnp.zeros((1, 128), F32).at[0, :G_HEADS].set(x)
    full = lambda shape: pl.BlockSpec(shape, lambda b, c: (0,) * len(shape))
    cur = lambda width: pl.BlockSpec((chunk, width), lambda b, c: (b * nc + jnp.minimum(c, nc - 1), 0))
    prev = lambda width: pl.BlockSpec((chunk, width), lambda b, c: (b * nc + jnp.maximum(c - 1, 0), 0))
    hd = lambda width, dtype: pltpu.VMEM((G_HEADS, chunk, width), dtype)
    return pl.pallas_call(
        functools.partial(_gdn_prompt_kernel, nc=nc, chunk=chunk),
        grid=(batch, nc + 1),
        in_specs=[cur(3 * bw), prev(bw), cur(128),
                  full((G_CONV, 3 * bw)), full((1, 128)), full((1, 128)), full((1, G_DK))],
        out_specs=[prev(bw), pl.BlockSpec((1, G_HEADS, G_DK, G_DK), lambda b, c: (b, 0, 0, 0))],
        out_shape=[jax.ShapeDtypeStruct((p_qkv.shape[0], bw), BF16),
                   jax.ShapeDtypeStruct((batch, G_HEADS, G_DK, G_DK), F32)],
        scratch_shapes=[pltpu.VMEM((8 + chunk, 3 * bw), F32), pltpu.VMEM((G_HEADS, G_DK, G_DK), F32),
                        hd(chunk, F32), hd(2 * G_DK, F32), hd(G_DK, BF16), hd(chunk, BF16), hd(G_DK, BF16),
                        pltpu.VMEM((G_HEADS, 1, G_DK), F32)],
        compiler_params=_cparams(("parallel", "arbitrary")),
        name="gdn_prompt",
    )(p_qkv, p_z, p_ab, conv_w, pad(a_log), pad(dt_bias), norm_w.reshape(1, G_DK))


def _gdn_step_kernel(qkv_ref, z_ref, ab_ref, cbuf_ref, cw_ref, alog_ref, dtb_ref, nw_ref, s_ref, *rest, bb):
    o_ref, cbuf_out_ref, s_out_ref = rest[-3:]
    bw = BRANCH_WIDTH
    x = qkv_ref[...]
    conv = x * cw_ref[G_CONV - 1:G_CONV, :]
    for i in range(G_CONV - 1):
        conv = conv + cbuf_ref[i] * cw_ref[i:i + 1, :]
    for i in range(G_CONV - 2):
        cbuf_out_ref[i] = cbuf_ref[i + 1]
    cbuf_out_ref[G_CONV - 2] = x
    qkv = _silu(conv)
    ab = ab_ref[...]
    g_all = -jnp.exp(alog_ref[...]) * _softplus(ab + dtb_ref[...])
    beta_all = _sigmoid(ab)
    eye = _eye(G_DK)
    z = z_ref[...]
    nw = nw_ref[...]
    for h in range(G_HEADS):
        q = _l2norm_rows(qkv[:, h * G_DK:(h + 1) * G_DK]) * (G_DK ** -0.5)
        k = _l2norm_rows(qkv[:, bw + h * G_DK:bw + (h + 1) * G_DK])
        v = qkv[:, 2 * bw + h * G_DK:2 * bw + (h + 1) * G_DK]
        eg = jnp.exp(g_all[:, h:h + 1])
        beta = beta_all[:, G_HEADS + h:G_HEADS + h + 1]
        q_cols = _dot_sel(eye, q, _NT)
        k_cols = _dot_sel(eye, k, _NT)
        rng = range(bb)
        s_l = [s_ref[i, h] for i in rng]
        ks_l = [jnp.sum(k_cols[:, i:i + 1] * s_l[i], axis=0, keepdims=True) for i in rng]
        w_l = [beta[i:i + 1, :] * (v[i:i + 1, :] - eg[i:i + 1, :] * ks_l[i]) for i in rng]
        sn_l = [eg[i:i + 1, :] * s_l[i] + k_cols[:, i:i + 1] * w_l[i] for i in rng]
        for i in rng:
            s_out_ref[i, h] = sn_l[i]
        o = jnp.concatenate([jnp.sum(q_cols[:, i:i + 1] * sn_l[i], axis=0, keepdims=True) for i in rng], axis=0)
        zh = z[:, h * G_DK:(h + 1) * G_DK]
        o_ref[:, h * G_DK:(h + 1) * G_DK] = (_rms_rows(o) * nw * _silu(zh)).astype(o_ref.dtype)


def gdn_step(p_qkv, p_z, p_ab, row0, conv_buf_t, states, layer, conv_w, a_log, dt_bias, norm_w, o_all, s_all, bb=8):
    nb = states.shape[1]
    bw = BRANCH_WIDTH
    r0 = row0 // bb
    pad = lambda x: jnp.zeros((1, 128), F32).at[0, :G_HEADS].set(x)
    full = lambda shape: pl.BlockSpec(shape, lambda j: (0,) * len(shape))
    cb = pl.BlockSpec((G_CONV - 1, bb, 3 * bw), lambda j: (0, j, 0))
    st = pl.BlockSpec((None, bb, G_HEADS, G_DK, G_DK), lambda j: (layer, j, 0, 0, 0))
    extra, extra_specs, aliases = _alias_args(9, {0: o_all, 2: s_all})
    return pl.pallas_call(
        functools.partial(_gdn_step_kernel, bb=bb),
        grid=(nb // bb,),
        in_specs=[pl.BlockSpec((bb, 3 * bw), lambda j: (r0 + j, 0)),
                  pl.BlockSpec((bb, bw), lambda j: (r0 + j, 0)),
                  pl.BlockSpec((bb, 128), lambda j: (r0 + j, 0)),
                  cb, full((G_CONV, 3 * bw)), full((1, 128)), full((1, 128)), full((1, G_DK)), st, *extra_specs],
        out_specs=[pl.BlockSpec((bb, bw), lambda j: (r0 + j, 0)), cb, st],
        out_shape=[jax.ShapeDtypeStruct(o_all.shape, BF16), jax.ShapeDtypeStruct(conv_buf_t.shape, F32),
                   jax.ShapeDtypeStruct(states.shape, F32)],
        input_output_aliases=aliases,
        compiler_params=_cparams(("parallel",)),
        name="gdn_step",
    )(p_qkv, p_z, p_ab, conv_buf_t, conv_w, pad(a_log), pad(dt_bias), norm_w.reshape(1, G_DK), states, *extra)


def _rope_tables(pos):
    half = R_DK // 2
    inv = ROPE_BASE ** (-jnp.arange(half, dtype=F32) / half)
    ang = pos.astype(F32)[:, None] * inv[None, :]
    return jnp.cos(ang), jnp.sin(ang)


def kernel(x_prompt, x_sample, state_ret, state_s5_re, state_s5_im, state_gdn, state_gdn_conv, norm_mix_w, w_in, s5_lambda_re, s5_lambda_im, s5_log_dt, s5_b_re, s5_b_im, s5_c_re, s5_c_im, s5_d, s5_w_glu, s5_b_glu, gdn_conv_w, gdn_a_log, gdn_dt_bias, gdn_norm_w, w_branch, w_out, norm_ffn_w, w_gate_ffn, w_up_ffn, w_down_ffn, norm_final_w):
    bp, lp, d = x_prompt.shape
    nb = x_sample.shape[0]
    depth = w_in.shape[0]
    tp = bp * lp
    bw = BRANCH_WIDTH
    x = jnp.concatenate([x_prompt.reshape(tp, d), x_sample.reshape(nb, d)], axis=0)

    cos_p, sin_p = _rope_tables(jnp.arange(lp, dtype=jnp.int32))
    cos_s, sin_s = _rope_tables(PAST_LEN + jnp.arange(1, dtype=jnp.int32))
    log_gamma = jnp.log1p(-jnp.exp2(-5.0 - jnp.arange(R_HEADS, dtype=F32)))

    o_su, o_gqkv, o_ga, o_gz, o_gates = 4 * bw, 5 * bw, 8 * bw, 8 * bw + 2 * G_HEADS, 9 * bw + 2 * G_HEADS
    w_in_t = jnp.swapaxes(w_in, 1, 2)
    w_in_b = w_in_t[0].astype(BF16)[None]
    w_gate_b, w_up_b, w_down_b = w_gate_ffn[0].astype(BF16), w_up_ffn[0].astype(BF16), w_down_ffn[0].astype(BF16)
    w_branch_b, w_out_b, w_glu_b = w_branch.astype(BF16), w_out.astype(BF16), s5_w_glu.astype(BF16)
    s5_raw = (s5_lambda_re, s5_lambda_im, s5_log_dt, s5_b_re, s5_b_im, s5_c_re, s5_c_im, s5_d)
    s5_pp = jax.vmap(s5_prompt_params)(*s5_raw)
    s5_sp = jax.vmap(s5_step_params)(*s5_raw)

    new_p = [[], [], [], [], []]
    new_s = [[], [], []]
    ret_s_all = gdn_s_all = None
    xn = rmsnorm_rows(x, norm_mix_w[0], BF16)
    y_final = None
    for l in range(depth):
        p_ret = matmul(xn, w_in_b, 0, 0, o_su, tn=1024, name="proj_ret")
        p_s5 = matmul(xn, w_in_b, 0, o_su, bw, tn=1024, name="proj_s5")
        p_qkv = matmul(xn, w_in_b, 0, o_gqkv, 3 * bw, tn=1024, name="proj_qkv")
        p_ab = matmul(xn, w_in_b, 0, o_ga, LANE, tn=LANE, name="proj_ab")
        p_z = matmul(xn, w_in_b, 0, o_gz, bw, tn=512, name="proj_z")

        r_all, ret_p = retention_prompt(p_ret, cos_p, sin_p, log_gamma, bp, lp)
        r_all, ret_s_all = retention_step(p_ret, tp, cos_s, sin_s, log_gamma, state_ret, l, r_all, ret_s_all)

        h_all, f_re, f_im = s5_prompt(p_s5, s5_pp, l, bp, lp)
        fin = lambda f: f[:, :bp, :].reshape(S5_GROUPS // 2, bp, 2, S5_STATE).transpose(1, 0, 2, 3).reshape(
            bp, S5_GROUPS, S5_STATE)
        h_all, s5re_s, s5im_s = s5_step(p_s5, tp, state_s5_re[l].reshape(nb, -1), state_s5_im[l].reshape(nb, -1),
                                        s5_sp, l, h_all)
        s_all = s5_glu(h_all, w_glu_b, l, s5_b_glu[l])

        g_all, gdn_p = gdn_prompt(p_qkv, p_z, p_ab, gdn_conv_w[l], gdn_a_log[l], gdn_dt_bias[l], gdn_norm_w[l], bp, lp)
        g_all, conv_s_t, gdn_s_all = gdn_step(p_qkv, p_z, p_ab, tp, state_gdn_conv[l].transpose(1, 0, 2), state_gdn, l,
                                              gdn_conv_w[l], gdn_a_log[l], gdn_dt_bias[l], gdn_norm_w[l],
                                              g_all, gdn_s_all)
        conv_p = jnp.stack([p_qkv[(b + 1) * lp - (G_CONV - 1):(b + 1) * lp] for b in range(bp)], axis=0)

        merged = merge_branches(xn, r_all, s_all, g_all, w_in_b, o_gates, w_branch_b, l)
        h, hn = out_proj(merged, w_out_b, l, x, norm_ffn_w[l])
        last = l == depth - 1
        if last:
            x, y_final = ffn(hn, h, w_gate_b, w_up_b, w_down_b, norm_final_w, F32)
        else:
            x, xn, w_gate_b, w_up_b, w_down_b, w_in_next = ffn(
                hn, h, w_gate_b, w_up_b, w_down_b, norm_mix_w[l + 1], BF16,
                cast_next=(l + 1, w_gate_ffn, w_up_ffn, w_down_ffn, w_in_t))
            w_in_b = w_in_next[None]

        for lst, val in zip(new_p, (ret_p, fin(f_re), fin(f_im), gdn_p, conv_p)):
            lst.append(val)
        for lst, val in zip(new_s, (s5re_s.reshape(nb, S5_GROUPS, S5_STATE),
                                    s5im_s.reshape(nb, S5_GROUPS, S5_STATE), conv_s_t.transpose(1, 0, 2))):
            lst.append(val)

    y_prompt = y_final[:tp].reshape(bp, lp, d)
    y_sample = y_final[tp:].reshape(nb, 1, d)
    outs_p = [jnp.stack(a, 0) for a in new_p]
    s5re_s, s5im_s, conv_s = [jnp.stack(a, 0) for a in new_s]
    return (y_prompt, y_sample, *outs_p, ret_s_all, s5re_s, s5im_s, gdn_s_all, conv_s)
```

```python
import functools
import math

import jax
import jax.numpy as jnp
from jax import lax
from jax.experimental import pallas as pl
from jax.experimental.pallas import tpu as pltpu

F32 = jnp.float32
BF16 = jnp.bfloat16

D_MODEL = 2048
BRANCH_WIDTH = 1024
N_BRANCH = 3
R_HEADS = 4
R_DK = 256
R_CHUNK = 128
ROPE_BASE = 10000.0
S5_GROUP = 16
S5_GROUPS = 64
S5_STATE = 64
S5_CHUNK = 16
G_HEADS = 8
G_DK = 128
G_CONV = 4
G_CHUNK = 64
NORM_EPS = 1e-6
PAST_LEN = 16384

VMEM_LIMIT_BYTES = 56 * 1024 * 1024
TM = 640


def _cparams(sem):
    return pltpu.CompilerParams(dimension_semantics=sem, vmem_limit_bytes=VMEM_LIMIT_BYTES)


_NN = (((1,), (0,)), ((), ()))
_NT = (((1,), (1,)), ((), ()))
_TN = (((0,), (0,)), ((), ()))


def _dg(a, b, dims):
    return lax.dot_general(a, b, dims, preferred_element_type=F32)


def _split2(a):
    hi = a.astype(BF16)
    lo = (a - hi.astype(F32)).astype(BF16)
    return hi, lo


def _dot3(a, b, dims=_NN):
    ah, al = _split2(a)
    bh, bl = _split2(b)
    return _dg(ah, bh, dims) + (_dg(ah, bl, dims) + _dg(al, bh, dims))


def _dot2(a_bf16, b, dims=_NN):
    bh, bl = _split2(b)
    return _dg(a_bf16, bh, dims) + _dg(a_bf16, bl, dims)


def _dot_sel(sel_bf16, x, dims=_NN):
    h1 = x.astype(BF16)
    r1 = x - h1.astype(F32)
    h2 = r1.astype(BF16)
    h3 = (r1 - h2.astype(F32)).astype(BF16)
    return _dg(sel_bf16, h1, dims) + (_dg(sel_bf16, h2, dims) + _dg(sel_bf16, h3, dims))


def _sel_dot(x, sel_bf16, dims=_NN):
    h1 = x.astype(BF16)
    r1 = x - h1.astype(F32)
    h2 = r1.astype(BF16)
    h3 = (r1 - h2.astype(F32)).astype(BF16)
    return _dg(h1, sel_bf16, dims) + (_dg(h2, sel_bf16, dims) + _dg(h3, sel_bf16, dims))


def _eye(n, dtype=BF16):
    r = lax.broadcasted_iota(jnp.int32, (n, n), 0)
    c = lax.broadcasted_iota(jnp.int32, (n, n), 1)
    return jnp.where(r == c, 1.0, 0.0).astype(dtype)


def _sigmoid(x):
    return 1.0 / (1.0 + jnp.exp(-x))


def _silu(x):
    return x * _sigmoid(x)


def _gelu_tanh(x):
    return 0.5 * x * (1.0 + jnp.tanh(math.sqrt(2.0 / math.pi) * (x + 0.044715 * (x * x * x))))


def _softplus(x):
    return jnp.maximum(x, 0.0) + jnp.log1p(jnp.exp(-jnp.abs(x)))


def _rms_rows(x, eps=NORM_EPS):
    return x * lax.rsqrt(jnp.mean(x * x, axis=-1, keepdims=True) + eps)


def _norm_kernel(x_ref, w_ref, o_ref):
    o_ref[...] = (_rms_rows(x_ref[...]) * w_ref[...]).astype(o_ref.dtype)


def rmsnorm_rows(x, w, out_dtype, tm=TM):
    t, d = x.shape
    return pl.pallas_call(
        _norm_kernel,
        grid=(t // tm,),
        in_specs=[pl.BlockSpec((tm, d), lambda i: (i, 0)), pl.BlockSpec((1, d), lambda i: (0, 0))],
        out_specs=pl.BlockSpec((tm, d), lambda i: (i, 0)),
        out_shape=jax.ShapeDtypeStruct((t, d), out_dtype),
        compiler_params=_cparams(("parallel",)),
        name="rmsnorm",
    )(x, w.reshape(1, d))


LANE = 128
ROW_ALIGN = 16


def _shifted_rows(w_ref, wn_ref, shift):
    if shift == 0:
        return w_ref[...]
    return jnp.concatenate([w_ref[shift:, :], wn_ref[:shift, :]], axis=0)


def _shifted_specs(k, tn, layer, row0, nidx):
    base, shift = (row0 // tn) * tn, row0 % tn
    assert shift % ROW_ALIGN == 0 and tn % max(shift, ROW_ALIGN) == 0
    rows = max(shift, ROW_ALIGN)
    main = pl.BlockSpec((None, tn, k), lambda i, j: (layer, base // tn + nidx(i, j), 0))
    nxt = pl.BlockSpec((None, rows, k), lambda i, j: (layer, (base + (nidx(i, j) + 1) * tn) // rows, 0))
    return main, nxt, shift


def _mm_kernel(a_ref, w_ref, wn_ref, o_ref, *, shift):
    o_ref[...] = _dg(a_ref[...], _shifted_rows(w_ref, wn_ref, shift), _NT)


def matmul(a, wt, layer, row0, n, tn, tm=TM, name="proj"):
    t, k = a.shape
    main, nxt, shift = _shifted_specs(k, tn, layer, row0, lambda i, j: j)
    return pl.pallas_call(
        functools.partial(_mm_kernel, shift=shift),
        grid=(t // tm, n // tn),
        in_specs=[pl.BlockSpec((tm, k), lambda i, j: (i, 0)), main, nxt],
        out_specs=pl.BlockSpec((tm, tn), lambda i, j: (i, j)),
        out_shape=jax.ShapeDtypeStruct((t, n), F32),
        compiler_params=_cparams(("parallel", "arbitrary")),
        name=name,
    )(a, wt, wt)


def _merge_kernel(xn_ref, br_r, br_s, br_g, wg0, wn0, wg1, wn1, wg2, wn2, wb_ref, o_ref, *, shift):
    xn = xn_ref[...]
    acc = None
    for i, (br, wg, wn) in enumerate(((br_r, wg0, wn0), (br_s, wg1, wn1), (br_g, wg2, wn2))):
        gate = _sigmoid(_dg(xn, _shifted_rows(wg, wn, shift), _NT))
        up = jnp.dot(br[...], wb_ref[i], preferred_element_type=F32)
        acc = gate * up if acc is None else acc + gate * up
    o_ref[...] = acc.astype(o_ref.dtype)


def merge_branches(xn, br_r, br_s, br_g, w_in, gate_col0, w_branch, layer, tn=512, tm=TM):
    t, d = xn.shape
    bw = br_r.shape[1]
    nj = d // tn
    a_spec = pl.BlockSpec((tm, bw), lambda i, j: (i, 0))
    wg_specs = []
    for n in range(N_BRANCH):
        main, nxt, shift = _shifted_specs(d, tn, 0, gate_col0, functools.partial(lambda i, j, n: n * nj + j, n=n))
        wg_specs += [main, nxt]
    return pl.pallas_call(
        functools.partial(_merge_kernel, shift=shift),
        grid=(t // tm, nj),
        in_specs=[pl.BlockSpec((tm, d), lambda i, j: (i, 0)), a_spec, a_spec, a_spec, *wg_specs,
                  pl.BlockSpec((None, N_BRANCH, bw, tn), lambda i, j: (layer, 0, 0, j))],
        out_specs=pl.BlockSpec((tm, tn), lambda i, j: (i, j)),
        out_shape=jax.ShapeDtypeStruct((t, d), BF16),
        compiler_params=_cparams(("parallel", "arbitrary")),
        name="merge",
    )(xn, br_r, br_s, br_g, *([w_in] * (2 * N_BRANCH)), w_branch)


def _outproj_kernel(m_ref, w_ref, x_ref, nw_ref, h_ref, hn_ref):
    h = x_ref[...] + jnp.dot(m_ref[...], w_ref[...], preferred_element_type=F32)
    h_ref[...] = h
    hn_ref[...] = (_rms_rows(h) * nw_ref[...]).astype(hn_ref.dtype)


def out_proj(merged, w_out, layer, x, norm_w, tm=TM):
    t, d = x.shape
    row = pl.BlockSpec((tm, d), lambda i: (i, 0))
    return pl.pallas_call(
        _outproj_kernel,
        grid=(t // tm,),
        in_specs=[row, pl.BlockSpec((None, d, d), lambda i: (layer, 0, 0)), row,
                  pl.BlockSpec((1, d), lambda i: (0, 0))],
        out_specs=[row, row],
        out_shape=[jax.ShapeDtypeStruct((t, d), F32), jax.ShapeDtypeStruct((t, d), BF16)],
        compiler_params=_cparams(("parallel",)),
        name="outproj",
    )(merged, w_out, x, norm_w.reshape(1, d))


def _ffn_kernel(hn_ref, h_ref, wg_ref, wu_ref, wd_ref, nw_ref, *rest, nf, n_cast):
    src, (y_ref, xn_ref), dst = rest[:n_cast], rest[n_cast:n_cast + 2], rest[n_cast + 2:]
    f = pl.program_id(1)

    @pl.when(f == 0)
    def _():
        y_ref[...] = h_ref[...]

    hn = hn_ref[...]
    g = jnp.dot(hn, wg_ref[...], preferred_element_type=F32)
    u = jnp.dot(hn, wu_ref[...], preferred_element_type=F32)
    act = (_silu(g) * u).astype(BF16)
    y_ref[...] += jnp.dot(act, wd_ref[...], preferred_element_type=F32)
    for s_ref, d_ref in zip(src, dst):
        d_ref[...] = s_ref[...].astype(d_ref.dtype)

    @pl.when(f == nf - 1)
    def _():
        xn_ref[...] = (_rms_rows(y_ref[...]) * nw_ref[...]).astype(xn_ref.dtype)


CAST_ROWS = 256
CAST_COLS_IN = 512


def ffn(hn, h, w_gate, w_up, w_down, next_norm_w, xn_dtype, cast_next=None, tf=512, tm=TM):
    t, d = h.shape
    dff = w_gate.shape[1]
    nf = dff // tf
    ni = t // tm
    row = pl.BlockSpec((tm, d), lambda i, f: (i, 0))
    in_specs = [row, row, pl.BlockSpec((d, tf), lambda i, f: (0, f)), pl.BlockSpec((d, tf), lambda i, f: (0, f)),
                pl.BlockSpec((tf, d), lambda i, f: (f, 0)), pl.BlockSpec((1, d), lambda i, f: (0, 0))]
    out_specs = [row, row]
    out_shape = [jax.ShapeDtypeStruct((t, d), F32), jax.ShapeDtypeStruct((t, d), xn_dtype)]
    args = [hn, h, w_gate, w_up, w_down, next_norm_w.reshape(1, d)]
    n_cast = 0
    if cast_next is not None:
        nl, src_g, src_u, src_d, src_in = cast_next
        n_in = src_in.shape[1]
        nrb = ni * nf // (d // CAST_COLS_IN)
        rows_in = -(-n_in // nrb)
        rows_in += (-rows_in) % ROW_ALIGN
        while n_in % rows_in:
            rows_in += ROW_ALIGN
        ncb = d // CAST_COLS_IN
        nwb = (d // CAST_ROWS) * nf
        nib = (n_in // rows_in) * ncb
        assert nwb <= ni * nf and nib <= ni * nf
        wb = lambda i, f: jnp.minimum(i * nf + f, nwb - 1)
        ib = lambda i, f: jnp.minimum(i * nf + f, nib - 1)
        in_specs += [pl.BlockSpec((None, CAST_ROWS, tf), lambda i, f: (nl, wb(i, f) // nf, wb(i, f) % nf)),
                     pl.BlockSpec((None, CAST_ROWS, tf), lambda i, f: (nl, wb(i, f) // nf, wb(i, f) % nf)),
                     pl.BlockSpec((None, tf, CAST_ROWS), lambda i, f: (nl, wb(i, f) % nf, wb(i, f) // nf)),
                     pl.BlockSpec((None, rows_in, CAST_COLS_IN), lambda i, f: (nl, ib(i, f) // ncb, ib(i, f) % ncb))]
        out_specs += [pl.BlockSpec((CAST_ROWS, tf), lambda i, f: (wb(i, f) // nf, wb(i, f) % nf)),
                      pl.BlockSpec((CAST_ROWS, tf), lambda i, f: (wb(i, f) // nf, wb(i, f) % nf)),
                      pl.BlockSpec((tf, CAST_ROWS), lambda i, f: (wb(i, f) % nf, wb(i, f) // nf)),
                      pl.BlockSpec((rows_in, CAST_COLS_IN), lambda i, f: (ib(i, f) // ncb, ib(i, f) % ncb))]
        out_shape += [jax.ShapeDtypeStruct((d, dff), BF16), jax.ShapeDtypeStruct((d, dff), BF16),
                      jax.ShapeDtypeStruct((dff, d), BF16), jax.ShapeDtypeStruct((n_in, d), BF16)]
        args += [src_g, src_u, src_d, src_in]
        n_cast = 4
    return pl.pallas_call(
        functools.partial(_ffn_kernel, nf=nf, n_cast=n_cast),
        grid=(ni, nf),
        in_specs=in_specs,
        out_specs=out_specs,
        out_shape=out_shape,
        compiler_params=_cparams(("arbitrary", "arbitrary")),
        name="ffn",
    )(*args)


def _glu_kernel(h_ref, w_ref, b_ref, o_ref):
    h = h_ref[...]
    z = jnp.dot(h.astype(BF16), w_ref[...], preferred_element_type=F32) + b_ref[...]
    o_ref[...] = (h * _sigmoid(z)).astype(o_ref.dtype)


def s5_glu(h, w_glu, layer, b_glu, tm=TM):
    t, d = h.shape
    row = pl.BlockSpec((tm, d), lambda i: (i, 0))
    return pl.pallas_call(
        _glu_kernel,
        grid=(t // tm,),
        in_specs=[row, pl.BlockSpec((None, d, d), lambda i: (layer, 0, 0)), pl.BlockSpec((1, d), lambda i: (0, 0))],
        out_specs=row,
        out_shape=jax.ShapeDtypeStruct((t, d), BF16),
        compiler_params=_cparams(("parallel",)),
        name="s5_glu",
    )(h, w_glu, b_glu.reshape(1, d))


def _rotary(x, cos, sin):
    half = x.shape[-1] // 2
    x1, x2 = x[:, :half], x[:, half:]
    return jnp.concatenate([x1 * cos - x2 * sin, x1 * sin + x2 * cos], axis=-1)


def _ret_prompt_kernel(lg_ref, q_ref, k_ref, v_ref, g_ref, cos_ref, sin_ref, o_ref, s_out_ref, s_ref, *, nc, chunk):
    c = pl.program_id(1)

    @pl.when(c == 0)
    def _():
        s_ref[...] = jnp.zeros_like(s_ref)

    cos, sin = cos_ref[...], sin_ref[...]
    ii = lax.broadcasted_iota(jnp.int32, (chunk, chunk), 0)
    jj = lax.broadcasted_iota(jnp.int32, (chunk, chunk), 1)
    diff = (ii - jj).astype(F32)
    idx = lax.broadcasted_iota(jnp.int32, (chunk, 1), 0).astype(F32)
    heads = range(R_HEADS)
    hs = lambda h: slice(h * R_DK, (h + 1) * R_DK)
    lg = [lg_ref[h] for h in heads]
    q_l = [_rotary(q_ref[:, hs(h)], cos, sin) for h in heads]
    k_l = [_rotary(k_ref[:, hs(h)], cos, sin) * (R_DK ** -0.5) for h in heads]
    v_l = [v_ref[:, hs(h)].astype(BF16) for h in heads]
    s_l = [s_ref[h] for h in heads]
    scores = [_dg(q_l[h].astype(BF16), k_l[h].astype(BF16), _NT)
              * jnp.where(diff >= 0, jnp.exp(lg[h] * jnp.maximum(diff, 0.0)), 0.0) for h in heads]
    cross = [_dg((q_l[h] * jnp.exp(lg[h] * (idx + 1.0))).astype(BF16), s_l[h].astype(BF16), _NN) for h in heads]
    for h in heads:
        o = _dg(scores[h].astype(BF16), v_l[h], _NN) + cross[h]
        o_ref[:, hs(h)] = (_silu(g_ref[:, hs(h)]) * _rms_rows(o)).astype(o_ref.dtype)
    for h in heads:
        k_dec = (k_l[h] * jnp.exp(lg[h] * (chunk - 1.0 - idx))).astype(BF16)
        s_ref[h] = jnp.exp(lg[h] * jnp.full((1, 1), float(chunk), F32)) * s_l[h] + _dg(k_dec, v_l[h], _TN)

    @pl.when(c == nc - 1)
    def _():
        s_out_ref[0] = s_ref[...]


def retention_prompt(p_ret, cos, sin, log_gamma, batch, seq, chunk=R_CHUNK):
    nc = seq // chunk
    bw = BRANCH_WIDTH
    rows = lambda off: pl.BlockSpec((chunk, bw), functools.partial(lambda b, c, off: (b * nc + c, off), off=off))
    tab = pl.BlockSpec((chunk, R_DK // 2), lambda b, c: (c, 0))
    return pl.pallas_call(
        functools.partial(_ret_prompt_kernel, nc=nc, chunk=chunk),
        grid=(batch, nc),
        in_specs=[pl.BlockSpec(memory_space=pltpu.SMEM), rows(0), rows(1), rows(2), rows(3), tab, tab],
        out_specs=[pl.BlockSpec((chunk, bw), lambda b, c: (b * nc + c, 0)),
                   pl.BlockSpec((1, R_HEADS, R_DK, R_DK), lambda b, c: (b, 0, 0, 0))],
        out_shape=[jax.ShapeDtypeStruct((p_ret.shape[0], bw), BF16),
                   jax.ShapeDtypeStruct((batch, R_HEADS, R_DK, R_DK), F32)],
        scratch_shapes=[pltpu.VMEM((R_HEADS, R_DK, R_DK), F32)],
        compiler_params=_cparams(("parallel", "arbitrary")),
        name="ret_prompt",
    )(log_gamma, p_ret, p_ret, p_ret, p_ret, cos, sin)


def _ret_step_kernel(lg_ref, q_ref, k_ref, v_ref, g_ref, cos_ref, sin_ref, s_ref, *rest, bb):
    o_ref, s_out_ref = rest[-2:]
    h = pl.program_id(0)
    gamma = jnp.exp(lg_ref[h] * jnp.ones((1, 1), F32))
    cos, sin = cos_ref[...], sin_ref[...]
    q = _rotary(q_ref[...], cos, sin)
    k = _rotary(k_ref[...], cos, sin) * (R_DK ** -0.5)
    v = v_ref[...]
    eye = _eye(R_DK)
    q_cols = _dot_sel(eye, q, _NT)
    k_cols = _dot_sel(eye, k, _NT)
    for i in range(bb):
        s_new = gamma * s_ref[i, 0] + k_cols[:, i:i + 1] * v[i:i + 1, :]
        s_out_ref[i, 0] = s_new
        o = jnp.sum(q_cols[:, i:i + 1] * s_new, axis=0, keepdims=True)
        o_ref[i:i + 1, :] = (_silu(g_ref[i:i + 1, :]) * _rms_rows(o)).astype(o_ref.dtype)


def _alias_args(first_index, outputs):
    args, specs, aliases = [], [], {}
    for out_idx, a in outputs.items():
        if a is not None:
            aliases[first_index + len(args)] = out_idx
            args.append(a)
            specs.append(pl.BlockSpec(memory_space=pl.ANY))
    return args, specs, aliases


def retention_step(p_ret, row0, cos, sin, log_gamma, states, layer, o_all, s_all, bb=16):
    nb = states.shape[1]
    r0 = row0 // bb
    rows = lambda off: pl.BlockSpec((bb, R_DK), functools.partial(lambda h, j, off: (r0 + j, off + h), off=off))
    tab = pl.BlockSpec((1, R_DK // 2), lambda h, j: (0, 0))
    st = pl.BlockSpec((None, bb, 1, R_DK, R_DK), lambda h, j: (layer, j, h, 0, 0))
    extra, extra_specs, aliases = _alias_args(8, {0: o_all, 1: s_all})
    return pl.pallas_call(
        functools.partial(_ret_step_kernel, bb=bb),
        grid=(R_HEADS, nb // bb),
        in_specs=[pl.BlockSpec(memory_space=pltpu.SMEM), rows(0), rows(R_HEADS), rows(2 * R_HEADS),
                  rows(3 * R_HEADS), tab, tab, st, *extra_specs],
        out_specs=[pl.BlockSpec((bb, R_DK), lambda h, j: (r0 + j, h)), st],
        out_shape=[jax.ShapeDtypeStruct(o_all.shape, BF16), jax.ShapeDtypeStruct(states.shape, F32)],
        input_output_aliases=aliases,
        compiler_params=_cparams(("parallel", "parallel")),
        name="ret_step",
    )(log_gamma, p_ret, p_ret, p_ret, p_ret, cos, sin, states, *extra)


def _cmul(ar, ai, br, bi):
    return ar * br - ai * bi, ar * bi + ai * br


def s5_discretize(lam_re, lam_im, log_dt, b_re, b_im):
    dt = jnp.exp(log_dt)[:, None]
    mag = jnp.exp(lam_re * dt)
    lbr, lbi = mag * jnp.cos(lam_im * dt), mag * jnp.sin(lam_im * dt)
    den = lam_re * lam_re + lam_im * lam_im
    nr, ni = lbr - 1.0, lbi
    cr, ci = (nr * lam_re + ni * lam_im) / den, (ni * lam_re - nr * lam_im) / den
    bbr, bbi = _cmul(cr[..., None], ci[..., None], b_re, b_im)
    return dt, lbr, lbi, bbr, bbi


def _lam_power(lam_re, lam_im, dt, n):
    n = jnp.asarray(n, F32).reshape((-1,) + (1,) * lam_re.ndim)
    mag = jnp.exp(lam_re * dt * n)
    return mag * jnp.cos(lam_im * dt * n), mag * jnp.sin(lam_im * dt * n)


def s5_prompt_params(lam_re, lam_im, log_dt, b_re, b_im, c_re, c_im, d_skip):
    g, p, q, ng = S5_GROUPS, S5_STATE, S5_CHUNK, 128 // S5_GROUP
    nt = g // ng
    dt, lbr, lbi, bbr, bbi = s5_discretize(lam_re, lam_im, log_dt, b_re, b_im)
    pr, pi = _lam_power(lam_re, lam_im, dt, jnp.arange(q + 1))
    lb_r, lb_i = _cmul(pr[:q, :, :, None], pi[:q, :, :, None], bbr[None], bbi[None])
    hp = lax.Precision.HIGHEST
    kern = (jnp.einsum('gkp,tgpc->tgkc', c_re, lb_r, precision=hp)
            - jnp.einsum('gkp,tgpc->tgkc', c_im, lb_i, precision=hp))
    eye = jnp.eye(ng, dtype=F32)

    def block_diag(a):
        return a[:, :, :, :, None, :] * eye[None, None, :, None, :, None]

    k5 = kern[::-1].reshape(q, nt, ng, S5_GROUP, S5_GROUP).transpose(1, 0, 2, 4, 3)
    bd = block_diag(k5).reshape(nt, q * 128, 128)
    er, ei = _cmul(pr[:q][::-1][:, :, :, None], pi[:q][::-1][:, :, :, None], bbr[None], bbi[None])
    e5 = lambda a: a.reshape(q, nt, ng, p, S5_GROUP).transpose(1, 0, 2, 4, 3)
    wb = jnp.concatenate([block_diag(e5(er)).reshape(nt, q * 128, ng * p),
                          block_diag(e5(ei)).reshape(nt, q * 128, ng * p)], axis=-1)
    vr, vi = _cmul(c_re[None], c_im[None], pr[1:, :, None, :], pi[1:, :, None, :])
    v5 = lambda a: a.reshape(q, nt, ng, S5_GROUP, p).transpose(1, 2, 4, 0, 3)

    def rows_gp(a):
        return (a[:, :, :, :, None, :] * eye[None, :, None, None, :, None]).reshape(nt, ng * p, q * 128)

    vb = jnp.concatenate([rows_gp(v5(vr)), rows_gp(v5(-vi))], axis=1)
    sr, si = _lam_power(lam_re, lam_im, dt, q * (2.0 ** jnp.arange(8)))
    bf = lambda a: a.astype(BF16)
    return (bf(bd), bf(wb), bf(vb), sr.reshape(8, g * p), si.reshape(8, g * p), d_skip.reshape(1, g * S5_GROUP))


S5_TILE_GROUPS = 128 // S5_GROUP


def _s5_prompt_kernel(u_ref, bd_ref, wb_ref, vb_ref, lre_ref, lim_ref, d_ref, h_ref, fre_ref, fim_ref, xcat_ref,
                      *, nchb, nb):
    rows = nb * nchb
    q = S5_CHUNK
    ns = S5_TILE_GROUPS * S5_STATE
    for s in range(q):
        xcat_ref[:, s * LANE:(s + 1) * LANE] = u_ref[pl.ds(s, rows, stride=q), :].astype(BF16)
    e = _dg(xcat_ref[...], wb_ref[...], _NN)
    e_re, e_im = e[:, :ns], e[:, ns:]
    pos = jnp.bitwise_and(lax.broadcasted_iota(jnp.int32, (rows, ns), 0), nchb - 1)
    d, k = 1, 0
    while d < nchb:
        keep = pos >= d
        lr, li = lre_ref[k:k + 1, :], lim_ref[k:k + 1, :]
        sr, si = pltpu.roll(e_re, d, 0), pltpu.roll(e_im, d, 0)
        e_re, e_im = (e_re + jnp.where(keep, lr * sr - li * si, 0.0),
                      e_im + jnp.where(keep, lr * si + li * sr, 0.0))
        d, k = d * 2, k + 1
    for b in range(nb):
        r = (b + 1) * nchb - 1
        fre_ref[b:b + 1, :] = e_re[r:r + 1, :]
        fim_ref[b:b + 1, :] = e_im[r:r + 1, :]
    first = pos == 0
    xc = jnp.concatenate([jnp.where(first, 0.0, pltpu.roll(e_re, 1, 0)),
                          jnp.where(first, 0.0, pltpu.roll(e_im, 1, 0))], axis=1).astype(BF16)
    y_carry = _dg(xc, vb_ref[...], _NN)
    dv = d_ref[...]
    for t in range(q):
        y = (y_carry[:, t * LANE:(t + 1) * LANE]
             + _dg(xcat_ref[:, :(t + 1) * LANE], bd_ref[(q - 1 - t) * LANE:, :], _NN)
             + u_ref[pl.ds(t, rows, stride=q), :] * dv)
        h_ref[pl.ds(t, rows, stride=q), :] = _gelu_tanh(y)


def s5_prompt(p_s5, params, layer, batch, seq):
    bd, wb, vb, l_re, l_im, d_vec = params
    q = S5_CHUNK
    ns = S5_TILE_GROUPS * S5_STATE
    nchb = seq // q
    rows = batch * nchb
    tp = batch * seq
    nt = BRANCH_WIDTH // LANE
    tile = lambda a, b: pl.BlockSpec((None, None, a, b), lambda j: (layer, j, 0, 0))
    lam = pl.BlockSpec((None, 8, ns), lambda j: (layer, 0, j))
    fin = pl.BlockSpec((batch, ns), lambda j: (0, j))
    return pl.pallas_call(
        functools.partial(_s5_prompt_kernel, nchb=nchb, nb=batch),
        grid=(nt,),
        in_specs=[pl.BlockSpec((tp, LANE), lambda j: (0, j)), tile(q * LANE, LANE), tile(q * LANE, 2 * ns),
                  tile(2 * ns, q * LANE), lam, lam, pl.BlockSpec((None, 1, LANE), lambda j: (layer, 0, j))],
        out_specs=[pl.BlockSpec((tp, LANE), lambda j: (0, j)), fin, fin],
        out_shape=[jax.ShapeDtypeStruct(p_s5.shape, F32),
                   jax.ShapeDtypeStruct((batch, S5_GROUPS * S5_STATE), F32),
                   jax.ShapeDtypeStruct((batch, S5_GROUPS * S5_STATE), F32)],
        scratch_shapes=[pltpu.VMEM((rows, q * LANE), BF16)],
        compiler_params=_cparams(("parallel",)),
        name="s5_prompt",
    )(p_s5, bd, wb, vb, l_re, l_im, d_vec)


def s5_step_params(lam_re, lam_im, log_dt, b_re, b_im, c_re, c_im, d_skip):
    g, p = S5_GROUPS, S5_STATE
    _, lbr, lbi, bbr, bbi = s5_discretize(lam_re, lam_im, log_dt, b_re, b_im)
    eye8 = jnp.eye(8, dtype=F32)

    def bd_in(bb):
        x = bb.reshape(8, 8, p, S5_GROUP)
        return jnp.einsum('jgpc,gh->jgchp', x, eye8).reshape(8, 8 * S5_GROUP, 8 * p)

    def bd_out(cc):
        x = cc.reshape(8, 8, S5_GROUP, p)
        return jnp.einsum('jgkp,gh->jgphk', x, eye8).reshape(8, 8 * p, 8 * S5_GROUP)

    return (bd_in(bbr), bd_in(bbi), bd_out(c_re), bd_out(-c_im), lbr.reshape(1, g * p), lbi.reshape(1, g * p),
            d_skip.reshape(1, g * S5_GROUP))


def _s5_step_kernel(u_ref, xr_ref, xi_ref, bre_ref, bim_ref, cre_ref, cim_ref, lr_ref, li_ref, d_ref, *rest):
    h_ref, nr_ref, ni_ref = rest[-3:]
    u = u_ref[...]
    lr, li = lr_ref[...], li_ref[...]
    xr, xi = xr_ref[...], xi_ref[...]
    nr = lr * xr - li * xi + _dot3(u, bre_ref[0])
    ni = lr * xi + li * xr + _dot3(u, bim_ref[0])
    nr_ref[...] = nr
    ni_ref[...] = ni
    y = _dot3(nr, cre_ref[0]) + _dot3(ni, cim_ref[0]) + u * d_ref[...]
    h_ref[...] = _gelu_tanh(y)


def s5_step(p_s5, row0, x_re, x_im, params, layer, h_all):
    b_re, b_im, c_re, c_im, l_re, l_im, d_vec = params
    nb, gp = x_re.shape
    lt = 8 * S5_GROUP
    st = 8 * S5_STATE
    r0 = row0 // nb
    u_spec = pl.BlockSpec((nb, lt), lambda j: (r0, j))
    x_spec = pl.BlockSpec((nb, st), lambda j: (0, j))
    mat = lambda a, b: pl.BlockSpec((None, 1, a, b), lambda j: (layer, j, 0, 0))
    vec = lambda n: pl.BlockSpec((None, 1, n), lambda j: (layer, 0, j))
    extra, extra_specs, aliases = _alias_args(10, {0: h_all})
    return pl.pallas_call(
        _s5_step_kernel,
        grid=(gp // st,),
        in_specs=[u_spec, x_spec, x_spec, mat(lt, st), mat(lt, st), mat(st, lt), mat(st, lt),
                  vec(st), vec(st), vec(lt), *extra_specs],
        out_specs=[u_spec, x_spec, x_spec],
        out_shape=[jax.ShapeDtypeStruct(h_all.shape, F32), jax.ShapeDtypeStruct((nb, gp), F32),
                   jax.ShapeDtypeStruct((nb, gp), F32)],
        input_output_aliases=aliases,
        compiler_params=_cparams(("parallel",)),
        name="s5_step",
    )(p_s5, x_re, x_im, b_re, b_im, c_re, c_im, l_re, l_im, d_vec, *extra)


def _l2norm_rows(x):
    return x * lax.rsqrt(jnp.sum(x * x, axis=-1, keepdims=True) + NORM_EPS)


def _gdn_prompt_kernel(qkv_ref, z_ref, ab_ref, cw_ref, alog_ref, dtb_ref, nw_ref, o_ref, s_out_ref,
                       xbuf_ref, s_ref, m_s, sol_s, qe_s, qkd_s, kdec_s, egl_s, *, nc, chunk):
    c = pl.program_id(1)
    bw = BRANCH_WIDTH
    halo = 8
    heads = range(G_HEADS)

    @pl.when(c == 0)
    def _():
        xbuf_ref[0:halo, :] = jnp.zeros((halo, 3 * bw), F32)
        s_ref[...] = jnp.zeros_like(s_ref)
        for ref in (m_s, sol_s, qe_s, qkd_s, kdec_s, egl_s):
            ref[...] = jnp.zeros_like(ref)

    m_l = [m_s[h] for h in heads]
    sol_l = [sol_s[h] for h in heads]
    qe_l = [qe_s[h] for h in heads]
    qkd_l = [qkd_s[h] for h in heads]
    kdec_l = [kdec_s[h] for h in heads]
    egl_l = [egl_s[h] for h in heads]

    x = qkv_ref[...]
    xbuf_ref[halo:halo + chunk, :] = x
    conv = x * cw_ref[G_CONV - 1:G_CONV, :]
    for i in range(1, G_CONV):
        conv = conv + xbuf_ref[halo - i:halo - i + chunk, :] * cw_ref[G_CONV - 1 - i:G_CONV - i, :]
    xbuf_ref[0:halo, :] = x[chunk - halo:chunk, :]
    qkv = _silu(conv)

    ab = ab_ref[...]
    g_all = -jnp.exp(alog_ref[...]) * _softplus(ab + dtb_ref[...])
    beta_all = _sigmoid(ab)
    ri = lax.broadcasted_iota(jnp.int32, (chunk, chunk), 0)
    ci = lax.broadcasted_iota(jnp.int32, (chunk, chunk), 1)
    causal = ri >= ci
    strict = ri > ci
    ltri = jnp.where(causal, 1.0, 0.0).astype(BF16)
    gcum = _dot_sel(ltri, g_all)
    gcum_t = _dot_sel(_eye(128), gcum, _NT)
    z = z_ref[...]
    nw = nw_ref[...]

    def prepare(h):
        q = _l2norm_rows(qkv[:, h * G_DK:(h + 1) * G_DK]) * (G_DK ** -0.5)
        k = _l2norm_rows(qkv[:, bw + h * G_DK:bw + (h + 1) * G_DK])
        v = qkv[:, 2 * bw + h * G_DK:2 * bw + (h + 1) * G_DK]
        gc = gcum[:, h:h + 1]
        gr = gcum_t[h:h + 1, :]
        beta = beta_all[:, G_HEADS + h:G_HEADS + h + 1]
        decay = jnp.where(causal, jnp.exp(jnp.where(causal, gc - gr, 0.0)), 0.0)
        kb = k.astype(BF16)
        qk_kk = _dg(jnp.concatenate([kb, q.astype(BF16)], axis=0), kb, _NT)
        eg = jnp.exp(gc)
        g_last = gc[chunk - 1:chunk, :]
        qe_s[h] = (q * eg).astype(BF16)
        qkd_s[h] = (qk_kk[chunk:] * decay).astype(BF16)
        kdec_s[h] = (k * jnp.exp(g_last - gc)).astype(BF16)
        egl_s[h] = jnp.broadcast_to(jnp.exp(g_last), (1, G_DK))
        m_s[h] = jnp.where(strict, -(beta * qk_kk[:chunk] * decay), 0.0)
        sol_s[h] = jnp.concatenate([beta * v, (beta * eg) * k], axis=-1)

    todo = list(heads)
    span = 1
    while span < chunk:
        for h in heads:
            mb = m_l[h].astype(BF16)
            if 2 * span < chunk:
                sol_hi, sol_lo = _split2(sol_l[h])
                r = _dg(mb, jnp.concatenate([sol_hi, mb], axis=-1), _NN)
                sol_l[h] = sol_l[h] + (r[:, :2 * G_DK] + _dg(mb, sol_lo, _NN))
                m_l[h] = r[:, 2 * G_DK:]
            else:
                sol_l[h] = sol_l[h] + _dot2(mb, sol_l[h])
        prepare(todo.pop(0))
        span *= 2
    s_l = [s_ref[h] for h in heads]
    ks_qs_l = [_dg(jnp.concatenate([sol_l[h][:, G_DK:].astype(BF16), qe_l[h]], axis=0), s_l[h].astype(BF16), _NN)
               for h in heads]
    wb_l = [(sol_l[h][:, :G_DK] - ks_qs_l[h][:chunk]).astype(BF16) for h in heads]
    while todo:
        prepare(todo.pop(0))
    for h in heads:
        o = ks_qs_l[h][chunk:] + _dg(qkd_l[h], wb_l[h], _NN)
        zh = z[:, h * G_DK:(h + 1) * G_DK]
        o_ref[:, h * G_DK:(h + 1) * G_DK] = (_rms_rows(o) * nw * _silu(zh)).astype(o_ref.dtype)
    for h in heads:
        s_ref[h] = egl_l[h] * s_l[h] + _dg(kdec_l[h], wb_l[h], _TN)

    @pl.when(c == nc)
    def _():
        s_out_ref[0] = s_ref[...]


def gdn_prompt(p_qkv, p_z, p_ab, conv_w, a_log, dt_bias, norm_w, batch, seq, chunk=G_CHUNK):
    nc = seq // chunk
    bw = BRANCH_WIDTH
    pad = lambda x: jnp.zeros((1, 128), F32).at[0, :G_HEADS].set(x)
    full = lambda shape: pl.BlockSpec(shape, lambda b, c: (0,) * len(shape))
    cur = lambda width: pl.BlockSpec((chunk, width), lambda b, c: (b * nc + jnp.minimum(c, nc - 1), 0))
    prev = lambda width: pl.BlockSpec((chunk, width), lambda b, c: (b * nc + jnp.maximum(c - 1, 0), 0))
    hd = lambda width, dtype: pltpu.VMEM((G_HEADS, chunk, width), dtype)
    return pl.pallas_call(
        functools.partial(_gdn_prompt_kernel, nc=nc, chunk=chunk),
        grid=(batch, nc + 1),
        in_specs=[cur(3 * bw), prev(bw), cur(128),
                  full((G_CONV, 3 * bw)), full((1, 128)), full((1, 128)), full((1, G_DK))],
        out_specs=[prev(bw), pl.BlockSpec((1, G_HEADS, G_DK, G_DK), lambda b, c: (b, 0, 0, 0))],
        out_shape=[jax.ShapeDtypeStruct((p_qkv.shape[0], bw), BF16),
                   jax.ShapeDtypeStruct((batch, G_HEADS, G_DK, G_DK), F32)],
        scratch_shapes=[pltpu.VMEM((8 + chunk, 3 * bw), F32), pltpu.VMEM((G_HEADS, G_DK, G_DK), F32),
                        hd(chunk, F32), hd(2 * G_DK, F32), hd(G_DK, BF16), hd(chunk, BF16), hd(G_DK, BF16),
                        pltpu.VMEM((G_HEADS, 1, G_DK), F32)],
        compiler_params=_cparams(("parallel", "arbitrary")),
        name="gdn_prompt",
    )(p_qkv, p_z, p_ab, conv_w, pad(a_log), pad(dt_bias), norm_w.reshape(1, G_DK))


def _gdn_step_kernel(qkv_ref, z_ref, ab_ref, cbuf_ref, cw_ref, alog_ref, dtb_ref, nw_ref, s_ref, *rest, bb):
    o_ref, cbuf_out_ref, s_out_ref = rest[-3:]
    bw = BRANCH_WIDTH
    x = qkv_ref[...]
    conv = x * cw_ref[G_CONV - 1:G_CONV, :]
    for i in range(G_CONV - 1):
        conv = conv + cbuf_ref[i] * cw_ref[i:i + 1, :]
    for i in range(G_CONV - 2):
        cbuf_out_ref[i] = cbuf_ref[i + 1]
    cbuf_out_ref[G_CONV - 2] = x
    qkv = _silu(conv)
    ab = ab_ref[...]
    g_all = -jnp.exp(alog_ref[...]) * _softplus(ab + dtb_ref[...])
    beta_all = _sigmoid(ab)
    eye = _eye(G_DK)
    z = z_ref[...]
    nw = nw_ref[...]
    for h in range(G_HEADS):
        q = _l2norm_rows(qkv[:, h * G_DK:(h + 1) * G_DK]) * (G_DK ** -0.5)
        k = _l2norm_rows(qkv[:, bw + h * G_DK:bw + (h + 1) * G_DK])
        v = qkv[:, 2 * bw + h * G_DK:2 * bw + (h + 1) * G_DK]
        eg = jnp.exp(g_all[:, h:h + 1])
        beta = beta_all[:, G_HEADS + h:G_HEADS + h + 1]
        q_cols = _dot_sel(eye, q, _NT)
        k_cols = _dot_sel(eye, k, _NT)
        rng = range(bb)
        s_l = [s_ref[i, h] for i in rng]
        ks_l = [jnp.sum(k_cols[:, i:i + 1] * s_l[i], axis=0, keepdims=True) for i in rng]
        w_l = [beta[i:i + 1, :] * (v[i:i + 1, :] - eg[i:i + 1, :] * ks_l[i]) for i in rng]
        sn_l = [eg[i:i + 1, :] * s_l[i] + k_cols[:, i:i + 1] * w_l[i] for i in rng]
        for i in rng:
            s_out_ref[i, h] = sn_l[i]
        o = jnp.concatenate([jnp.sum(q_cols[:, i:i + 1] * sn_l[i], axis=0, keepdims=True) for i in rng], axis=0)
        zh = z[:, h * G_DK:(h + 1) * G_DK]
        o_ref[:, h * G_DK:(h + 1) * G_DK] = (_rms_rows(o) * nw * _silu(zh)).astype(o_ref.dtype)


def gdn_step(p_qkv, p_z, p_ab, row0, conv_buf_t, states, layer, conv_w, a_log, dt_bias, norm_w, o_all, s_all, bb=8):
    nb = states.shape[1]
    bw = BRANCH_WIDTH
    r0 = row0 // bb
    pad = lambda x: jnp.zeros((1, 128), F32).at[0, :G_HEADS].set(x)
    full = lambda shape: pl.BlockSpec(shape, lambda j: (0,) * len(shape))
    cb = pl.BlockSpec((G_CONV - 1, bb, 3 * bw), lambda j: (0, j, 0))
    st = pl.BlockSpec((None, bb, G_HEADS, G_DK, G_DK), lambda j: (layer, j, 0, 0, 0))
    extra, extra_specs, aliases = _alias_args(9, {0: o_all, 2: s_all})
    return pl.pallas_call(
        functools.partial(_gdn_step_kernel, bb=bb),
        grid=(nb // bb,),
        in_specs=[pl.BlockSpec((bb, 3 * bw), lambda j: (r0 + j, 0)),
                  pl.BlockSpec((bb, bw), lambda j: (r0 + j, 0)),
                  pl.BlockSpec((bb, 128), lambda j: (r0 + j, 0)),
                  cb, full((G_CONV, 3 * bw)), full((1, 128)), full((1, 128)), full((1, G_DK)), st, *extra_specs],
        out_specs=[pl.BlockSpec((bb, bw), lambda j: (r0 + j, 0)), cb, st],
        out_shape=[jax.ShapeDtypeStruct(o_all.shape, BF16), jax.ShapeDtypeStruct(conv_buf_t.shape, F32),
                   jax.ShapeDtypeStruct(states.shape, F32)],
        input_output_aliases=aliases,
        compiler_params=_cparams(("parallel",)),
        name="gdn_step",
    )(p_qkv, p_z, p_ab, conv_buf_t, conv_w, pad(a_log), pad(dt_bias), norm_w.reshape(1, G_DK), states, *extra)


def _rope_tables(pos):
    half = R_DK // 2
    inv = ROPE_BASE ** (-jnp.arange(half, dtype=F32) / half)
    ang = pos.astype(F32)[:, None] * inv[None, :]
    return jnp.cos(ang), jnp.sin(ang)


def kernel(x_prompt, x_sample, state_ret, state_s5_re, state_s5_im, state_gdn, state_gdn_conv, norm_mix_w, w_in, s5_lambda_re, s5_lambda_im, s5_log_dt, s5_b_re, s5_b_im, s5_c_re, s5_c_im, s5_d, s5_w_glu, s5_b_glu, gdn_conv_w, gdn_a_log, gdn_dt_bias, gdn_norm_w, w_branch, w_out, norm_ffn_w, w_gate_ffn, w_up_ffn, w_down_ffn, norm_final_w):
    bp, lp, d = x_prompt.shape
    nb = x_sample.shape[0]
    depth = w_in.shape[0]
    tp = bp * lp
    bw = BRANCH_WIDTH
    x = jnp.concatenate([x_prompt.reshape(tp, d), x_sample.reshape(nb, d)], axis=0)

    cos_p, sin_p = _rope_tables(jnp.arange(lp, dtype=jnp.int32))
    cos_s, sin_s = _rope_tables(PAST_LEN + jnp.arange(1, dtype=jnp.int32))
    log_gamma = jnp.log1p(-jnp.exp2(-5.0 - jnp.arange(R_HEADS, dtype=F32)))

    o_su, o_gqkv, o_ga, o_gz, o_gates = 4 * bw, 5 * bw, 8 * bw, 8 * bw + 2 * G_HEADS, 9 * bw + 2 * G_HEADS
    w_in_t = jnp.swapaxes(w_in, 1, 2)
    w_in_b = w_in_t[0].astype(BF16)[None]
    w_gate_b, w_up_b, w_down_b = w_gate_ffn[0].astype(BF16), w_up_ffn[0].astype(BF16), w_down_ffn[0].astype(BF16)
    w_branch_b, w_out_b, w_glu_b = w_branch.astype(BF16), w_out.astype(BF16), s5_w_glu.astype(BF16)
    s5_raw = (s5_lambda_re, s5_lambda_im, s5_log_dt, s5_b_re, s5_b_im, s5_c_re, s5_c_im, s5_d)
    s5_pp = jax.vmap(s5_prompt_params)(*s5_raw)
    s5_sp = jax.vmap(s5_step_params)(*s5_raw)

    new_p = [[], [], [], [], []]
    new_s = [[], [], []]
    ret_s_all = gdn_s_all = None
    xn = rmsnorm_rows(x, norm_mix_w[0], BF16)
    y_final = None
    for l in range(depth):
        p_ret = matmul(xn, w_in_b, 0, 0, o_su, tn=1024, name="proj_ret")
        p_s5 = matmul(xn, w_in_b, 0, o_su, bw, tn=1024, name="proj_s5")
        p_qkv = matmul(xn, w_in_b, 0, o_gqkv, 3 * bw, tn=1024, name="proj_qkv")
        p_ab = matmul(xn, w_in_b, 0, o_ga, LANE, tn=LANE, name="proj_ab")
        p_z = matmul(xn, w_in_b, 0, o_gz, bw, tn=512, name="proj_z")

        r_all, ret_p = retention_prompt(p_ret, cos_p, sin_p, log_gamma, bp, lp)
        r_all, ret_s_all = retention_step(p_ret, tp, cos_s, sin_s, log_gamma, state_ret, l, r_all, ret_s_all)

        h_all, f_re, f_im = s5_prompt(p_s5, s5_pp, l, bp, lp)
        fin = lambda f: f.reshape(bp, S5_GROUPS, S5_STATE)
        h_all, s5re_s, s5im_s = s5_step(p_s5, tp, state_s5_re[l].reshape(nb, -1), state_s5_im[l].reshape(nb, -1),
                                        s5_sp, l, h_all)
        s_all = s5_glu(h_all, w_glu_b, l, s5_b_glu[l])

        g_all, gdn_p = gdn_prompt(p_qkv, p_z, p_ab, gdn_conv_w[l], gdn_a_log[l], gdn_dt_bias[l], gdn_norm_w[l], bp, lp)
        g_all, conv_s_t, gdn_s_all = gdn_step(p_qkv, p_z, p_ab, tp, state_gdn_conv[l].transpose(1, 0, 2), state_gdn, l,
                                              gdn_conv_w[l], gdn_a_log[l], gdn_dt_bias[l], gdn_norm_w[l],
                                              g_all, gdn_s_all)
        conv_p = jnp.stack([p_qkv[(b + 1) * lp - (G_CONV - 1):(b + 1) * lp] for b in range(bp)], axis=0)

        merged = merge_branches(xn, r_all, s_all, g_all, w_in_b, o_gates, w_branch_b, l)
        h, hn = out_proj(merged, w_out_b, l, x, norm_ffn_w[l])
        last = l == depth - 1
        if last:
            x, y_final = ffn(hn, h, w_gate_b, w_up_b, w_down_b, norm_final_w, F32)
        else:
            x, xn, w_gate_b, w_up_b, w_down_b, w_in_next = ffn(
                hn, h, w_gate_b, w_up_b, w_down_b, norm_mix_w[l + 1], BF16,
                cast_next=(l + 1, w_gate_ffn, w_up_ffn, w_down_ffn, w_in_t))
            w_in_b = w_in_next[None]

        for lst, val in zip(new_p, (ret_p, fin(f_re), fin(f_im), gdn_p, conv_p)):
            lst.append(val)
        for lst, val in zip(new_s, (s5re_s.reshape(nb, S5_GROUPS, S5_STATE),
                                    s5im_s.reshape(nb, S5_GROUPS, S5_STATE), conv_s_t.transpose(1, 0, 2))):
            lst.append(val)

    y_prompt = y_final[:tp].reshape(bp, lp, d)
    y_sample = y_final[tp:].reshape(nb, 1, d)
    outs_p = [jnp.stack(a, 0) for a in new_p]
    s5re_s, s5im_s, conv_s = [jnp.stack(a, 0) for a in new_s]
    return (y_prompt, y_sample, *outs_p, ret_s_all, s5re_s, s5im_s, gdn_s_all, conv_s)
```

```python
import functools
import math

import jax
import jax.numpy as jnp
from jax import lax
from jax.experimental import pallas as pl
from jax.experimental.pallas import tpu as pltpu

F32 = jnp.float32
BF16 = jnp.bfloat16

D_MODEL = 2048
BRANCH_WIDTH = 1024
N_BRANCH = 3
R_HEADS = 4
R_DK = 256
R_CHUNK = 128
ROPE_BASE = 10000.0
S5_GROUP = 16
S5_GROUPS = 64
S5_STATE = 64
S5_CHUNK = 16
G_HEADS = 8
G_DK = 128
G_CONV = 4
G_CHUNK = 64
NORM_EPS = 1e-6
PAST_LEN = 16384

VMEM_LIMIT_BYTES = 56 * 1024 * 1024
TM = 640


def _cparams(sem):
    return pltpu.CompilerParams(dimension_semantics=sem, vmem_limit_bytes=VMEM_LIMIT_BYTES)


_NN = (((1,), (0,)), ((), ()))
_NT = (((1,), (1,)), ((), ()))
_TN = (((0,), (0,)), ((), ()))


def _dg(a, b, dims):
    return lax.dot_general(a, b, dims, preferred_element_type=F32)


def _split2(a):
    hi = a.astype(BF16)
    lo = (a - hi.astype(F32)).astype(BF16)
    return hi, lo


def _dot3(a, b, dims=_NN):
    ah, al = _split2(a)
    bh, bl = _split2(b)
    return _dg(ah, bh, dims) + (_dg(ah, bl, dims) + _dg(al, bh, dims))


def _dot2(a_bf16, b, dims=_NN):
    bh, bl = _split2(b)
    return _dg(a_bf16, bh, dims) + _dg(a_bf16, bl, dims)


def _dot_sel(sel_bf16, x, dims=_NN):
    h1 = x.astype(BF16)
    r1 = x - h1.astype(F32)
    h2 = r1.astype(BF16)
    h3 = (r1 - h2.astype(F32)).astype(BF16)
    return _dg(sel_bf16, h1, dims) + (_dg(sel_bf16, h2, dims) + _dg(sel_bf16, h3, dims))


def _sel_dot(x, sel_bf16, dims=_NN):
    h1 = x.astype(BF16)
    r1 = x - h1.astype(F32)
    h2 = r1.astype(BF16)
    h3 = (r1 - h2.astype(F32)).astype(BF16)
    return _dg(h1, sel_bf16, dims) + (_dg(h2, sel_bf16, dims) + _dg(h3, sel_bf16, dims))


def _eye(n, dtype=BF16):
    r = lax.broadcasted_iota(jnp.int32, (n, n), 0)
    c = lax.broadcasted_iota(jnp.int32, (n, n), 1)
    return jnp.where(r == c, 1.0, 0.0).astype(dtype)


def _sigmoid(x):
    return 1.0 / (1.0 + jnp.exp(-x))


def _silu(x):
    return x * _sigmoid(x)


def _gelu_tanh(x):
    return 0.5 * x * (1.0 + jnp.tanh(math.sqrt(2.0 / math.pi) * (x + 0.044715 * (x * x * x))))


def _softplus(x):
    return jnp.maximum(x, 0.0) + jnp.log1p(jnp.exp(-jnp.abs(x)))


def _rms_rows(x, eps=NORM_EPS):
    return x * lax.rsqrt(jnp.mean(x * x, axis=-1, keepdims=True) + eps)


def _norm_kernel(x_ref, w_ref, o_ref):
    o_ref[...] = (_rms_rows(x_ref[...]) * w_ref[...]).astype(o_ref.dtype)


def rmsnorm_rows(x, w, out_dtype, tm=TM):
    t, d = x.shape
    return pl.pallas_call(
        _norm_kernel,
        grid=(t // tm,),
        in_specs=[pl.BlockSpec((tm, d), lambda i: (i, 0)), pl.BlockSpec((1, d), lambda i: (0, 0))],
        out_specs=pl.BlockSpec((tm, d), lambda i: (i, 0)),
        out_shape=jax.ShapeDtypeStruct((t, d), out_dtype),
        compiler_params=_cparams(("parallel",)),
        name="rmsnorm",
    )(x, w.reshape(1, d))


LANE = 128
ROW_ALIGN = 16


def _shifted_rows(w_ref, wn_ref, shift):
    if shift == 0:
        return w_ref[...]
    return jnp.concatenate([w_ref[shift:, :], wn_ref[:shift, :]], axis=0)


def _shifted_specs(k, tn, layer, row0, nidx):
    base, shift = (row0 // tn) * tn, row0 % tn
    assert shift % ROW_ALIGN == 0 and tn % max(shift, ROW_ALIGN) == 0
    rows = max(shift, ROW_ALIGN)
    main = pl.BlockSpec((None, tn, k), lambda i, j: (layer, base // tn + nidx(i, j), 0))
    nxt = pl.BlockSpec((None, rows, k), lambda i, j: (layer, (base + (nidx(i, j) + 1) * tn) // rows, 0))
    return main, nxt, shift


def _mm_kernel(a_ref, w_ref, wn_ref, o_ref, *, shift):
    o_ref[...] = _dg(a_ref[...], _shifted_rows(w_ref, wn_ref, shift), _NT)


def matmul(a, wt, layer, row0, n, tn, tm=TM, name="proj"):
    t, k = a.shape
    main, nxt, shift = _shifted_specs(k, tn, layer, row0, lambda i, j: j)
    return pl.pallas_call(
        functools.partial(_mm_kernel, shift=shift),
        grid=(t // tm, n // tn),
        in_specs=[pl.BlockSpec((tm, k), lambda i, j: (i, 0)), main, nxt],
        out_specs=pl.BlockSpec((tm, tn), lambda i, j: (i, j)),
        out_shape=jax.ShapeDtypeStruct((t, n), F32),
        compiler_params=_cparams(("parallel", "arbitrary")),
        name=name,
    )(a, wt, wt)


def _mm_side_kernel(a_ref, w_ref, wn_ref, ws_ref, o_ref, os_ref, *, shift):
    a = a_ref[...]
    o_ref[...] = _dg(a, _shifted_rows(w_ref, wn_ref, shift), _NT)

    @pl.when(pl.program_id(1) == 0)
    def _():
        os_ref[...] = _dg(a, ws_ref[...], _NT)


def matmul_with_side(a, wt, layer, row0, n, tn, side_row0, side_n, tm=TM, name="proj"):
    t, k = a.shape
    main, nxt, shift = _shifted_specs(k, tn, layer, row0, lambda i, j: j)
    return pl.pallas_call(
        functools.partial(_mm_side_kernel, shift=shift),
        grid=(t // tm, n // tn),
        in_specs=[pl.BlockSpec((tm, k), lambda i, j: (i, 0)), main, nxt,
                  pl.BlockSpec((None, side_n, k), lambda i, j: (layer, side_row0 // side_n, 0))],
        out_specs=[pl.BlockSpec((tm, tn), lambda i, j: (i, j)), pl.BlockSpec((tm, side_n), lambda i, j: (i, 0))],
        out_shape=[jax.ShapeDtypeStruct((t, n), F32), jax.ShapeDtypeStruct((t, side_n), F32)],
        compiler_params=_cparams(("parallel", "arbitrary")),
        name=name,
    )(a, wt, wt, wt)


def _cast_kernel(s_ref, d_ref):
    d_ref[...] = s_ref[...].astype(d_ref.dtype)


def cast_layer(w, layer, rows, cols):
    _, r, c = w.shape
    return pl.pallas_call(
        _cast_kernel,
        grid=(r // rows, c // cols),
        in_specs=[pl.BlockSpec((None, rows, cols), lambda i, j: (layer, i, j))],
        out_specs=pl.BlockSpec((rows, cols), lambda i, j: (i, j)),
        out_shape=jax.ShapeDtypeStruct((r, c), BF16),
        compiler_params=_cparams(("parallel", "parallel")),
        name="cast",
    )(w)


def _merge_kernel(xn_ref, br_r, br_s, br_g, wg0, wn0, wg1, wn1, wg2, wn2, wb_ref, o_ref, *, shift):
    xn = xn_ref[...]
    acc = None
    for i, (br, wg, wn) in enumerate(((br_r, wg0, wn0), (br_s, wg1, wn1), (br_g, wg2, wn2))):
        gate = _sigmoid(_dg(xn, _shifted_rows(wg, wn, shift), _NT))
        up = jnp.dot(br[...], wb_ref[i], preferred_element_type=F32)
        acc = gate * up if acc is None else acc + gate * up
    o_ref[...] = acc.astype(o_ref.dtype)


def merge_branches(xn, br_r, br_s, br_g, w_in, gate_col0, w_branch, layer, tn=512, tm=TM):
    t, d = xn.shape
    bw = br_r.shape[1]
    nj = d // tn
    a_spec = pl.BlockSpec((tm, bw), lambda i, j: (i, 0))
    wg_specs = []
    for n in range(N_BRANCH):
        main, nxt, shift = _shifted_specs(d, tn, 0, gate_col0, functools.partial(lambda i, j, n: n * nj + j, n=n))
        wg_specs += [main, nxt]
    return pl.pallas_call(
        functools.partial(_merge_kernel, shift=shift),
        grid=(t // tm, nj),
        in_specs=[pl.BlockSpec((tm, d), lambda i, j: (i, 0)), a_spec, a_spec, a_spec, *wg_specs,
                  pl.BlockSpec((None, N_BRANCH, bw, tn), lambda i, j: (layer, 0, 0, j))],
        out_specs=pl.BlockSpec((tm, tn), lambda i, j: (i, j)),
        out_shape=jax.ShapeDtypeStruct((t, d), BF16),
        compiler_params=_cparams(("parallel", "arbitrary")),
        name="merge",
    )(xn, br_r, br_s, br_g, *([w_in] * (2 * N_BRANCH)), w_branch)


def _outproj_kernel(m_ref, w_ref, x_ref, nw_ref, h_ref, hn_ref):
    h = x_ref[...] + jnp.dot(m_ref[...], w_ref[...], preferred_element_type=F32)
    h_ref[...] = h
    hn_ref[...] = (_rms_rows(h) * nw_ref[...]).astype(hn_ref.dtype)


def out_proj(merged, w_out, layer, x, norm_w, tm=TM):
    t, d = x.shape
    row = pl.BlockSpec((tm, d), lambda i: (i, 0))
    return pl.pallas_call(
        _outproj_kernel,
        grid=(t // tm,),
        in_specs=[row, pl.BlockSpec((None, d, d), lambda i: (layer, 0, 0)), row,
                  pl.BlockSpec((1, d), lambda i: (0, 0))],
        out_specs=[row, row],
        out_shape=[jax.ShapeDtypeStruct((t, d), F32), jax.ShapeDtypeStruct((t, d), BF16)],
        compiler_params=_cparams(("parallel",)),
        name="outproj",
    )(merged, w_out, x, norm_w.reshape(1, d))


def _ffn_kernel(hn_ref, h_ref, wg_ref, wu_ref, wd_ref, nw_ref, *rest, nf, n_cast):
    src, (y_ref, xn_ref), dst = rest[:n_cast], rest[n_cast:n_cast + 2], rest[n_cast + 2:]
    f = pl.program_id(1)

    @pl.when(f == 0)
    def _():
        y_ref[...] = h_ref[...]

    hn = hn_ref[...]
    g = jnp.dot(hn, wg_ref[...], preferred_element_type=F32)
    u = jnp.dot(hn, wu_ref[...], preferred_element_type=F32)
    act = (_silu(g) * u).astype(BF16)
    y_ref[...] += jnp.dot(act, wd_ref[...], preferred_element_type=F32)
    for s_ref, d_ref in zip(src, dst):
        d_ref[...] = s_ref[...].astype(d_ref.dtype)

    @pl.when(f == nf - 1)
    def _():
        xn_ref[...] = (_rms_rows(y_ref[...]) * nw_ref[...]).astype(xn_ref.dtype)


CAST_ROWS = 256
CAST_COLS_IN = 512


def ffn(hn, h, w_gate, w_up, w_down, next_norm_w, xn_dtype, cast_next=None, tf=512, tm=TM):
    t, d = h.shape
    dff = w_gate.shape[1]
    nf = dff // tf
    ni = t // tm
    row = pl.BlockSpec((tm, d), lambda i, f: (i, 0))
    in_specs = [row, row, pl.BlockSpec((d, tf), lambda i, f: (0, f)), pl.BlockSpec((d, tf), lambda i, f: (0, f)),
                pl.BlockSpec((tf, d), lambda i, f: (f, 0)), pl.BlockSpec((1, d), lambda i, f: (0, 0))]
    out_specs = [row, row]
    out_shape = [jax.ShapeDtypeStruct((t, d), F32), jax.ShapeDtypeStruct((t, d), xn_dtype)]
    args = [hn, h, w_gate, w_up, w_down, next_norm_w.reshape(1, d)]
    n_cast = 0
    if cast_next is not None:
        nl, src_g, src_u, src_d, src_in = cast_next
        n_in = src_in.shape[1]
        nrb = ni * nf // (d // CAST_COLS_IN)
        rows_in = -(-n_in // nrb)
        rows_in += (-rows_in) % ROW_ALIGN
        while n_in % rows_in:
            rows_in += ROW_ALIGN
        ncb = d // CAST_COLS_IN
        nwb = (d // CAST_ROWS) * nf
        nib = (n_in // rows_in) * ncb
        assert nwb <= ni * nf and nib <= ni * nf
        wb = lambda i, f: jnp.minimum(i * nf + f, nwb - 1)
        ib = lambda i, f: jnp.minimum(i * nf + f, nib - 1)
        in_specs += [pl.BlockSpec((None, CAST_ROWS, tf), lambda i, f: (nl, wb(i, f) // nf, wb(i, f) % nf)),
                     pl.BlockSpec((None, CAST_ROWS, tf), lambda i, f: (nl, wb(i, f) // nf, wb(i, f) % nf)),
                     pl.BlockSpec((None, tf, CAST_ROWS), lambda i, f: (nl, wb(i, f) % nf, wb(i, f) // nf)),
                     pl.BlockSpec((None, rows_in, CAST_COLS_IN), lambda i, f: (nl, ib(i, f) // ncb, ib(i, f) % ncb))]
        out_specs += [pl.BlockSpec((CAST_ROWS, tf), lambda i, f: (wb(i, f) // nf, wb(i, f) % nf)),
                      pl.BlockSpec((CAST_ROWS, tf), lambda i, f: (wb(i, f) // nf, wb(i, f) % nf)),
                      pl.BlockSpec((tf, CAST_ROWS), lambda i, f: (wb(i, f) % nf, wb(i, f) // nf)),
                      pl.BlockSpec((rows_in, CAST_COLS_IN), lambda i, f: (ib(i, f) // ncb, ib(i, f) % ncb))]
        out_shape += [jax.ShapeDtypeStruct((d, dff), BF16), jax.ShapeDtypeStruct((d, dff), BF16),
                      jax.ShapeDtypeStruct((dff, d), BF16), jax.ShapeDtypeStruct((n_in, d), BF16)]
        args += [src_g, src_u, src_d, src_in]
        n_cast = 4
    return pl.pallas_call(
        functools.partial(_ffn_kernel, nf=nf, n_cast=n_cast),
        grid=(ni, nf),
        in_specs=in_specs,
        out_specs=out_specs,
        out_shape=out_shape,
        compiler_params=_cparams(("arbitrary", "arbitrary")),
        name="ffn",
    )(*args)


def _glu_kernel(h_ref, w_ref, b_ref, o_ref):
    h = h_ref[...]
    z = jnp.dot(h.astype(BF16), w_ref[...], preferred_element_type=F32) + b_ref[...]
    o_ref[...] = (h * _sigmoid(z)).astype(o_ref.dtype)


def s5_glu(h, w_glu, layer, b_glu, tm=TM):
    t, d = h.shape
    row = pl.BlockSpec((tm, d), lambda i: (i, 0))
    return pl.pallas_call(
        _glu_kernel,
        grid=(t // tm,),
        in_specs=[row, pl.BlockSpec((None, d, d), lambda i: (layer, 0, 0)), pl.BlockSpec((1, d), lambda i: (0, 0))],
        out_specs=row,
        out_shape=jax.ShapeDtypeStruct((t, d), BF16),
        compiler_params=_cparams(("parallel",)),
        name="s5_glu",
    )(h, w_glu, b_glu.reshape(1, d))


def _rotary(x, cos, sin):
    half = x.shape[-1] // 2
    x1, x2 = x[:, :half], x[:, half:]
    return jnp.concatenate([x1 * cos - x2 * sin, x1 * sin + x2 * cos], axis=-1)


def _ret_prompt_kernel(lg_ref, q_ref, k_ref, v_ref, g_ref, cos_ref, sin_ref, o_ref, s_out_ref, s_ref, *, nc, chunk):
    c = pl.program_id(1)

    @pl.when(c == 0)
    def _():
        s_ref[...] = jnp.zeros_like(s_ref)

    cos, sin = cos_ref[...], sin_ref[...]
    ii = lax.broadcasted_iota(jnp.int32, (chunk, chunk), 0)
    jj = lax.broadcasted_iota(jnp.int32, (chunk, chunk), 1)
    diff = (ii - jj).astype(F32)
    idx = lax.broadcasted_iota(jnp.int32, (chunk, 1), 0).astype(F32)
    heads = range(R_HEADS)
    hs = lambda h: slice(h * R_DK, (h + 1) * R_DK)
    lg = [lg_ref[h] for h in heads]
    q_l = [_rotary(q_ref[:, hs(h)], cos, sin) for h in heads]
    k_l = [_rotary(k_ref[:, hs(h)], cos, sin) * (R_DK ** -0.5) for h in heads]
    v_l = [v_ref[:, hs(h)].astype(BF16) for h in heads]
    s_l = [s_ref[h] for h in heads]
    scores = [_dg(q_l[h].astype(BF16), k_l[h].astype(BF16), _NT)
              * jnp.where(diff >= 0, jnp.exp(lg[h] * jnp.maximum(diff, 0.0)), 0.0) for h in heads]
    cross = [_dg((q_l[h] * jnp.exp(lg[h] * (idx + 1.0))).astype(BF16), s_l[h].astype(BF16), _NN) for h in heads]
    for h in heads:
        o = _dg(scores[h].astype(BF16), v_l[h], _NN) + cross[h]
        o_ref[:, hs(h)] = (_silu(g_ref[:, hs(h)]) * _rms_rows(o)).astype(o_ref.dtype)
    for h in heads:
        k_dec = (k_l[h] * jnp.exp(lg[h] * (chunk - 1.0 - idx))).astype(BF16)
        s_ref[h] = jnp.exp(lg[h] * jnp.full((1, 1), float(chunk), F32)) * s_l[h] + _dg(k_dec, v_l[h], _TN)

    @pl.when(c == nc - 1)
    def _():
        s_out_ref[0] = s_ref[...]


def retention_prompt(p_ret, cos, sin, log_gamma, batch, seq, chunk=R_CHUNK):
    nc = seq // chunk
    bw = BRANCH_WIDTH
    rows = lambda off: pl.BlockSpec((chunk, bw), functools.partial(lambda b, c, off: (b * nc + c, off), off=off))
    tab = pl.BlockSpec((chunk, R_DK // 2), lambda b, c: (c, 0))
    return pl.pallas_call(
        functools.partial(_ret_prompt_kernel, nc=nc, chunk=chunk),
        grid=(batch, nc),
        in_specs=[pl.BlockSpec(memory_space=pltpu.SMEM), rows(0), rows(1), rows(2), rows(3), tab, tab],
        out_specs=[pl.BlockSpec((chunk, bw), lambda b, c: (b * nc + c, 0)),
                   pl.BlockSpec((1, R_HEADS, R_DK, R_DK), lambda b, c: (b, 0, 0, 0))],
        out_shape=[jax.ShapeDtypeStruct((p_ret.shape[0], bw), BF16),
                   jax.ShapeDtypeStruct((batch, R_HEADS, R_DK, R_DK), F32)],
        scratch_shapes=[pltpu.VMEM((R_HEADS, R_DK, R_DK), F32)],
        compiler_params=_cparams(("parallel", "arbitrary")),
        name="ret_prompt",
    )(log_gamma, p_ret, p_ret, p_ret, p_ret, cos, sin)


def _ret_step_kernel(lg_ref, q_ref, k_ref, v_ref, g_ref, cos_ref, sin_ref, s_ref, *rest, bb):
    o_ref, s_out_ref = rest[-2:]
    h = pl.program_id(0)
    gamma = jnp.exp(lg_ref[h] * jnp.ones((1, 1), F32))
    cos, sin = cos_ref[...], sin_ref[...]
    q = _rotary(q_ref[...], cos, sin)
    k = _rotary(k_ref[...], cos, sin) * (R_DK ** -0.5)
    v = v_ref[...]
    eye = _eye(R_DK)
    q_cols = _dot_sel(eye, q, _NT)
    k_cols = _dot_sel(eye, k, _NT)
    for i in range(bb):
        s_new = gamma * s_ref[i, 0] + k_cols[:, i:i + 1] * v[i:i + 1, :]
        s_out_ref[i, 0] = s_new
        o = jnp.sum(q_cols[:, i:i + 1] * s_new, axis=0, keepdims=True)
        o_ref[i:i + 1, :] = (_silu(g_ref[i:i + 1, :]) * _rms_rows(o)).astype(o_ref.dtype)


def _alias_args(first_index, outputs):
    args, specs, aliases = [], [], {}
    for out_idx, a in outputs.items():
        if a is not None:
            aliases[first_index + len(args)] = out_idx
            args.append(a)
            specs.append(pl.BlockSpec(memory_space=pl.ANY))
    return args, specs, aliases


def retention_step(p_ret, row0, cos, sin, log_gamma, states, layer, o_all, s_all, bb=16):
    nb = states.shape[1]
    r0 = row0 // bb
    rows = lambda off: pl.BlockSpec((bb, R_DK), functools.partial(lambda h, j, off: (r0 + j, off + h), off=off))
    tab = pl.BlockSpec((1, R_DK // 2), lambda h, j: (0, 0))
    st = pl.BlockSpec((None, bb, 1, R_DK, R_DK), lambda h, j: (layer, j, h, 0, 0))
    extra, extra_specs, aliases = _alias_args(8, {0: o_all, 1: s_all})
    return pl.pallas_call(
        functools.partial(_ret_step_kernel, bb=bb),
        grid=(R_HEADS, nb // bb),
        in_specs=[pl.BlockSpec(memory_space=pltpu.SMEM), rows(0), rows(R_HEADS), rows(2 * R_HEADS),
                  rows(3 * R_HEADS), tab, tab, st, *extra_specs],
        out_specs=[pl.BlockSpec((bb, R_DK), lambda h, j: (r0 + j, h)), st],
        out_shape=[jax.ShapeDtypeStruct(o_all.shape, BF16), jax.ShapeDtypeStruct(states.shape, F32)],
        input_output_aliases=aliases,
        compiler_params=_cparams(("parallel", "parallel")),
        name="ret_step",
    )(log_gamma, p_ret, p_ret, p_ret, p_ret, cos, sin, states, *extra)


def _cmul(ar, ai, br, bi):
    return ar * br - ai * bi, ar * bi + ai * br


def s5_discretize(lam_re, lam_im, log_dt, b_re, b_im):
    dt = jnp.exp(log_dt)[:, None]
    mag = jnp.exp(lam_re * dt)
    lbr, lbi = mag * jnp.cos(lam_im * dt), mag * jnp.sin(lam_im * dt)
    den = lam_re * lam_re + lam_im * lam_im
    nr, ni = lbr - 1.0, lbi
    cr, ci = (nr * lam_re + ni * lam_im) / den, (ni * lam_re - nr * lam_im) / den
    bbr, bbi = _cmul(cr[..., None], ci[..., None], b_re, b_im)
    return dt, lbr, lbi, bbr, bbi


def _lam_power(lam_re, lam_im, dt, n):
    n = jnp.asarray(n, F32).reshape((-1,) + (1,) * lam_re.ndim)
    mag = jnp.exp(lam_re * dt * n)
    return mag * jnp.cos(lam_im * dt * n), mag * jnp.sin(lam_im * dt * n)


def s5_prompt_params(lam_re, lam_im, log_dt, b_re, b_im, c_re, c_im, d_skip):
    g, p, q = S5_GROUPS, S5_STATE, S5_CHUNK
    dt, lbr, lbi, bbr, bbi = s5_discretize(lam_re, lam_im, log_dt, b_re, b_im)
    pr, pi = _lam_power(lam_re, lam_im, dt, jnp.arange(q + 1))
    lb_r, lb_i = _cmul(pr[:q, :, :, None], pi[:q, :, :, None], bbr[None], bbi[None])
    hp = lax.Precision.HIGHEST
    kern = (jnp.einsum('gkp,tgpc->tgkc', c_re, lb_r, precision=hp)
            - jnp.einsum('gkp,tgpc->tgkc', c_im, lb_i, precision=hp))
    krow = kern.transpose(1, 3, 0, 2).reshape(g, S5_GROUP, q * S5_GROUP)
    er, ei = _cmul(pr[:q][::-1][:, :, :, None], pi[:q][::-1][:, :, :, None], bbr[None], bbi[None])
    w_re = er.transpose(1, 0, 3, 2).reshape(g, q * S5_GROUP, p)
    w_im = ei.transpose(1, 0, 3, 2).reshape(g, q * S5_GROUP, p)
    vr, vi = _cmul(c_re[None], c_im[None], pr[1:, :, None, :], pi[1:, :, None, :])
    v_re = vr.transpose(1, 3, 0, 2).reshape(g, p, q * S5_GROUP)
    v_im = (-vi).transpose(1, 3, 0, 2).reshape(g, p, q * S5_GROUP)

    def pair_cols(a):
        z = jnp.zeros_like(a)
        even = (jnp.arange(g) % 2 == 0)[:, None, None]
        return jnp.concatenate([jnp.where(even, a, z), jnp.where(even, z, a)], axis=-1)

    def pair_rows(a):
        z = jnp.zeros_like(a)
        even = (jnp.arange(g) % 2 == 0)[:, None, None]
        return jnp.concatenate([jnp.where(even, a, z), jnp.where(even, z, a)], axis=1)

    sr, si = _lam_power(lam_re, lam_im, dt, q * (2.0 ** jnp.arange(8)))
    lam_sc_re = sr.reshape(8, g // 2, 2 * p).transpose(1, 0, 2)
    lam_sc_im = si.reshape(8, g // 2, 2 * p).transpose(1, 0, 2)
    d_vec = jnp.tile(d_skip.reshape(g, 1, S5_GROUP), (1, q, 1)).reshape(g, 1, q * S5_GROUP)
    bf = lambda a: a.astype(BF16)
    return (krow, bf(pair_cols(w_re)), bf(pair_cols(w_im)), bf(pair_rows(v_re)), bf(pair_rows(v_im)),
            lam_sc_re, lam_sc_im, d_vec)


S5_TILE_GROUPS = 128 // S5_GROUP


def _toeplitz(krow, q, gw):
    blocks = [krow] + [jnp.concatenate([jnp.zeros((gw, s * gw), F32), krow[:, :(q - s) * gw]], axis=1)
                       for s in range(1, q)]
    return jnp.concatenate(blocks, axis=0).astype(BF16)


def _s5_prompt_kernel(u_ref, krow_ref, wre_ref, wim_ref, vre_ref, vim_ref, lre_ref, lim_ref, d_ref,
                      h_ref, fre_ref, fim_ref, ut_ref, ht_ref, tile_ref, *, nchb, nb):
    rows = nb * nchb
    q, gw, ng = S5_CHUNK, S5_GROUP, S5_TILE_GROUPS
    for s in range(q):
        xs = u_ref[pl.ds(s, rows, stride=q), :]
        for gg in range(ng):
            ut_ref[gg, :, s * gw:(s + 1) * gw] = xs[:, gg * gw:(gg + 1) * gw]

    pos = jnp.bitwise_and(lax.broadcasted_iota(jnp.int32, (rows, 2 * S5_STATE), 0), nchb - 1)
    first = pos == 0
    pairs = range(ng // 2)
    ub_l = [(ut_ref[2 * pr].astype(BF16), ut_ref[2 * pr + 1].astype(BF16)) for pr in pairs]
    e_re = [_dg(ub_l[pr][0], wre_ref[2 * pr], _NN) + _dg(ub_l[pr][1], wre_ref[2 * pr + 1], _NN) for pr in pairs]
    e_im = [_dg(ub_l[pr][0], wim_ref[2 * pr], _NN) + _dg(ub_l[pr][1], wim_ref[2 * pr + 1], _NN) for pr in pairs]
    d, k = 1, 0
    while d < nchb:
        keep = pos >= d
        for pr in pairs:
            lr, li = lre_ref[pr, k:k + 1, :], lim_ref[pr, k:k + 1, :]
            sr, si = pltpu.roll(e_re[pr], d, 0), pltpu.roll(e_im[pr], d, 0)
            e_re[pr], e_im[pr] = (e_re[pr] + jnp.where(keep, lr * sr - li * si, 0.0),
                                  e_im[pr] + jnp.where(keep, lr * si + li * sr, 0.0))
        d, k = d * 2, k + 1
    for pr in pairs:
        for b in range(nb):
            r = (b + 1) * nchb - 1
            fre_ref[pr, b:b + 1, :] = e_re[pr][r:r + 1, :]
            fim_ref[pr, b:b + 1, :] = e_im[pr][r:r + 1, :]
    for pr in pairs:
        xc_re = jnp.where(first, 0.0, pltpu.roll(e_re[pr], 1, 0)).astype(BF16)
        xc_im = jnp.where(first, 0.0, pltpu.roll(e_im[pr], 1, 0)).astype(BF16)
        for gi in range(2):
            gg = 2 * pr + gi
            y = (_dg(ub_l[pr][gi], _toeplitz(krow_ref[gg], q, gw), _NN)
                 + _dg(xc_re, vre_ref[gg], _NN) + _dg(xc_im, vim_ref[gg], _NN)
                 ) + ut_ref[gg] * d_ref[gg]
            ht_ref[gg] = _gelu_tanh(y)
    for s in range(q):
        for gg in range(ng):
            tile_ref[:, gg * gw:(gg + 1) * gw] = ht_ref[gg, :, s * gw:(s + 1) * gw]
        h_ref[pl.ds(s, rows, stride=q), :] = tile_ref[...]


def s5_prompt(p_s5, params, layer, batch, seq):
    krow, w_re, w_im, v_re, v_im, l_re, l_im, d_vec = params
    ng, qc, p2 = S5_TILE_GROUPS, S5_CHUNK * S5_GROUP, 2 * S5_STATE
    nchb = seq // S5_CHUNK
    rows = batch * nchb
    tp = batch * seq
    nt = BRANCH_WIDTH // 128
    grp = lambda n, *tail: pl.BlockSpec((None, n) + tail, lambda j: (layer, j) + (0,) * len(tail))
    fin = pl.BlockSpec((ng // 2, batch, p2), lambda j: (j, 0, 0))
    return pl.pallas_call(
        functools.partial(_s5_prompt_kernel, nchb=nchb, nb=batch),
        grid=(nt,),
        in_specs=[pl.BlockSpec((tp, 128), lambda j: (0, j)), grp(ng, S5_GROUP, qc), grp(ng, qc, p2), grp(ng, qc, p2),
                  grp(ng, p2, qc), grp(ng, p2, qc), grp(ng // 2, 8, p2), grp(ng // 2, 8, p2), grp(ng, 1, qc)],
        out_specs=[pl.BlockSpec((tp, 128), lambda j: (0, j)), fin, fin],
        out_shape=[jax.ShapeDtypeStruct(p_s5.shape, F32),
                   jax.ShapeDtypeStruct((S5_GROUPS // 2, batch, p2), F32),
                   jax.ShapeDtypeStruct((S5_GROUPS // 2, batch, p2), F32)],
        scratch_shapes=[pltpu.VMEM((ng, rows, qc), F32), pltpu.VMEM((ng, rows, qc), F32),
                        pltpu.VMEM((rows, 128), F32)],
        compiler_params=_cparams(("parallel",)),
        name="s5_prompt",
    )(p_s5, krow, w_re, w_im, v_re, v_im, l_re, l_im, d_vec)


def s5_step_params(lam_re, lam_im, log_dt, b_re, b_im, c_re, c_im, d_skip):
    g, p = S5_GROUPS, S5_STATE
    _, lbr, lbi, bbr, bbi = s5_discretize(lam_re, lam_im, log_dt, b_re, b_im)
    eye8 = jnp.eye(8, dtype=F32)

    def bd_in(bb):
        x = bb.reshape(8, 8, p, S5_GROUP)
        return jnp.einsum('jgpc,gh->jgchp', x, eye8).reshape(8, 8 * S5_GROUP, 8 * p)

    def bd_out(cc):
        x = cc.reshape(8, 8, S5_GROUP, p)
        return jnp.einsum('jgkp,gh->jgphk', x, eye8).reshape(8, 8 * p, 8 * S5_GROUP)

    return (bd_in(bbr), bd_in(bbi), bd_out(c_re), bd_out(-c_im), lbr.reshape(1, g * p), lbi.reshape(1, g * p),
            d_skip.reshape(1, g * S5_GROUP))


def _s5_step_kernel(u_ref, xr_ref, xi_ref, bre_ref, bim_ref, cre_ref, cim_ref, lr_ref, li_ref, d_ref, *rest):
    h_ref, nr_ref, ni_ref = rest[-3:]
    u = u_ref[...]
    lr, li = lr_ref[...], li_ref[...]
    xr, xi = xr_ref[...], xi_ref[...]
    nr = lr * xr - li * xi + _dot3(u, bre_ref[0])
    ni = lr * xi + li * xr + _dot3(u, bim_ref[0])
    nr_ref[...] = nr
    ni_ref[...] = ni
    y = _dot3(nr, cre_ref[0]) + _dot3(ni, cim_ref[0]) + u * d_ref[...]
    h_ref[...] = _gelu_tanh(y)


def s5_step(p_s5, row0, x_re, x_im, params, layer, h_all):
    b_re, b_im, c_re, c_im, l_re, l_im, d_vec = params
    nb, gp = x_re.shape
    lt = 8 * S5_GROUP
    st = 8 * S5_STATE
    r0 = row0 // nb
    u_spec = pl.BlockSpec((nb, lt), lambda j: (r0, j))
    x_spec = pl.BlockSpec((nb, st), lambda j: (0, j))
    mat = lambda a, b: pl.BlockSpec((None, 1, a, b), lambda j: (layer, j, 0, 0))
    vec = lambda n: pl.BlockSpec((None, 1, n), lambda j: (layer, 0, j))
    extra, extra_specs, aliases = _alias_args(10, {0: h_all})
    return pl.pallas_call(
        _s5_step_kernel,
        grid=(gp // st,),
        in_specs=[u_spec, x_spec, x_spec, mat(lt, st), mat(lt, st), mat(st, lt), mat(st, lt),
                  vec(st), vec(st), vec(lt), *extra_specs],
        out_specs=[u_spec, x_spec, x_spec],
        out_shape=[jax.ShapeDtypeStruct(h_all.shape, F32), jax.ShapeDtypeStruct((nb, gp), F32),
                   jax.ShapeDtypeStruct((nb, gp), F32)],
        input_output_aliases=aliases,
        compiler_params=_cparams(("parallel",)),
        name="s5_step",
    )(p_s5, x_re, x_im, b_re, b_im, c_re, c_im, l_re, l_im, d_vec, *extra)


def _l2norm_rows(x):
    return x * lax.rsqrt(jnp.sum(x * x, axis=-1, keepdims=True) + NORM_EPS)


def _gdn_prompt_kernel(qkv_ref, z_ref, ab_ref, cw_ref, alog_ref, dtb_ref, nw_ref, o_ref, s_out_ref,
                       xbuf_ref, s_ref, m_s, sol_s, qe_s, qkd_s, kdec_s, egl_s, *, nc, chunk):
    c = pl.program_id(1)
    bw = BRANCH_WIDTH
    halo = 8
    heads = range(G_HEADS)

    @pl.when(c == 0)
    def _():
        xbuf_ref[0:halo, :] = jnp.zeros((halo, 3 * bw), F32)
        s_ref[...] = jnp.zeros_like(s_ref)
        for ref in (m_s, sol_s, qe_s, qkd_s, kdec_s, egl_s):
            ref[...] = jnp.zeros_like(ref)

    m_l = [m_s[h] for h in heads]
    sol_l = [sol_s[h] for h in heads]
    qe_l = [qe_s[h] for h in heads]
    qkd_l = [qkd_s[h] for h in heads]
    kdec_l = [kdec_s[h] for h in heads]
    egl_l = [egl_s[h] for h in heads]

    x = qkv_ref[...]
    xbuf_ref[halo:halo + chunk, :] = x
    conv = x * cw_ref[G_CONV - 1:G_CONV, :]
    for i in range(1, G_CONV):
        conv = conv + xbuf_ref[halo - i:halo - i + chunk, :] * cw_ref[G_CONV - 1 - i:G_CONV - i, :]
    xbuf_ref[0:halo, :] = x[chunk - halo:chunk, :]
    qkv = _silu(conv)

    ab = ab_ref[...]
    g_all = -jnp.exp(alog_ref[...]) * _softplus(ab + dtb_ref[...])
    beta_all = _sigmoid(ab)
    ri = lax.broadcasted_iota(jnp.int32, (chunk, chunk), 0)
    ci = lax.broadcasted_iota(jnp.int32, (chunk, chunk), 1)
    causal = ri >= ci
    strict = ri > ci
    ltri = jnp.where(causal, 1.0, 0.0).astype(BF16)
    gcum = _dot_sel(ltri, g_all)
    gcum_t = _dot_sel(_eye(128), gcum, _NT)
    z = z_ref[...]
    nw = nw_ref[...]

    def prepare(h):
        q = _l2norm_rows(qkv[:, h * G_DK:(h + 1) * G_DK]) * (G_DK ** -0.5)
        k = _l2norm_rows(qkv[:, bw + h * G_DK:bw + (h + 1) * G_DK])
        v = qkv[:, 2 * bw + h * G_DK:2 * bw + (h + 1) * G_DK]
        gc = gcum[:, h:h + 1]
        gr = gcum_t[h:h + 1, :]
        beta = beta_all[:, G_HEADS + h:G_HEADS + h + 1]
        decay = jnp.where(causal, jnp.exp(jnp.where(causal, gc - gr, 0.0)), 0.0)
        kb = k.astype(BF16)
        qk_kk = _dg(jnp.concatenate([kb, q.astype(BF16)], axis=0), kb, _NT)
        eg = jnp.exp(gc)
        g_last = gc[chunk - 1:chunk, :]
        qe_s[h] = (q * eg).astype(BF16)
        qkd_s[h] = (qk_kk[chunk:] * decay).astype(BF16)
        kdec_s[h] = (k * jnp.exp(g_last - gc)).astype(BF16)
        egl_s[h] = jnp.broadcast_to(jnp.exp(g_last), (1, G_DK))
        m_s[h] = jnp.where(strict, -(beta * qk_kk[:chunk] * decay), 0.0)
        sol_s[h] = jnp.concatenate([beta * v, (beta * eg) * k], axis=-1)

    todo = list(heads)
    span = 1
    while span < chunk:
        for h in heads:
            mb = m_l[h].astype(BF16)
            if 2 * span < chunk:
                sol_hi, sol_lo = _split2(sol_l[h])
                r = _dg(mb, jnp.concatenate([sol_hi, mb], axis=-1), _NN)
                sol_l[h] = sol_l[h] + (r[:, :2 * G_DK] + _dg(mb, sol_lo, _NN))
                m_l[h] = r[:, 2 * G_DK:]
            else:
                sol_l[h] = sol_l[h] + _dot2(mb, sol_l[h])
        prepare(todo.pop(0))
        span *= 2
    s_l = [s_ref[h] for h in heads]
    ks_qs_l = [_dg(jnp.concatenate([sol_l[h][:, G_DK:].astype(BF16), qe_l[h]], axis=0), s_l[h].astype(BF16), _NN)
               for h in heads]
    wb_l = [(sol_l[h][:, :G_DK] - ks_qs_l[h][:chunk]).astype(BF16) for h in heads]
    while todo:
        prepare(todo.pop(0))
    for h in heads:
        o = ks_qs_l[h][chunk:] + _dg(qkd_l[h], wb_l[h], _NN)
        zh = z[:, h * G_DK:(h + 1) * G_DK]
        o_ref[:, h * G_DK:(h + 1) * G_DK] = (_rms_rows(o) * nw * _silu(zh)).astype(o_ref.dtype)
    for h in heads:
        s_ref[h] = egl_l[h] * s_l[h] + _dg(kdec_l[h], wb_l[h], _TN)

    @pl.when(c == nc)
    def _():
        s_out_ref[0] = s_ref[...]


def gdn_prompt(p_qkv, p_z, p_ab, conv_w, a_log, dt_bias, norm_w, batch, seq, chunk=G_CHUNK):
    nc = seq // chunk
    bw = BRANCH_WIDTH
    pad = lambda x: jnp.zeros((1, 128), F32).at[0, :G_HEADS].set(x)
    full = lambda shape: pl.BlockSpec(shape, lambda b, c: (0,) * len(shape))
    cur = lambda width: pl.BlockSpec((chunk, width), lambda b, c: (b * nc + jnp.minimum(c, nc - 1), 0))
    prev = lambda width: pl.BlockSpec((chunk, width), lambda b, c: (b * nc + jnp.maximum(c - 1, 0), 0))
    hd = lambda width, dtype: pltpu.VMEM((G_HEADS, chunk, width), dtype)
    return pl.pallas_call(
        functools.partial(_gdn_prompt_kernel, nc=nc, chunk=chunk),
        grid=(batch, nc + 1),
        in_specs=[cur(3 * bw), prev(bw), cur(128),
                  full((G_CONV, 3 * bw)), full((1, 128)), full((1, 128)), full((1, G_DK))],
        out_specs=[prev(bw), pl.BlockSpec((1, G_HEADS, G_DK, G_DK), lambda b, c: (b, 0, 0, 0))],
        out_shape=[jax.ShapeDtypeStruct((p_qkv.shape[0], bw), BF16),
                   jax.ShapeDtypeStruct((batch, G_HEADS, G_DK, G_DK), F32)],
        scratch_shapes=[pltpu.VMEM((8 + chunk, 3 * bw), F32), pltpu.VMEM((G_HEADS, G_DK, G_DK), F32),
                        hd(chunk, F32), hd(2 * G_DK, F32), hd(G_DK, BF16), hd(chunk, BF16), hd(G_DK, BF16),
                        pltpu.VMEM((G_HEADS, 1, G_DK), F32)],
        compiler_params=_cparams(("parallel", "arbitrary")),
        name="gdn_prompt",
    )(p_qkv, p_z, p_ab, conv_w, pad(a_log), pad(dt_bias), norm_w.reshape(1, G_DK))


def _gdn_step_kernel(qkv_ref, z_ref, ab_ref, cbuf_ref, cw_ref, alog_ref, dtb_ref, nw_ref, s_ref, *rest, bb):
    o_ref, cbuf_out_ref, s_out_ref = rest[-3:]
    bw = BRANCH_WIDTH
    x = qkv_ref[...]
    conv = x * cw_ref[G_CONV - 1:G_CONV, :]
    for i in range(G_CONV - 1):
        conv = conv + cbuf_ref[i] * cw_ref[i:i + 1, :]
    for i in range(G_CONV - 2):
        cbuf_out_ref[i] = cbuf_ref[i + 1]
    cbuf_out_ref[G_CONV - 2] = x
    qkv = _silu(conv)
    ab = ab_ref[...]
    g_all = -jnp.exp(alog_ref[...]) * _softplus(ab + dtb_ref[...])
    beta_all = _sigmoid(ab)
    eye = _eye(G_DK)
    z = z_ref[...]
    nw = nw_ref[...]
    for h in range(G_HEADS):
        q = _l2norm_rows(qkv[:, h * G_DK:(h + 1) * G_DK]) * (G_DK ** -0.5)
        k = _l2norm_rows(qkv[:, bw + h * G_DK:bw + (h + 1) * G_DK])
        v = qkv[:, 2 * bw + h * G_DK:2 * bw + (h + 1) * G_DK]
        eg = jnp.exp(g_all[:, h:h + 1])
        beta = beta_all[:, G_HEADS + h:G_HEADS + h + 1]
        q_cols = _dot_sel(eye, q, _NT)
        k_cols = _dot_sel(eye, k, _NT)
        rng = range(bb)
        s_l = [s_ref[i, h] for i in rng]
        ks_l = [jnp.sum(k_cols[:, i:i + 1] * s_l[i], axis=0, keepdims=True) for i in rng]
        w_l = [beta[i:i + 1, :] * (v[i:i + 1, :] - eg[i:i + 1, :] * ks_l[i]) for i in rng]
        sn_l = [eg[i:i + 1, :] * s_l[i] + k_cols[:, i:i + 1] * w_l[i] for i in rng]
        for i in rng:
            s_out_ref[i, h] = sn_l[i]
        o = jnp.concatenate([jnp.sum(q_cols[:, i:i + 1] * sn_l[i], axis=0, keepdims=True) for i in rng], axis=0)
        zh = z[:, h * G_DK:(h + 1) * G_DK]
        o_ref[:, h * G_DK:(h + 1) * G_DK] = (_rms_rows(o) * nw * _silu(zh)).astype(o_ref.dtype)


def gdn_step(p_qkv, p_z, p_ab, row0, conv_buf_t, states, layer, conv_w, a_log, dt_bias, norm_w, o_all, s_all, bb=8):
    nb = states.shape[1]
    bw = BRANCH_WIDTH
    r0 = row0 // bb
    pad = lambda x: jnp.zeros((1, 128), F32).at[0, :G_HEADS].set(x)
    full = lambda shape: pl.BlockSpec(shape, lambda j: (0,) * len(shape))
    cb = pl.BlockSpec((G_CONV - 1, bb, 3 * bw), lambda j: (0, j, 0))
    st = pl.BlockSpec((None, bb, G_HEADS, G_DK, G_DK), lambda j: (layer, j, 0, 0, 0))
    extra, extra_specs, aliases = _alias_args(9, {0: o_all, 2: s_all})
    return pl.pallas_call(
        functools.partial(_gdn_step_kernel, bb=bb),
        grid=(nb // bb,),
        in_specs=[pl.BlockSpec((bb, 3 * bw), lambda j: (r0 + j, 0)),
                  pl.BlockSpec((bb, bw), lambda j: (r0 + j, 0)),
                  pl.BlockSpec((bb, 128), lambda j: (r0 + j, 0)),
                  cb, full((G_CONV, 3 * bw)), full((1, 128)), full((1, 128)), full((1, G_DK)), st, *extra_specs],
        out_specs=[pl.BlockSpec((bb, bw), lambda j: (r0 + j, 0)), cb, st],
        out_shape=[jax.ShapeDtypeStruct(o_all.shape, BF16), jax.ShapeDtypeStruct(conv_buf_t.shape, F32),
                   jax.ShapeDtypeStruct(states.shape, F32)],
        input_output_aliases=aliases,
        compiler_params=_cparams(("parallel",)),
        name="gdn_step",
    )(p_qkv, p_z, p_ab, conv_buf_t, conv_w, pad(a_log), pad(dt_bias), norm_w.reshape(1, G_DK), states, *extra)


def _rope_tables(pos):
    half = R_DK // 2
    inv = ROPE_BASE ** (-jnp.arange(half, dtype=F32) / half)
    ang = pos.astype(F32)[:, None] * inv[None, :]
    return jnp.cos(ang), jnp.sin(ang)


def kernel(x_prompt, x_sample, state_ret, state_s5_re, state_s5_im, state_gdn, state_gdn_conv, norm_mix_w, w_in, s5_lambda_re, s5_lambda_im, s5_log_dt, s5_b_re, s5_b_im, s5_c_re, s5_c_im, s5_d, s5_w_glu, s5_b_glu, gdn_conv_w, gdn_a_log, gdn_dt_bias, gdn_norm_w, w_branch, w_out, norm_ffn_w, w_gate_ffn, w_up_ffn, w_down_ffn, norm_final_w):
    bp, lp, d = x_prompt.shape
    nb = x_sample.shape[0]
    depth = w_in.shape[0]
    tp = bp * lp
    bw = BRANCH_WIDTH
    x = jnp.concatenate([x_prompt.reshape(tp, d), x_sample.reshape(nb, d)], axis=0)

    cos_p, sin_p = _rope_tables(jnp.arange(lp, dtype=jnp.int32))
    cos_s, sin_s = _rope_tables(PAST_LEN + jnp.arange(1, dtype=jnp.int32))
    log_gamma = jnp.log1p(-jnp.exp2(-5.0 - jnp.arange(R_HEADS, dtype=F32)))

    o_su, o_gqkv, o_ga, o_gz, o_gates = 4 * bw, 5 * bw, 8 * bw, 8 * bw + 2 * G_HEADS, 9 * bw + 2 * G_HEADS
    w_in_t = jnp.swapaxes(w_in, 1, 2)
    n_in = w_in_t.shape[1]
    cast_rows = max(r for r in range(ROW_ALIGN, 512 + 1, ROW_ALIGN) if n_in % r == 0)
    w_in_b = cast_layer(w_in_t, 0, cast_rows, d)[None]
    w_gate_b, w_up_b, w_down_b = w_gate_ffn[0].astype(BF16), w_up_ffn[0].astype(BF16), w_down_ffn[0].astype(BF16)
    w_branch_b, w_out_b, w_glu_b = w_branch.astype(BF16), w_out.astype(BF16), s5_w_glu.astype(BF16)
    s5_raw = (s5_lambda_re, s5_lambda_im, s5_log_dt, s5_b_re, s5_b_im, s5_c_re, s5_c_im, s5_d)
    s5_pp = jax.vmap(s5_prompt_params)(*s5_raw)
    s5_sp = jax.vmap(s5_step_params)(*s5_raw)

    new_p = [[], [], [], [], []]
    new_s = [[], [], []]
    ret_s_all = gdn_s_all = None
    xn = rmsnorm_rows(x, norm_mix_w[0], BF16)
    y_final = None
    for l in range(depth):
        p_ret = matmul(xn, w_in_b, 0, 0, o_su, tn=1024, name="proj_ret")
        p_s5 = matmul(xn, w_in_b, 0, o_su, bw, tn=1024, name="proj_s5")
        p_qkv = matmul(xn, w_in_b, 0, o_gqkv, 3 * bw, tn=1024, name="proj_qkv")
        p_z, p_ab = matmul_with_side(xn, w_in_b, 0, o_gz, bw, 512, o_ga, LANE, name="proj_z")

        r_all, ret_p = retention_prompt(p_ret, cos_p, sin_p, log_gamma, bp, lp)
        r_all, ret_s_all = retention_step(p_ret, tp, cos_s, sin_s, log_gamma, state_ret, l, r_all, ret_s_all)

        h_all, f_re, f_im = s5_prompt(p_s5, s5_pp, l, bp, lp)
        fin = lambda f: f[:, :bp, :].reshape(S5_GROUPS // 2, bp, 2, S5_STATE).transpose(1, 0, 2, 3).reshape(
            bp, S5_GROUPS, S5_STATE)
        h_all, s5re_s, s5im_s = s5_step(p_s5, tp, state_s5_re[l].reshape(nb, -1), state_s5_im[l].reshape(nb, -1),
                                        s5_sp, l, h_all)
        s_all = s5_glu(h_all, w_glu_b, l, s5_b_glu[l])

        g_all, gdn_p = gdn_prompt(p_qkv, p_z, p_ab, gdn_conv_w[l], gdn_a_log[l], gdn_dt_bias[l], gdn_norm_w[l], bp, lp)
        g_all, conv_s_t, gdn_s_all = gdn_step(p_qkv, p_z, p_ab, tp, state_gdn_conv[l].transpose(1, 0, 2), state_gdn, l,
                                              gdn_conv_w[l], gdn_a_log[l], gdn_dt_bias[l], gdn_norm_w[l],
                                              g_all, gdn_s_all)
        conv_p = jnp.stack([p_qkv[(b + 1) * lp - (G_CONV - 1):(b + 1) * lp] for b in range(bp)], axis=0)

        merged = merge_branches(xn, r_all, s_all, g_all, w_in_b, o_gates, w_branch_b, l)
        h, hn = out_proj(merged, w_out_b, l, x, norm_ffn_w[l])
        last = l == depth - 1
        if last:
            x, y_final = ffn(hn, h, w_gate_b, w_up_b, w_down_b, norm_final_w, F32)
        else:
            x, xn, w_gate_b, w_up_b, w_down_b, w_in_next = ffn(
                hn, h, w_gate_b, w_up_b, w_down_b, norm_mix_w[l + 1], BF16,
                cast_next=(l + 1, w_gate_ffn, w_up_ffn, w_down_ffn, w_in_t))
            w_in_b = w_in_next[None]

        for lst, val in zip(new_p, (ret_p, fin(f_re), fin(f_im), gdn_p, conv_p)):
            lst.append(val)
        for lst, val in zip(new_s, (s5re_s.reshape(nb, S5_GROUPS, S5_STATE),
                                    s5im_s.reshape(nb, S5_GROUPS, S5_STATE), conv_s_t.transpose(1, 0, 2))):
            lst.append(val)

    y_prompt = y_final[:tp].reshape(bp, lp, d)
    y_sample = y_final[tp:].reshape(nb, 1, d)
    outs_p = [jnp.stack(a, 0) for a in new_p]
    s5re_s, s5im_s, conv_s = [jnp.stack(a, 0) for a in new_s]
    return (y_prompt, y_sample, *outs_p, ret_s_all, s5re_s, s5im_s, gdn_s_all, conv_s)
```
